```python
import math
import jax, jax.numpy as jnp
from jax import lax
import numpy as np

D_MODEL = 1024
BATCH = 4
SEQ = 8192
DEPTH = 2
DEC_BATCH = 32
DEC_SEQ = 8
PAST_LEN = 16384
PAGE_SIZE = 128

N_META = 16
D_RNN = D_MODEL
N_RNN_BLOCKS = 8
RNN_BLOCK = D_RNN // N_RNN_BLOCKS
RNN_CONV = 4
LRU_C = 8.0
N_HEADS = 8
HEAD_DIM = D_MODEL // N_HEADS
N_KV_HEADS = 4
KV_GROUP = N_HEADS // N_KV_HEADS
N_IDX_HEADS = 8
IDX_DIM = 64
TOPK_MAX = 256
Q_BLOCK = 128
D_FF = 3 * D_MODEL
FFN_CONV = 3
ALPHA = (2.0 * DEPTH) ** 0.25
BETA = (8.0 * DEPTH) ** -0.25
LN_EPS = 1e-5
NEG = -1e30
D_IN = 2 * D_RNN + (N_HEADS + 2 * N_KV_HEADS) * HEAD_DIM + N_IDX_HEADS * IDX_DIM + IDX_DIM + N_IDX_HEADS + 2 * D_MODEL

kernel_name = 'hawk_dsa_gated_hybrid_step'


def layer_norm(x, g, b):
    xf = x.astype(jnp.float32)
    mu = jnp.mean(xf, axis=-1, keepdims=True)
    var = jnp.mean(jnp.square(xf - mu), axis=-1, keepdims=True)
    return ((xf - mu) * lax.rsqrt(var + LN_EPS) * g + b).astype(x.dtype)


def split_in(proj):
    sizes = (D_RNN, D_RNN, N_HEADS * HEAD_DIM, N_KV_HEADS * HEAD_DIM, N_KV_HEADS * HEAD_DIM,
             N_IDX_HEADS * IDX_DIM, IDX_DIM, N_IDX_HEADS, D_MODEL, D_MODEL)
    offs = []
    acc = 0
    for s in sizes[:-1]:
        acc += s
        offs.append(acc)
    return jnp.split(proj, offs, axis=-1)


def causal_dwconv(x, buf, w, b):
    width = w.shape[0]
    T = x.shape[1]
    xp = jnp.concatenate([buf.astype(x.dtype), x], axis=1)
    y = b + sum(w[j] * xp[:, j:j + T] for j in range(width))
    return y.astype(x.dtype), xp[:, T:]


def linear_scan(a, bx, h0):
    bx = bx.at[:, 0].add(a[:, 0] * h0)
    def comb(c1, c2):
        a1, b1 = c1
        a2, b2 = c2
        return a1 * a2, a2 * b1 + b2
    _, h = lax.associative_scan(comb, (a, bx), axis=1)
    return h


def rglru(xc, h0, wa, ba, wx, bx, lam):
    B, T, _ = xc.shape
    xf = xc.astype(jnp.float32)
    xb = xf.reshape(B, T, N_RNN_BLOCKS, RNN_BLOCK)
    r = jax.nn.sigmoid(jnp.einsum('btnc,ncd->btnd', xb, wa.astype(jnp.float32)).reshape(B, T, D_RNN) + ba)
    i = jax.nn.sigmoid(jnp.einsum('btnc,ncd->btnd', xb, wx.astype(jnp.float32)).reshape(B, T, D_RNN) + bx)
    log_a = -LRU_C * r * jax.nn.softplus(-lam.astype(jnp.float32))
    a = jnp.exp(log_a)
    inp = jnp.sqrt(-jnp.expm1(2.0 * log_a)) * (i * xf)
    h = linear_scan(a, inp, h0.astype(jnp.float32))
    return h, h[:, -1]


def gather_rows(arr, idx):
    return jax.vmap(lambda a, i: a[i])(arr, idx)


def dsa_attention(q, qi, wi, q_pos, ki, topk, gather_kv):
    L = ki.shape[1]
    rel = jax.nn.relu(jnp.einsum('bqhd,bld->bqhl', qi.astype(jnp.float32), ki.astype(jnp.float32)) * IDX_DIM ** -0.5)
    score = jnp.einsum('bqh,bqhl->bql', wi.astype(jnp.float32) * N_IDX_HEADS ** -0.5, rel)
    allowed = jnp.arange(L, dtype=jnp.int32)[None, :] <= q_pos[:, None]
    score = jnp.where(allowed[None], score, NEG)
    _, idx = lax.top_k(score, topk)
    valid = idx <= q_pos[None, :, None]
    k_sel, v_sel = gather_kv(idx)
    B, Q = q.shape[:2]
    qg = q.reshape(B, Q, N_KV_HEADS, KV_GROUP, HEAD_DIM).astype(jnp.float32)
    s = jnp.einsum('bqhgd,bqkhd->bqhgk', qg, k_sel.astype(jnp.float32)) * HEAD_DIM ** -0.5
    s = jnp.where(valid[:, :, None, None, :], s, NEG)
    p = jax.nn.softmax(s, axis=-1)
    o = jnp.einsum('bqhgk,bqkhd->bqhgd', p, v_sel.astype(jnp.float32))
    return o.reshape(B, Q, N_HEADS * HEAD_DIM).astype(q.dtype)


def prompt_attention(q, k, v, qi, ki, wi):
    B, T = q.shape[:2]
    topk = min(TOPK_MAX, T // 4)
    n_blk = -(-T // Q_BLOCK)
    pad = n_blk * Q_BLOCK - T
    def blocks(a):
        a = jnp.pad(a, [(0, 0), (0, pad)] + [(0, 0)] * (a.ndim - 2))
        return a.reshape((B, n_blk, Q_BLOCK) + a.shape[2:]).swapaxes(0, 1)
    pos = jnp.arange(n_blk * Q_BLOCK, dtype=jnp.int32).reshape(n_blk, Q_BLOCK)
    gkv = lambda idx: (gather_rows(k, idx), gather_rows(v, idx))
    def one(args):
        qb, qib, wib, pb = args
        return dsa_attention(qb, qib, wib, pb, ki, topk, gkv)
    o = lax.map(one, (blocks(q), blocks(qi), blocks(wi), pos))
    return o.swapaxes(0, 1).reshape(B, n_blk * Q_BLOCK, -1)[:, :T]


def make_sample_attention(cache_k_l, cache_v_l, cache_ik_l, page_table):
    def attend(q, k, v, qi, ki, wi):
        B, S = q.shape[:2]
        past_len = page_table.shape[1] * PAGE_SIZE
        past_ki = cache_ik_l[page_table].reshape(B, past_len, IDX_DIM)
        ki_all = jnp.concatenate([past_ki.astype(ki.dtype), ki], axis=1)
        topk = min(TOPK_MAX, (past_len + S) // 4)
        q_pos = past_len + jnp.arange(S, dtype=jnp.int32)
        def gkv(idx):
            in_past = (idx < past_len)[..., None, None]
            pidx = jnp.minimum(idx, past_len - 1)
            phys = gather_rows(page_table, pidx // PAGE_SIZE)
            slot = pidx % PAGE_SIZE
            nidx = jnp.clip(idx - past_len, 0, S - 1)
            k_sel = jnp.where(in_past, cache_k_l[phys, slot].astype(k.dtype), gather_rows(k, nidx))
            v_sel = jnp.where(in_past, cache_v_l[phys, slot].astype(v.dtype), gather_rows(v, nidx))
            return k_sel, v_sel
        return dsa_attention(q, qi, wi, q_pos, ki_all, topk, gkv)
    return attend


def trunk_layer(x, p, attention, lru_h0, lru_buf, ffn_buf):
    (w_in, rnn_conv_w, rnn_conv_b, lru_wa, lru_ba, lru_wx, lru_bx, lru_lambda,
     w_branch_a, w_branch_b, w_out, ln1_g, ln1_b,
     ffn_w_up, ffn_conv_w, ffn_conv_b, ffn_w_down, ln2_g, ln2_b) = p
    B, T, _ = x.shape
    xr, gr, q, k, v, qi, ki, wi, ga, gb = split_in(x @ w_in)
    xc, new_lru_buf = causal_dwconv(xr, lru_buf, rnn_conv_w, rnn_conv_b)
    h, h_last = rglru(xc, lru_h0, lru_wa, lru_ba, lru_wx, lru_bx, lru_lambda)
    ya = (h.astype(x.dtype) * jax.nn.gelu(gr)) @ w_branch_a
    q = q.reshape(B, T, N_HEADS, HEAD_DIM)
    k = k.reshape(B, T, N_KV_HEADS, HEAD_DIM)
    v = v.reshape(B, T, N_KV_HEADS, HEAD_DIM)
    qi = qi.reshape(B, T, N_IDX_HEADS, IDX_DIM)
    yb = attention(q, k, v, qi, ki, wi) @ w_branch_b
    mix = (jax.nn.sigmoid(ga) * ya + jax.nn.sigmoid(gb) * yb) @ w_out
    x = layer_norm(ALPHA * x + mix, ln1_g, ln1_b)
    u = x @ ffn_w_up
    uc, new_ffn_buf = causal_dwconv(u, ffn_buf, ffn_conv_w, ffn_conv_b)
    g, val = jnp.split(uc, 2, axis=-1)
    f = (jax.nn.gelu(g) * val) @ ffn_w_down
    x = layer_norm(ALPHA * x + f, ln2_g, ln2_b)
    return x, (k, v, ki, h_last.astype(x.dtype), new_lru_buf, new_ffn_buf)


def setup_inputs(seed: int = 0) -> dict:
    key = jax.random.key(seed)
    ks = jax.random.split(key, 32)
    f32 = jnp.float32
    n_pages = PAST_LEN // PAGE_SIZE
    n_used = DEC_BATCH * n_pages
    n_pool = n_used + max(1, n_used // 4)
    def nrm(k, shape, scale):
        return scale * jax.random.normal(k, shape, f32)
    a0 = jax.random.uniform(ks[17], (DEPTH, D_RNN), f32, 0.9, 0.999) ** (1.0 / LRU_C)
    return {
        'x_prompt': nrm(ks[0], (BATCH, SEQ, D_MODEL), 1.0),
        'x_sample': nrm(ks[1], (DEC_BATCH, DEC_SEQ, D_MODEL), 1.0),
        'cache_k': nrm(ks[2], (DEPTH, n_pool, PAGE_SIZE, N_KV_HEADS, HEAD_DIM), 1.0),
        'cache_v': nrm(ks[3], (DEPTH, n_pool, PAGE_SIZE, N_KV_HEADS, HEAD_DIM), 1.0),
        'cache_idx_k': nrm(ks[4], (DEPTH, n_pool, PAGE_SIZE, IDX_DIM), 1.0),
        'state_lru_h': nrm(ks[5], (DEPTH, DEC_BATCH, D_RNN), 0.5),
        'state_lru_conv': nrm(ks[6], (DEPTH, DEC_BATCH, RNN_CONV - 1, D_RNN), 1.0),
        'state_ffn_conv': nrm(ks[7], (DEPTH, DEC_BATCH, FFN_CONV - 1, 2 * D_FF), 1.0),
        'page_table': jax.random.permutation(ks[8], n_pool)[:n_used].reshape(DEC_BATCH, n_pages).astype(jnp.int32),
        'meta_tokens': nrm(ks[9], (N_META, D_MODEL), 1.0),
        'w_in': nrm(ks[10], (DEPTH, D_MODEL, D_IN), D_MODEL ** -0.5),
        'rnn_conv_w': nrm(ks[11], (DEPTH, RNN_CONV, D_RNN), RNN_CONV ** -0.5),
        'rnn_conv_b': nrm(ks[12], (DEPTH, D_RNN), 0.01),
        'lru_wa': nrm(ks[13], (DEPTH, N_RNN_BLOCKS, RNN_BLOCK, RNN_BLOCK), RNN_BLOCK ** -0.5),
        'lru_ba': nrm(ks[14], (DEPTH, D_RNN), 0.01),
        'lru_wx': nrm(ks[15], (DEPTH, N_RNN_BLOCKS, RNN_BLOCK, RNN_BLOCK), RNN_BLOCK ** -0.5),
        'lru_bx': nrm(ks[16], (DEPTH, D_RNN), 0.01),
        'lru_lambda': jnp.log(a0) - jnp.log1p(-a0),
        'w_branch_a': nrm(ks[18], (DEPTH, D_RNN, D_MODEL), BETA * D_RNN ** -0.5),
        'w_branch_b': nrm(ks[19], (DEPTH, N_HEADS * HEAD_DIM, D_MODEL), BETA * (N_HEADS * HEAD_DIM) ** -0.5),
        'w_out': nrm(ks[20], (DEPTH, D_MODEL, D_MODEL), BETA * D_MODEL ** -0.5),
        'ln1_g': 1.0 + nrm(ks[21], (DEPTH, D_MODEL), 0.02),
        'ln1_b': nrm(ks[22], (DEPTH, D_MODEL), 0.02),
        'ffn_w_up': nrm(ks[23], (DEPTH, D_MODEL, 2 * D_FF), D_MODEL ** -0.5),
        'ffn_conv_w': nrm(ks[24], (DEPTH, FFN_CONV, 2 * D_FF), FFN_CONV ** -0.5),
        'ffn_conv_b': nrm(ks[25], (DEPTH, 2 * D_FF), 0.01),
        'ffn_w_down': nrm(ks[26], (DEPTH, D_FF, D_MODEL), BETA * D_FF ** -0.5),
        'ln2_g': 1.0 + nrm(ks[27], (DEPTH, D_MODEL), 0.02),
        'ln2_b': nrm(ks[28], (DEPTH, D_MODEL), 0.02),
    }


def reference(x_prompt, x_sample, cache_k, cache_v, cache_idx_k, state_lru_h, state_lru_conv,
              state_ffn_conv, page_table, meta_tokens, w_in, rnn_conv_w, rnn_conv_b, lru_wa, lru_ba,
              lru_wx, lru_bx, lru_lambda, w_branch_a, w_branch_b, w_out, ln1_g, ln1_b,
              ffn_w_up, ffn_conv_w, ffn_conv_b, ffn_w_down, ln2_g, ln2_b):
    B = x_prompt.shape[0]
    meta = jnp.broadcast_to(meta_tokens.astype(x_prompt.dtype), (B, N_META, D_MODEL))
    xp = jnp.concatenate([meta, x_prompt], axis=1)
    xs = x_sample
    kp, vp, ikp, hp, cp, fp = [], [], [], [], [], []
    ks_, vs_, iks, hs, cs, fs = [], [], [], [], [], []
    for l in range(DEPTH):
        p = (w_in[l], rnn_conv_w[l], rnn_conv_b[l], lru_wa[l], lru_ba[l], lru_wx[l], lru_bx[l],
             lru_lambda[l], w_branch_a[l], w_branch_b[l], w_out[l], ln1_g[l], ln1_b[l],
             ffn_w_up[l], ffn_conv_w[l], ffn_conv_b[l], ffn_w_down[l], ln2_g[l], ln2_b[l])
        xp, st = trunk_layer(xp, p, prompt_attention,
                             jnp.zeros((B, D_RNN), xp.dtype),
                             jnp.zeros((B, RNN_CONV - 1, D_RNN), xp.dtype),
                             jnp.zeros((B, FFN_CONV - 1, 2 * D_FF), xp.dtype))
        kp.append(st[0]); vp.append(st[1]); ikp.append(st[2]); hp.append(st[3]); cp.append(st[4]); fp.append(st[5])
        attend = make_sample_attention(cache_k[l], cache_v[l], cache_idx_k[l], page_table)
        xs, st = trunk_layer(xs, p, attend, state_lru_h[l], state_lru_conv[l], state_ffn_conv[l])
        ks_.append(st[0]); vs_.append(st[1]); iks.append(st[2]); hs.append(st[3]); cs.append(st[4]); fs.append(st[5])
    y_prompt = xp[:, N_META:]
    y_sample = xs
    return (y_prompt, y_sample,
            jnp.stack(kp), jnp.stack(vp), jnp.stack(ikp), jnp.stack(hp), jnp.stack(cp), jnp.stack(fp),
            jnp.stack(ks_), jnp.stack(vs_), jnp.stack(iks), jnp.stack(hs), jnp.stack(cs), jnp.stack(fs))
```

```python
import functools

import jax
import jax.numpy as jnp
from jax import lax
from jax.experimental import pallas as pl
from jax.experimental.pallas import tpu as pltpu

f32 = jnp.float32

N_META = 16
N_RNN_BLOCKS = 8
LRU_C = 8.0
N_HEADS = 8
N_KV_HEADS = 4
KV_GROUP = N_HEADS // N_KV_HEADS
N_IDX_HEADS = 8
TOPK_MAX = 256
PAGE_SIZE = 128
LN_EPS = 1e-5

NEG = -1e30
BIG = 1e30
NO_TIE_LIMIT = 1e9

MXU_DTYPE = jnp.bfloat16
ACT_DTYPE = jnp.bfloat16

V7X_VMEM_BYTES = 64 * 1024 * 1024
VMEM_LIMIT = V7X_VMEM_BYTES * 7 // 8
LANES = 128
SUBLANES = 8

ATTN_QB = 128
ATTN_KC = 256
ATTN_KC2 = 1024
MAX_BISECT = 4096
MAX_TIE_BISECT = 64
SAMPLE_PAGES_PER_STEP = 16


def _cparams(n_grid):
    return pltpu.CompilerParams(
        dimension_semantics=("arbitrary",) * n_grid, vmem_limit_bytes=VMEM_LIMIT)


def _resident(shape):
    nd = len(shape)
    return pl.BlockSpec(tuple(shape), lambda *_: (0,) * nd, pipeline_mode=pl.Buffered(1))


def _largest_tile(n, limit, multiple):
    best = None
    for d in range(multiple, min(n, limit) + 1, multiple):
        if n % d == 0:
            best = d
    assert best is not None, (n, limit, multiple)
    return best


def _gelu(x):
    return jax.nn.gelu(x, approximate=True)


def _softplus(x):
    return jnp.maximum(x, 0.0) + jnp.log1p(jnp.exp(-jnp.abs(x)))


def _layer_norm(y, g, b):
    mu = jnp.mean(y, axis=-1, keepdims=True)
    d = y - mu
    var = jnp.mean(d * d, axis=-1, keepdims=True)
    return d * lax.rsqrt(var + LN_EPS) * g + b


def _causal_conv(u, carry, w, b, tstride):
    width = w.shape[0]
    rows = u.shape[0]
    y = b + w[width - 1:width, :] * u
    if tstride % SUBLANES == 0:
        ext = jnp.concatenate([carry, u], axis=0)
        for j in range(width - 1):
            y = y + w[j:j + 1, :] * ext[j * tstride:j * tstride + rows, :]
        return y, ext[rows:, :]
    assert tstride == 1 and carry.shape[0] == SUBLANES
    row = lax.broadcasted_iota(jnp.int32, (SUBLANES, u.shape[1]), 0)
    for s in range(1, width):
        rolled = pltpu.roll(u, s, axis=0)
        head = jnp.where(row < s, pltpu.roll(carry, s, axis=0), rolled[0:SUBLANES, :])
        shifted = jnp.concatenate([head, rolled[SUBLANES:, :]], axis=0)
        y = y + w[width - 1 - s:width - s, :] * shifted
    return y, u[rows - SUBLANES:, :]


def _proj_kernel(x_ref, *refs, n_outs, n_tr, scales, t_valid, tile, kc):
    n_w = len(n_outs) + n_tr
    w_refs = refs[:n_w]
    o_refs = list(refs[n_w:])
    t0 = pl.program_id(1) * tile
    row = lax.broadcasted_iota(jnp.int32, (tile, 1), 0) + t0
    xb = jnp.where(row < t_valid, x_ref[0], 0.0).astype(MXU_DTYPE)
    for n, n_out in enumerate(n_outs):
        y = jnp.dot(xb, w_refs[n][...], preferred_element_type=f32)
        if scales[n] != 1.0:
            y = y * scales[n]
        for _ in range(n_out):
            o = o_refs.pop(0)
            o[0] = y.astype(o.dtype)
    for n in range(len(n_outs), n_w):
        y = lax.dot_general(w_refs[n][...], xb, (((1,), (1,)), ((), ())),
                            preferred_element_type=f32)
        if scales[n] != 1.0:
            y = y * scales[n]
        o = o_refs.pop(0)
        if len(o.shape) == 4:
            for cc in range(tile // kc):
                o[0, cc] = y[:, cc * kc:(cc + 1) * kc].astype(o.dtype)
        else:
            o[0] = y.astype(o.dtype)


def _project(x, std, tr, *, tile, t_pad, kc):
    bsz, t_len, d = x.shape
    n_t = t_pad // tile
    in_specs = [pl.BlockSpec((1, tile, d), lambda b, t: (b, t, 0))]
    out_shapes, out_specs, scales = [], [], []
    for w, scale, outs in std:
        in_specs.append(_resident(w.shape))
        n = w.shape[1]
        for dtype, padded in outs:
            out_shapes.append(jax.ShapeDtypeStruct((bsz, t_pad if padded else t_len, n), dtype))
            out_specs.append(pl.BlockSpec((1, tile, n), lambda b, t: (b, t, 0)))
        scales.append(scale)
    for w, scale, dtype, chunked in tr:
        in_specs.append(_resident(w.shape))
        n = w.shape[0]
        if chunked:
            out_shapes.append(jax.ShapeDtypeStruct((bsz, t_pad // kc, n, kc), dtype))
            out_specs.append(pl.BlockSpec((1, tile // kc, n, kc), lambda b, t: (b, t, 0, 0)))
        else:
            out_shapes.append(jax.ShapeDtypeStruct((bsz, n, t_pad), dtype))
            out_specs.append(pl.BlockSpec((1, n, tile), lambda b, t: (b, 0, t)))
        scales.append(scale)
    kern = functools.partial(_proj_kernel, n_outs=tuple(len(outs) for _, _, outs in std), n_tr=len(tr),
                             scales=tuple(scales), t_valid=t_len, tile=tile, kc=kc)
    return pl.pallas_call(
        kern, grid=(bsz, n_t), in_specs=in_specs, out_specs=out_specs, out_shape=out_shapes,
        compiler_params=_cparams(2), name="proj",
    )(x, *[w for w, *_ in std], *[w for w, *_ in tr])


def _lru_gates(xc, wax_ref, ba, bx, lam):
    blk = wax_ref.shape[1]
    xcb = xc.astype(MXU_DTYPE)
    rs, gs = [], []
    for n in range(wax_ref.shape[0]):
        y = jnp.dot(xcb[:, n * blk:(n + 1) * blk], wax_ref[n], preferred_element_type=f32)
        rs.append(y[:, :blk])
        gs.append(y[:, blk:])
    r = jax.nn.sigmoid(jnp.concatenate(rs, axis=1) + ba)
    i = jax.nn.sigmoid(jnp.concatenate(gs, axis=1) + bx)
    log_a = (-LRU_C) * r * _softplus(-lam)
    a = jnp.exp(log_a)
    th = jnp.tanh(log_a)
    one_minus_a2 = -2.0 * th / (1.0 - th)
    return a, jnp.sqrt(one_minus_a2) * (i * xc)


def _scan8(a8, b8, h_prev):
    row = lax.broadcasted_iota(jnp.int32, a8.shape, 0)
    a, b = a8, b8
    for s in (1, 2, 4):
        ar = pltpu.roll(a, s, axis=0)
        br = pltpu.roll(b, s, axis=0)
        m = row >= s
        b = jnp.where(m, a * br + b, b)
        a = jnp.where(m, a * ar, a)
    return a * h_prev + b


def _rglru_kernel(xr_ref, gr_ref, h0_ref, buf_ref, cw_ref, cb_ref, wax_ref, ba_ref, bx_ref, lam_ref,
                  g_ref, hl_ref, nbuf_ref, a_s, b_s, hc_s, cc_s, *, tstride):
    @pl.when(pl.program_id(1) == 0)
    def _():
        cc_s[...] = buf_ref[0]
        hc_s[...] = jnp.broadcast_to(h0_ref[0], hc_s.shape)

    xr = xr_ref[0]
    rows = xr.shape[0]
    xc, ncarry = _causal_conv(xr, cc_s[...], cw_ref[...], cb_ref[...], tstride)
    cc_s[...] = ncarry
    nbuf_ref[0] = ncarry
    a, inp = _lru_gates(xc, wax_ref, ba_ref[...], bx_ref[...], lam_ref[...])
    if tstride == 1:
        a_s[...] = a
        b_s[...] = inp

        def body(k, h_prev):
            r0 = pl.multiple_of(k * SUBLANES, SUBLANES)
            h = _scan8(a_s[pl.ds(r0, SUBLANES), :], b_s[pl.ds(r0, SUBLANES), :], h_prev)
            b_s[pl.ds(r0, SUBLANES), :] = h
            return jnp.broadcast_to(h[SUBLANES - 1:SUBLANES, :], h.shape)

        h_last = lax.fori_loop(0, rows // SUBLANES, body, hc_s[...])
        hc_s[...] = h_last
        hl_ref[0] = h_last[0:1, :]
        hs = b_s[...]
    else:
        h = hc_s[...]
        pieces = []
        for s in range(rows // tstride):
            h = a[s * tstride:(s + 1) * tstride, :] * h + inp[s * tstride:(s + 1) * tstride, :]
            pieces.append(h)
        hc_s[...] = h
        hl_ref[0] = h
        hs = jnp.concatenate(pieces, axis=0)
    g_ref[0] = (hs * _gelu(gr_ref[0].astype(f32))).astype(g_ref.dtype)


def _rglru(xr, gr, h0, buf, cw, cb, wax, ba, bx, lam, *, tile, tstride):
    bsz, t_len, c = xr.shape
    hr, cr = h0.shape[1], buf.shape[1]
    hrows = SUBLANES if tstride == 1 else hr
    row_spec = pl.BlockSpec((1, tile, c), lambda b, t: (b, t, 0))
    scratch = [pltpu.VMEM((tile, c), f32), pltpu.VMEM((tile, c), f32),
               pltpu.VMEM((hrows, c), f32), pltpu.VMEM((cr, c), f32)]
    return pl.pallas_call(
        functools.partial(_rglru_kernel, tstride=tstride),
        grid=(bsz, t_len // tile),
        in_specs=[row_spec, row_spec,
                  pl.BlockSpec((1, hr, c), lambda b, t: (b, 0, 0)),
                  pl.BlockSpec((1, cr, c), lambda b, t: (b, 0, 0)),
                  _resident(cw.shape), _resident(cb.shape), _resident(wax.shape),
                  _resident(ba.shape), _resident(bx.shape), _resident(lam.shape)],
        out_specs=[row_spec,
                   pl.BlockSpec((1, hr, c), lambda b, t: (b, 0, 0)),
                   pl.BlockSpec((1, cr, c), lambda b, t: (b, 0, 0))],
        out_shape=[jax.ShapeDtypeStruct((bsz, t_len, c), ACT_DTYPE),
                   jax.ShapeDtypeStruct((bsz, hr, c), f32),
                   jax.ShapeDtypeStruct((bsz, cr, c), f32)],
        scratch_shapes=scratch, compiler_params=_cparams(2), name="rglru",
    )(xr, gr, h0, buf, cw, cb, wax, ba, bx, lam)


def _select_threshold(count_ge, count_tie, lo0, hi0, n_allowed, n_keys, topk):
    kf = float(topk)

    def n_active(done):
        return jnp.sum(jnp.where(done, 0.0, 1.0))

    c0 = count_ge(hi0)
    take = jnp.logical_and(n_allowed > kf, c0 >= kf)
    lo = jnp.where(take, hi0, lo0)
    cnt_lo = jnp.where(take, c0, n_allowed)
    cnt_hi = jnp.where(take, 0.0, c0)
    stalled = jnp.zeros_like(lo0)

    def bis_cond(st):
        return jnp.logical_and(st[-1] > 0.0, st[-2] < MAX_BISECT)

    def bis_body(st):
        lo, hi, cnt_lo, cnt_hi, stalled, it, _ = st
        live = jnp.logical_and(cnt_lo > kf, stalled <= 0.0)
        mid = lo + (hi - lo) * 0.5
        c = count_ge(mid)
        up = jnp.logical_and(live, c >= kf)
        dn = jnp.logical_and(live, c < kf)
        stall_now = jnp.logical_and(live, jnp.logical_or(mid <= lo, mid >= hi))
        lo = jnp.where(up, mid, lo)
        cnt_lo = jnp.where(up, c, cnt_lo)
        hi = jnp.where(dn, mid, hi)
        cnt_hi = jnp.where(dn, c, cnt_hi)
        stalled = jnp.where(stall_now, 1.0, stalled)
        done = jnp.logical_or(cnt_lo <= kf, stalled > 0.0)
        return lo, hi, cnt_lo, cnt_hi, stalled, it + 1, n_active(done)

    st = (lo, hi0, cnt_lo, cnt_hi, stalled, jnp.int32(0), n_active(cnt_lo <= kf))
    lo, _, cnt_lo, cnt_hi, _, _, _ = lax.while_loop(bis_cond, bis_body, st)
    t = lo

    need = cnt_lo > kf
    want = kf - cnt_hi

    def tie_cond(st):
        return jnp.logical_and(st[-1] > 0.0, st[-2] < MAX_TIE_BISECT)

    def tie_body(st):
        jl, jh, jf, found, it, _ = st
        live = found <= 0.0
        jm = jnp.floor((jl + jh) * 0.5)
        c = count_tie(t, jm)
        hit = jnp.logical_and(live, c == want)
        jf = jnp.where(hit, jm, jf)
        found = jnp.where(hit, 1.0, found)
        jl = jnp.where(jnp.logical_and(live, c < want), jm, jl)
        jh = jnp.where(jnp.logical_and(live, c > want), jm, jh)
        return jl, jh, jf, found, it + 1, n_active(found > 0.0)

    found0 = jnp.where(need, 0.0, 1.0)
    st = (jnp.zeros_like(t), jnp.zeros_like(t) + n_keys, jnp.full_like(t, NO_TIE_LIMIT), found0,
          jnp.int32(0), n_active(found0 > 0.0))
    _, _, jf, _, _, _ = lax.while_loop(tie_cond, tie_body, st)
    return t, jnp.where(need, jf, NO_TIE_LIMIT)


def _attn_prompt_kernel(qT_ref, qiT_ref, wiT_ref, kb_ref, vT_ref, kib_ref, o_ref,
                        s_ref, q2_ref, acc_ref, m_ref, l_ref, *, qb, kc, kc2, topk):
    i = pl.program_id(1)
    n_ih = wiT_ref.shape[1]
    di = qiT_ref.shape[1] // n_ih
    n_grp, dh = q2_ref.shape[0], q2_ref.shape[1]
    grp = q2_ref.shape[2] // qb
    c_last = (i * qb) // kc
    n_ch = c_last + 1
    n_ch2 = (n_ch * kc + kc2 - 1) // kc2

    qi_all = jnp.concatenate([qiT_ref[0, h * di:(h + 1) * di, :] for h in range(n_ih)], axis=1)
    wi = wiT_ref[0]
    kio = lax.broadcasted_iota(jnp.int32, (kc, qb), 0)
    qio = lax.broadcasted_iota(jnp.int32, (kc, qb), 1)

    def chunk_scores(c):
        k0 = pl.multiple_of(c * kc, kc)
        rel = jnp.dot(kib_ref[0, pl.ds(k0, kc), :], qi_all, preferred_element_type=f32)
        sc = wi[0:1, :] * jnp.maximum(rel[:, 0:qb], 0.0)
        for h in range(1, n_ih):
            sc = sc + wi[h:h + 1, :] * jnp.maximum(rel[:, h * qb:(h + 1) * qb], 0.0)
        return k0, sc

    def fold(x, op):
        return op(x.reshape(kc // SUBLANES, SUBLANES, qb), axis=0)

    def p1_body(c, carry):
        vmax, vmin = carry
        k0, sc = chunk_scores(c)
        s_ref[pl.ds(k0, kc), :] = sc
        return jnp.maximum(vmax, fold(sc, jnp.max)), jnp.minimum(vmin, fold(sc, jnp.min))

    vmax, vmin = lax.fori_loop(
        0, c_last, p1_body,
        (jnp.full((SUBLANES, qb), NEG, f32), jnp.full((SUBLANES, qb), BIG, f32)))
    k0, sc = chunk_scores(c_last)
    allowed = (kio + k0) <= (qio + i * qb)
    s_ref[pl.ds(k0, kc), :] = jnp.where(allowed, sc, NEG)
    vmax = jnp.maximum(vmax, fold(jnp.where(allowed, sc, NEG), jnp.max))
    vmin = jnp.minimum(vmin, fold(jnp.where(allowed, sc, BIG), jnp.min))

    def fill_body(c, carry):
        s_ref[pl.ds(pl.multiple_of(c * kc, kc), kc), :] = jnp.full((kc, qb), NEG, f32)
        return carry

    lax.fori_loop(n_ch, n_ch2 * (kc2 // kc), fill_body, 0)

    def csum(x):
        return jnp.sum(x.reshape(kc2 // SUBLANES, SUBLANES, qb), axis=0)

    def count_ge(x):
        def body(c, acc):
            blk = s_ref[pl.ds(pl.multiple_of(c * kc2, kc2), kc2), :]
            return acc + csum(jnp.where(blk >= x, 1.0, 0.0))
        acc = lax.fori_loop(0, n_ch2, body, jnp.zeros((SUBLANES, qb), f32))
        return jnp.sum(acc, axis=0, keepdims=True)

    def count_tie(t, j):
        def body(c, acc):
            r0 = pl.multiple_of(c * kc2, kc2)
            blk = s_ref[pl.ds(r0, kc2), :]
            kidx = (lax.broadcasted_iota(jnp.int32, (kc2, qb), 0) + r0).astype(f32)
            hit = jnp.logical_and(blk == t, kidx < j)
            return acc + csum(jnp.where(hit, 1.0, 0.0))
        acc = lax.fori_loop(0, n_ch2, body, jnp.zeros((SUBLANES, qb), f32))
        return jnp.sum(acc, axis=0, keepdims=True)

    n_allowed = (lax.broadcasted_iota(jnp.int32, (1, qb), 1) + (i * qb + 1)).astype(f32)
    t, jsel = _select_threshold(
        count_ge, count_tie,
        jnp.min(vmin, axis=0, keepdims=True), jnp.max(vmax, axis=0, keepdims=True),
        n_allowed, (n_ch * kc).astype(f32), topk)

    for g in range(n_grp):
        q2_ref[g] = jnp.concatenate(
            [qT_ref[0, (g * grp + j) * dh:(g * grp + j + 1) * dh, :] for j in range(grp)], axis=1)
    m_ref[...] = jnp.full(m_ref.shape, NEG, f32)
    l_ref[...] = jnp.zeros(l_ref.shape, f32)
    acc_ref[...] = jnp.zeros(acc_ref.shape, f32)

    def p3_body(c, carry):
        k0 = pl.multiple_of(c * kc, kc)
        blk = s_ref[pl.ds(k0, kc), :]
        kidx = (kio + k0).astype(f32)
        sel = jnp.logical_or(blk > t, jnp.logical_and(blk == t, kidx < jsel))
        bias = jnp.where(sel, 0.0, NEG)
        bias = jnp.concatenate([bias] * grp, axis=1)
        for g in range(n_grp):
            st = jnp.dot(kb_ref[0, pl.ds(k0, kc), g * dh:(g + 1) * dh], q2_ref[g],
                         preferred_element_type=f32) + bias
            m_old = m_ref[g]
            m_new = jnp.maximum(m_old, jnp.max(st, axis=0, keepdims=True))
            alpha = jnp.exp(m_old - m_new)
            p = jnp.exp(st - m_new)
            l_ref[g] = alpha * l_ref[g] + jnp.sum(p, axis=0, keepdims=True)
            m_ref[g] = m_new
            pv = jnp.dot(vT_ref[0, c, g * dh:(g + 1) * dh, :], p.astype(MXU_DTYPE),
                         preferred_element_type=f32)
            acc_ref[g] = alpha * acc_ref[g] + pv
        return carry

    lax.fori_loop(0, n_ch, p3_body, 0)

    for g in range(n_grp):
        o = acc_ref[g] / l_ref[g]
        for j in range(grp):
            h = g * grp + j
            o_ref[0, :, h * dh:(h + 1) * dh] = o[:, j * qb:(j + 1) * qb].T.astype(o_ref.dtype)


def _attn_prompt(qT, qiT, wiT, kb, vT4, kib, *, t_len, topk):
    bsz, dq, t_pad = qT.shape
    qb, kc, kc2 = ATTN_QB, ATTN_KC, ATTN_KC2
    dkv = kb.shape[2]
    dh = dq // N_HEADS
    s_rows = -(-t_pad // kc2) * kc2
    kern = functools.partial(_attn_prompt_kernel, qb=qb, kc=kc, kc2=kc2, topk=topk)
    return pl.pallas_call(
        kern, grid=(bsz, -(-t_len // qb)),
        in_specs=[pl.BlockSpec((1, dq, qb), lambda b, i: (b, 0, i)),
                  pl.BlockSpec((1, qiT.shape[1], qb), lambda b, i: (b, 0, i)),
                  pl.BlockSpec((1, wiT.shape[1], qb), lambda b, i: (b, 0, i)),
                  pl.BlockSpec((1, t_pad, dkv), lambda b, i: (b, 0, 0), pipeline_mode=pl.Buffered(1)),
                  pl.BlockSpec((1,) + vT4.shape[1:], lambda b, i: (b, 0, 0, 0),
                               pipeline_mode=pl.Buffered(1)),
                  pl.BlockSpec((1, t_pad, kib.shape[2]), lambda b, i: (b, 0, 0),
                               pipeline_mode=pl.Buffered(1))],
        out_specs=pl.BlockSpec((1, qb, dq), lambda b, i: (b, i, 0)),
        out_shape=jax.ShapeDtypeStruct((bsz, t_pad, dq), ACT_DTYPE),
        scratch_shapes=[pltpu.VMEM((s_rows, qb), f32),
                        pltpu.VMEM((N_KV_HEADS, dh, KV_GROUP * qb), MXU_DTYPE),
                        pltpu.VMEM((N_KV_HEADS, dh, KV_GROUP * qb), f32),
                        pltpu.VMEM((N_KV_HEADS, 1, KV_GROUP * qb), f32),
                        pltpu.VMEM((N_KV_HEADS, 1, KV_GROUP * qb), f32)],
        compiler_params=_cparams(2), name="attn_prompt",
    )(qT, qiT, wiT, kb, vT4, kib)


def _sattn_select_kernel(pt_ref, qi_ref, wi_ref, kin_ref, *rest, pg, ns, topk, n_new):
    del pt_ref
    pages = rest[:pg]
    s_ref, t_ref, j_ref = rest[pg:]
    st = pl.program_id(1)
    pgk = pg * PAGE_SIZE
    qi = qi_ref[0]
    wi = wi_ref[0]
    n_ih = qi.shape[0] // n_new

    def scores(keys):
        rel = lax.dot_general(qi, keys, (((1,), (1,)), ((), ())), preferred_element_type=f32)
        rel = jnp.maximum(rel, 0.0) * wi
        return jnp.sum(rel.reshape(n_ih, n_new, keys.shape[0]), axis=0)

    kp = jnp.concatenate([p[...] for p in pages], axis=0).astype(MXU_DTYPE)
    s_ref[0, st] = scores(kp)

    @pl.when(st == ns - 1)
    def _():
        n_pad = kin_ref.shape[1]
        sn = scores(kin_ref[0])
        lane = lax.broadcasted_iota(jnp.int32, (n_new, n_pad), 1)
        qrow = lax.broadcasted_iota(jnp.int32, (n_new, n_pad), 0)
        ok_new = lane <= qrow
        s_ref[0, ns] = jnp.concatenate(
            [jnp.where(ok_new, sn, NEG), jnp.full((n_new, pgk - n_pad), NEG, f32)], axis=1)

        s_all = s_ref[0]
        past = s_all[:-1]
        hi0 = jnp.maximum(jnp.max(jnp.max(past, axis=0), axis=1, keepdims=True),
                          jnp.max(jnp.where(ok_new, sn, NEG), axis=1, keepdims=True))
        lo0 = jnp.minimum(jnp.min(jnp.min(past, axis=0), axis=1, keepdims=True),
                          jnp.min(jnp.where(ok_new, sn, BIG), axis=1, keepdims=True))
        kidx = (lax.broadcasted_iota(jnp.int32, s_all.shape, 0) * pgk
                + lax.broadcasted_iota(jnp.int32, s_all.shape, 2)).astype(f32)

        def total(x):
            return jnp.sum(jnp.sum(x, axis=0), axis=1, keepdims=True)

        def count_ge(x):
            return total(jnp.where(s_all >= x[None], 1.0, 0.0))

        def count_tie(t, j):
            hit = jnp.logical_and(s_all == t[None], kidx < j[None])
            return total(jnp.where(hit, 1.0, 0.0))

        q1 = lax.broadcasted_iota(jnp.int32, (n_new, 1), 0)
        n_allowed = (q1 + (ns * pgk + 1)).astype(f32)
        t, jsel = _select_threshold(count_ge, count_tie, lo0, hi0, n_allowed, float(ns * pgk + n_pad),
                                    topk)
        t_ref[0] = jnp.broadcast_to(t, t_ref.shape[1:])
        j_ref[0] = jnp.broadcast_to(jsel, j_ref.shape[1:])


def _sattn_attend_kernel(pt_ref, qblk_ref, s_ref, snew_ref, t_ref, j_ref, knew_ref, vnew_ref, *rest,
                         pg, ns, n_new):
    del pt_ref
    kpages, vpages = rest[:pg], rest[pg:2 * pg]
    o_ref, m_s, l_s, acc_s = rest[2 * pg:]
    st = pl.program_id(1)
    pgk = pg * PAGE_SIZE
    qblk = qblk_ref[0]
    n_heads = qblk.shape[0] // n_new
    dh = qblk.shape[1] // N_KV_HEADS
    t = t_ref[0][:, 0:1]
    jsel = j_ref[0][:, 0:1]

    @pl.when(st == 0)
    def _():
        m_s[...] = jnp.full(m_s.shape, NEG, f32)
        l_s[...] = jnp.zeros(l_s.shape, f32)
        acc_s[...] = jnp.zeros(acc_s.shape, f32)

    def update(sc, base, keys, vals):
        n = sc.shape[1]
        kidx = (lax.broadcasted_iota(jnp.int32, (n_new, n), 1) + base).astype(f32)
        sel = jnp.logical_or(sc > t, jnp.logical_and(sc == t, kidx < jsel))
        bias = jnp.where(sel, 0.0, NEG)
        bias = jnp.concatenate([bias] * n_heads, axis=0)
        s = lax.dot_general(qblk, keys, (((1,), (1,)), ((), ())), preferred_element_type=f32) + bias
        m_old = m_s[...]
        m_new = jnp.maximum(m_old, jnp.max(s, axis=1, keepdims=True))
        alpha = jnp.exp(m_old - m_new)
        p = jnp.exp(s - m_new)
        l_s[...] = alpha * l_s[...] + jnp.sum(p, axis=1, keepdims=True)
        m_s[...] = m_new
        acc_s[...] = alpha * acc_s[...] + jnp.dot(p.astype(MXU_DTYPE), vals, preferred_element_type=f32)

    kp = jnp.concatenate([p[...] for p in kpages], axis=0).astype(MXU_DTYPE)
    vp = jnp.concatenate([p[...] for p in vpages], axis=0).astype(MXU_DTYPE)
    update(s_ref[0, 0], st * pgk, kp, vp)

    @pl.when(st == ns - 1)
    def _():
        n_pad = knew_ref.shape[1]
        update(snew_ref[0, 0][:, :n_pad], ns * pgk, knew_ref[0], vnew_ref[0])
        o = acc_s[...] / l_s[...]
        for h in range(n_heads):
            g = h // KV_GROUP
            o_ref[0, h * n_new:(h + 1) * n_new, :] = (
                o[h * n_new:(h + 1) * n_new, g * dh:(g + 1) * dh].astype(o_ref.dtype))


def _attn_sample(layer, page_table, cache_ik, cache_k2, cache_v2, qi_hq, wi_hq, kin, qblk, knew, vnew,
                 *, topk):
    bsz, n_pages = page_table.shape
    n_new = qi_hq.shape[1] // N_IDX_HEADS
    pg = _largest_tile(n_pages, SAMPLE_PAGES_PER_STEP, 1)
    ns = n_pages // pg
    pgk = pg * PAGE_SIZE
    di = cache_ik.shape[3]
    dkv = cache_k2.shape[3]
    n_pad = kin.shape[1]
    rows = qblk.shape[1]

    def page_spec(width, j):
        return pl.BlockSpec((None, None, PAGE_SIZE, width),
                            lambda b, s, pt: (layer, pt[b, s * pg + j], 0, 0))

    s_all, thr, jsel = pl.pallas_call(
        functools.partial(_sattn_select_kernel, pg=pg, ns=ns, topk=topk, n_new=n_new),
        grid_spec=pltpu.PrefetchScalarGridSpec(
            num_scalar_prefetch=1, grid=(bsz, ns),
            in_specs=[pl.BlockSpec((1,) + qi_hq.shape[1:], lambda b, s, pt: (b, 0, 0)),
                      pl.BlockSpec((1,) + wi_hq.shape[1:], lambda b, s, pt: (b, 0, 0)),
                      pl.BlockSpec((1, n_pad, di), lambda b, s, pt: (b, 0, 0))]
            + [page_spec(di, j) for j in range(pg)],
            out_specs=[pl.BlockSpec((1, ns + 1, n_new, pgk), lambda b, s, pt: (b, 0, 0, 0)),
                       pl.BlockSpec((1, n_new, LANES), lambda b, s, pt: (b, 0, 0)),
                       pl.BlockSpec((1, n_new, LANES), lambda b, s, pt: (b, 0, 0))]),
        out_shape=[jax.ShapeDtypeStruct((bsz, ns + 1, n_new, pgk), f32),
                   jax.ShapeDtypeStruct((bsz, n_new, LANES), f32),
                   jax.ShapeDtypeStruct((bsz, n_new, LANES), f32)],
        compiler_params=_cparams(2), name="sattn_select",
    )(page_table, qi_hq, wi_hq, kin, *([cache_ik] * pg))

    return pl.pallas_call(
        functools.partial(_sattn_attend_kernel, pg=pg, ns=ns, n_new=n_new),
        grid_spec=pltpu.PrefetchScalarGridSpec(
            num_scalar_prefetch=1, grid=(bsz, ns),
            in_specs=[pl.BlockSpec((1, rows, dkv), lambda b, s, pt: (b, 0, 0)),
                      pl.BlockSpec((1, 1, n_new, pgk), lambda b, s, pt: (b, s, 0, 0)),
                      pl.BlockSpec((1, 1, n_new, pgk), lambda b, s, pt: (b, ns, 0, 0)),
                      pl.BlockSpec((1, n_new, LANES), lambda b, s, pt: (b, 0, 0)),
                      pl.BlockSpec((1, n_new, LANES), lambda b, s, pt: (b, 0, 0)),
                      pl.BlockSpec((1, n_pad, dkv), lambda b, s, pt: (b, 0, 0)),
                      pl.BlockSpec((1, n_pad, dkv), lambda b, s, pt: (b, 0, 0))]
            + [page_spec(dkv, j) for j in range(pg)] * 2,
            out_specs=pl.BlockSpec((1, rows, dkv // N_KV_HEADS), lambda b, s, pt: (b, 0, 0)),
            scratch_shapes=[pltpu.VMEM((rows, 1), f32), pltpu.VMEM((rows, 1), f32),
                            pltpu.VMEM((rows, dkv), f32)]),
        out_shape=jax.ShapeDtypeStruct((bsz, rows, dkv // N_KV_HEADS), ACT_DTYPE),
        compiler_params=_cparams(2), name="sattn_attend",
    )(page_table, qblk, s_all, s_all, thr, jsel, knew, vnew, *([cache_k2] * pg), *([cache_v2] * pg))


def _merge_kernel(g_ref, o_ref, ga_ref, gb_ref, x_ref, wa_ref, wb_ref, wo_ref, lg_ref, lb_ref, out_ref,
                  *, alpha):
    ya = jnp.dot(g_ref[0].astype(MXU_DTYPE), wa_ref[...], preferred_element_type=f32)
    yb = jnp.dot(o_ref[0].astype(MXU_DTYPE), wb_ref[...], preferred_element_type=f32)
    mixed = (jax.nn.sigmoid(ga_ref[0].astype(f32)) * ya + jax.nn.sigmoid(gb_ref[0].astype(f32)) * yb)
    mix = jnp.dot(mixed.astype(MXU_DTYPE), wo_ref[...], preferred_element_type=f32)
    out_ref[0] = _layer_norm(alpha * x_ref[0] + mix, lg_ref[...], lb_ref[...])


def _merge(g, o, ga, gb, x, wa, wb, wo, lg, lb, *, tile, alpha):
    bsz, t_len, d = x.shape
    row_spec = pl.BlockSpec((1, tile, d), lambda b, t: (b, t, 0))
    return pl.pallas_call(
        functools.partial(_merge_kernel, alpha=alpha), grid=(bsz, t_len // tile),
        in_specs=[row_spec] * 5 + [_resident(a.shape) for a in (wa, wb, wo, lg, lb)],
        out_specs=row_spec, out_shape=jax.ShapeDtypeStruct((bsz, t_len, d), f32),
        compiler_params=_cparams(2), name="merge",
    )(g, o, ga, gb, x, wa, wb, wo, lg, lb)


def _ffn_kernel(x_ref, buf_ref, wup_ref, cw_ref, cb_ref, wdn_ref, lg_ref, lb_ref, out_ref, nbuf_ref,
                carry_ref, *, alpha, tstride, ck):
    @pl.when(pl.program_id(1) == 0)
    def _():
        carry_ref[...] = buf_ref[0]

    x = x_ref[0]
    xb = x.astype(MXU_DTYPE)
    d_ff = wdn_ref.shape[0]
    acc = jnp.zeros(x.shape, f32)
    for c in range(d_ff // ck):
        halves = []
        for off in (c * ck, d_ff + c * ck):
            u = jnp.dot(xb, wup_ref[:, off:off + ck], preferred_element_type=f32)
            uc, ncarry = _causal_conv(u, carry_ref[:, off:off + ck], cw_ref[:, off:off + ck],
                                      cb_ref[:, off:off + ck], tstride)
            carry_ref[:, off:off + ck] = ncarry
            halves.append(uc)
        act = (_gelu(halves[0]) * halves[1]).astype(MXU_DTYPE)
        acc = acc + jnp.dot(act, wdn_ref[c * ck:(c + 1) * ck, :], preferred_element_type=f32)
    out_ref[0] = _layer_norm(alpha * x + acc, lg_ref[...], lb_ref[...])
    nbuf_ref[0] = carry_ref[...]


def _ffn(x, buf, wup, cw, cb, wdn, lg, lb, *, tile, alpha, tstride):
    bsz, t_len, d = x.shape
    cr, f2 = buf.shape[1], buf.shape[2]
    row_spec = pl.BlockSpec((1, tile, d), lambda b, t: (b, t, 0))
    buf_spec = pl.BlockSpec((1, cr, f2), lambda b, t: (b, 0, 0))
    return pl.pallas_call(
        functools.partial(_ffn_kernel, alpha=alpha, tstride=tstride, ck=512),
        grid=(bsz, t_len // tile),
        in_specs=[row_spec, buf_spec] + [_resident(a.shape) for a in (wup, cw, cb, wdn, lg, lb)],
        out_specs=[row_spec, buf_spec],
        out_shape=[jax.ShapeDtypeStruct((bsz, t_len, d), f32),
                   jax.ShapeDtypeStruct((bsz, cr, f2), f32)],
        scratch_shapes=[pltpu.VMEM((cr, f2), f32)],
        compiler_params=_cparams(2), name="ffn",
    )(x, buf, wup, cw, cb, wdn, lg, lb)


def _layer_weights(l, w_in, rnn_conv_w, rnn_conv_b, lru_wa, lru_ba, lru_wx, lru_bx, lru_lambda,
                   w_branch_a, w_branch_b, w_out, ln1_g, ln1_b, ffn_w_up, ffn_conv_w, ffn_conv_b,
                   ffn_w_down, ln2_g, ln2_b):
    d = w_in.shape[1]
    dh = d // N_HEADS
    di = (w_in.shape[2] - 4 * d - (N_HEADS + 2 * N_KV_HEADS) * dh - N_IDX_HEADS) // (N_IDX_HEADS + 1)
    sizes = (d, d, N_HEADS * dh, N_KV_HEADS * dh, N_KV_HEADS * dh, N_IDX_HEADS * di, di, N_IDX_HEADS, d, d)
    names = ("xr", "gr", "q", "k", "v", "qi", "ki", "wi", "ga", "gb")
    cols, off = {}, 0
    for name, size in zip(names, sizes):
        cols[name] = w_in[l][:, off:off + size].astype(MXU_DTYPE)
        off += size
    row = lambda v: v[l][None, :]
    return dict(
        cols=cols, dh=dh, di=di,
        cw=rnn_conv_w[l], cb=row(rnn_conv_b),
        wax=jnp.concatenate([lru_wa[l], lru_wx[l]], axis=2).astype(MXU_DTYPE),
        ba=row(lru_ba), bx=row(lru_bx), lam=row(lru_lambda),
        wa=w_branch_a[l].astype(MXU_DTYPE), wb=w_branch_b[l].astype(MXU_DTYPE),
        wo=w_out[l].astype(MXU_DTYPE), lg1=row(ln1_g), lb1=row(ln1_b),
        wup=ffn_w_up[l].astype(MXU_DTYPE), fcw=ffn_conv_w[l], fcb=row(ffn_conv_b),
        wdn=ffn_w_down[l].astype(MXU_DTYPE), lg2=row(ln2_g), lb2=row(ln2_b))


def _prompt_layer(x, w, alpha):
    bsz, t_len, d = x.shape
    dh, di, cols = w["dh"], w["di"], w["cols"]
    kc = ATTN_KC
    t_pad = -(-t_len // kc) * kc
    ptile = kc * _largest_tile(t_pad // kc, 3, 1)
    rtile = _largest_tile(t_len, 1024, 16)
    topk = min(TOPK_MAX, t_len // 4)
    one = lambda dtype: [(dtype, False)]
    std = [(cols["xr"], 1.0, one(f32)), (cols["gr"], 1.0, one(ACT_DTYPE)),
           (cols["k"], 1.0, [(f32, False), (MXU_DTYPE, True)]), (cols["v"], 1.0, one(f32)),
           (cols["ki"], 1.0, [(f32, False), (MXU_DTYPE, True)]),
           (cols["ga"], 1.0, one(ACT_DTYPE)), (cols["gb"], 1.0, one(ACT_DTYPE))]
    tr = [(cols["q"].T, dh ** -0.5, MXU_DTYPE, False), (cols["qi"].T, di ** -0.5, MXU_DTYPE, False),
          (cols["wi"].T, N_IDX_HEADS ** -0.5, f32, False), (cols["v"].T, 1.0, MXU_DTYPE, True)]
    xr, gr, k, kb, v, ki, kib, ga, gb, qT, qiT, wiT, vT4 = _project(
        x, std, tr, tile=ptile, t_pad=t_pad, kc=kc)

    c_rnn, w_rnn = xr.shape[2], w["cw"].shape[0]
    g, h_last, nbuf = _rglru(
        xr, gr, jnp.zeros((bsz, 1, c_rnn), f32), jnp.zeros((bsz, SUBLANES, c_rnn), f32),
        w["cw"], w["cb"], w["wax"], w["ba"], w["bx"], w["lam"], tile=rtile, tstride=1)
    o = _attn_prompt(qT, qiT, wiT, kb, vT4, kib, t_len=t_len, topk=topk)
    x1 = _merge(g, o, ga, gb, x, w["wa"], w["wb"], w["wo"], w["lg1"], w["lb1"], tile=rtile, alpha=alpha)
    f2, w_ffn = w["fcw"].shape[1], w["fcw"].shape[0]
    x2, fbuf = _ffn(x1, jnp.zeros((bsz, SUBLANES, f2), f32), w["wup"], w["fcw"], w["fcb"], w["wdn"],
                    w["lg2"], w["lb2"], tile=_largest_tile(t_len, 512, 16), alpha=alpha, tstride=1)
    state = (k.reshape(bsz, t_len, N_KV_HEADS, dh), v.reshape(bsz, t_len, N_KV_HEADS, dh), ki,
             h_last[:, 0, :], nbuf[:, SUBLANES - (w_rnn - 1):, :], fbuf[:, SUBLANES - (w_ffn - 1):, :])
    return x2, state


def _to_time_major(a):
    a = jnp.swapaxes(a, 0, 1)
    return a.reshape((1, a.shape[0] * a.shape[1]) + a.shape[2:])


def _from_time_major(a, bsz):
    a = a.reshape((a.shape[1] // bsz, bsz) + a.shape[2:])
    return jnp.swapaxes(a, 0, 1)


def _sample_layer(x, w, alpha, layer, page_table, cache_ik, cache_k2, cache_v2, h0, lru_buf, ffn_buf, bsz):
    rows, d = x.shape[1], x.shape[2]
    n_new = rows // bsz
    dh, di, cols = w["dh"], w["di"], w["cols"]
    past = page_table.shape[1] * PAGE_SIZE
    topk = min(TOPK_MAX, (past + n_new) // 4)
    one = lambda dtype: [(dtype, False)]
    std = [(cols["xr"], 1.0, one(f32)), (cols["gr"], 1.0, one(ACT_DTYPE)),
           (cols["q"], dh ** -0.5, one(MXU_DTYPE)), (cols["k"], 1.0, one(f32)),
           (cols["v"], 1.0, one(f32)), (cols["qi"], di ** -0.5, one(MXU_DTYPE)),
           (cols["ki"], 1.0, one(f32)), (cols["wi"], N_IDX_HEADS ** -0.5, one(f32)),
           (cols["ga"], 1.0, one(ACT_DTYPE)), (cols["gb"], 1.0, one(ACT_DTYPE))]
    xr, gr, q, k, v, qi, ki, wi, ga, gb = _project(x, std, [], tile=rows, t_pad=rows, kc=ATTN_KC)

    g, h_last, nbuf = _rglru(xr, gr, h0[None], _to_time_major(lru_buf), w["cw"], w["cb"], w["wax"],
                             w["ba"], w["bx"], w["lam"], tile=rows, tstride=bsz)

    def heads_major(a, n_h):
        a = _from_time_major(a, bsz).reshape(bsz, n_new, n_h, -1)
        return jnp.swapaxes(a, 1, 2).reshape(bsz, n_h * n_new, -1)

    def pad_new(a):
        a = _from_time_major(a, bsz).astype(MXU_DTYPE)
        return jnp.pad(a, ((0, 0), (0, LANES - n_new), (0, 0)))

    q_hq = heads_major(q, N_HEADS)
    own = (jnp.arange(N_HEADS)[:, None] // KV_GROUP == jnp.arange(N_KV_HEADS)[None, :])
    own = jnp.repeat(own, n_new, axis=0)
    qblk = jnp.where(own[None, :, :, None], q_hq[:, :, None, :], jnp.zeros((), q_hq.dtype))
    qblk = qblk.reshape(bsz, N_HEADS * n_new, N_KV_HEADS * dh)
    o = _attn_sample(layer, page_table, cache_ik, cache_k2, cache_v2,
                     heads_major(qi, N_IDX_HEADS), heads_major(wi, N_IDX_HEADS), pad_new(ki),
                     qblk, pad_new(k), pad_new(v), topk=topk)
    o = jnp.swapaxes(o.reshape(bsz, N_HEADS, n_new, dh), 1, 2).reshape(bsz, n_new, N_HEADS * dh)
    o = _to_time_major(o)

    x1 = _merge(g, o, ga, gb, x, w["wa"], w["wb"], w["wo"], w["lg1"], w["lb1"], tile=rows, alpha=alpha)
    x2, fbuf = _ffn(x1, _to_time_major(ffn_buf), w["wup"], w["fcw"], w["fcb"], w["wdn"], w["lg2"],
                    w["lb2"], tile=rows, alpha=alpha, tstride=bsz)
    state = (_from_time_major(k, bsz).reshape(bsz, n_new, N_KV_HEADS, dh),
             _from_time_major(v, bsz).reshape(bsz, n_new, N_KV_HEADS, dh),
             _from_time_major(ki, bsz), h_last[0], _from_time_major(nbuf, bsz), _from_time_major(fbuf, bsz))
    return x2, state


def kernel(x_prompt, x_sample, cache_k, cache_v, cache_idx_k, state_lru_h, state_lru_conv, state_ffn_conv,
           page_table, meta_tokens, w_in, rnn_conv_w, rnn_conv_b, lru_wa, lru_ba, lru_wx, lru_bx, lru_lambda,
           w_branch_a, w_branch_b, w_out, ln1_g, ln1_b, ffn_w_up, ffn_conv_w, ffn_conv_b, ffn_w_down,
           ln2_g, ln2_b):
    depth = w_in.shape[0]
    alpha = (2.0 * depth) ** 0.25
    bsz, _, d = x_prompt.shape
    dbsz = x_sample.shape[0]
    meta = jnp.broadcast_to(meta_tokens.astype(x_prompt.dtype), (bsz, N_META, d))
    xp = jnp.concatenate([meta, x_prompt], axis=1)
    xs = _to_time_major(x_sample)
    cache_k2 = cache_k.reshape(cache_k.shape[:3] + (-1,))
    cache_v2 = cache_v.reshape(cache_v.shape[:3] + (-1,))
    p_states, s_states = [], []
    for l in range(depth):
        w = _layer_weights(l, w_in, rnn_conv_w, rnn_conv_b, lru_wa, lru_ba, lru_wx, lru_bx, lru_lambda,
                           w_branch_a, w_branch_b, w_out, ln1_g, ln1_b, ffn_w_up, ffn_conv_w, ffn_conv_b,
                           ffn_w_down, ln2_g, ln2_b)
        xp, st = _prompt_layer(xp, w, alpha)
        p_states.append(st)
        xs, st = _sample_layer(xs, w, alpha, l, page_table, cache_idx_k, cache_k2, cache_v2,
                               state_lru_h[l], state_lru_conv[l], state_ffn_conv[l], dbsz)
        s_states.append(st)
    stack = lambda states, n: jnp.stack([st[n] for st in states])
    return ((xp[:, N_META:], _from_time_major(xs, dbsz))
            + tuple(stack(p_states, n) for n in range(6))
            + tuple(stack(s_states, n) for n in range(6)))
```

```python
import functools

import jax
import jax.numpy as jnp
from jax import lax
from jax.experimental import pallas as pl
from jax.experimental.pallas import tpu as pltpu

f32 = jnp.float32

N_META = 16
N_RNN_BLOCKS = 8
LRU_C = 8.0
N_HEADS = 8
N_KV_HEADS = 4
KV_GROUP = N_HEADS // N_KV_HEADS
N_IDX_HEADS = 8
TOPK_MAX = 256
PAGE_SIZE = 128
LN_EPS = 1e-5

NEG = -1e30
BIG = 1e30
NO_TIE_LIMIT = 1e9
LOG2E = 1.4426950408889634

MXU_DTYPE = jnp.bfloat16
ACT_DTYPE = jnp.bfloat16

V7X_VMEM_BYTES = 64 * 1024 * 1024
VMEM_LIMIT = V7X_VMEM_BYTES * 7 // 8
LANES = 128
SUBLANES = 8

ATTN_QB = 128
ATTN_KC = 256
ATTN_KC2 = 1024
COUNT_CHAINS = 8
COUNT_ROWS = COUNT_CHAINS * SUBLANES
MAX_BISECT = 4096
MAX_TIE_BISECT = 64
SAMPLE_PAGES_PER_STEP = 16


def _cparams(n_grid):
    return pltpu.CompilerParams(
        dimension_semantics=("arbitrary",) * n_grid, vmem_limit_bytes=VMEM_LIMIT)


def _resident(shape):
    nd = len(shape)
    return pl.BlockSpec(tuple(shape), lambda *_: (0,) * nd, pipeline_mode=pl.Buffered(1))


def _largest_tile(n, limit, multiple):
    best = None
    for d in range(multiple, min(n, limit) + 1, multiple):
        if n % d == 0:
            best = d
    assert best is not None, (n, limit, multiple)
    return best


def _gelu(x):
    return jax.nn.gelu(x, approximate=True)


def _softplus(x):
    return jnp.maximum(x, 0.0) + jnp.log1p(jnp.exp(-jnp.abs(x)))


def _layer_norm(y, g, b):
    mu = jnp.mean(y, axis=-1, keepdims=True)
    d = y - mu
    var = jnp.mean(d * d, axis=-1, keepdims=True)
    return d * lax.rsqrt(var + LN_EPS) * g + b


def _causal_conv(u, carry, w, b, tstride):
    width = w.shape[0]
    rows = u.shape[0]
    y = b + w[width - 1:width, :] * u
    if tstride % SUBLANES == 0:
        ext = jnp.concatenate([carry, u], axis=0)
        for j in range(width - 1):
            y = y + w[j:j + 1, :] * ext[j * tstride:j * tstride + rows, :]
        return y, ext[rows:, :]
    assert tstride == 1 and carry.shape[0] == SUBLANES
    row = lax.broadcasted_iota(jnp.int32, (SUBLANES, u.shape[1]), 0)
    for s in range(1, width):
        rolled = pltpu.roll(u, s, axis=0)
        head = jnp.where(row < s, pltpu.roll(carry, s, axis=0), rolled[0:SUBLANES, :])
        shifted = jnp.concatenate([head, rolled[SUBLANES:, :]], axis=0)
        y = y + w[width - 1 - s:width - s, :] * shifted
    return y, u[rows - SUBLANES:, :]


def _proj_kernel(x_ref, *refs, n_outs, n_tr, scales, t_valid, tile, kc):
    n_w = len(n_outs) + n_tr
    w_refs = refs[:n_w]
    o_refs = list(refs[n_w:])
    t0 = pl.program_id(1) * tile
    row = lax.broadcasted_iota(jnp.int32, (tile, 1), 0) + t0
    xb = jnp.where(row < t_valid, x_ref[0], 0.0).astype(MXU_DTYPE)
    for n, n_out in enumerate(n_outs):
        y = jnp.dot(xb, w_refs[n][...], preferred_element_type=f32)
        if scales[n] != 1.0:
            y = y * scales[n]
        for _ in range(n_out):
            o = o_refs.pop(0)
            o[0] = y.astype(o.dtype)
    for n in range(len(n_outs), n_w):
        y = lax.dot_general(w_refs[n][...], xb, (((1,), (1,)), ((), ())),
                            preferred_element_type=f32)
        if scales[n] != 1.0:
            y = y * scales[n]
        o = o_refs.pop(0)
        if len(o.shape) == 4:
            for cc in range(tile // kc):
                o[0, cc] = y[:, cc * kc:(cc + 1) * kc].astype(o.dtype)
        else:
            o[0] = y.astype(o.dtype)


def _project(x, std, tr, *, tile, t_pad, kc):
    bsz, t_len, d = x.shape
    n_t = t_pad // tile
    in_specs = [pl.BlockSpec((1, tile, d), lambda b, t: (b, t, 0))]
    out_shapes, out_specs, scales = [], [], []
    for w, scale, outs in std:
        in_specs.append(_resident(w.shape))
        n = w.shape[1]
        for dtype, padded in outs:
            out_shapes.append(jax.ShapeDtypeStruct((bsz, t_pad if padded else t_len, n), dtype))
            out_specs.append(pl.BlockSpec((1, tile, n), lambda b, t: (b, t, 0)))
        scales.append(scale)
    for w, scale, dtype, chunked in tr:
        in_specs.append(_resident(w.shape))
        n = w.shape[0]
        if chunked:
            out_shapes.append(jax.ShapeDtypeStruct((bsz, t_pad // kc, n, kc), dtype))
            out_specs.append(pl.BlockSpec((1, tile // kc, n, kc), lambda b, t: (b, t, 0, 0)))
        else:
            out_shapes.append(jax.ShapeDtypeStruct((bsz, n, t_pad), dtype))
            out_specs.append(pl.BlockSpec((1, n, tile), lambda b, t: (b, 0, t)))
        scales.append(scale)
    kern = functools.partial(_proj_kernel, n_outs=tuple(len(outs) for _, _, outs in std), n_tr=len(tr),
                             scales=tuple(scales), t_valid=t_len, tile=tile, kc=kc)
    return pl.pallas_call(
        kern, grid=(bsz, n_t), in_specs=in_specs, out_specs=out_specs, out_shape=out_shapes,
        compiler_params=_cparams(2), name="proj",
    )(x, *[w for w, *_ in std], *[w for w, *_ in tr])


def _lru_gates(xc, wax_ref, ba, bx, lam):
    blk = wax_ref.shape[1]
    xcb = xc.astype(MXU_DTYPE)
    rs, gs = [], []
    for n in range(wax_ref.shape[0]):
        y = jnp.dot(xcb[:, n * blk:(n + 1) * blk], wax_ref[n], preferred_element_type=f32)
        rs.append(y[:, :blk])
        gs.append(y[:, blk:])
    r = jax.nn.sigmoid(jnp.concatenate(rs, axis=1) + ba)
    i = jax.nn.sigmoid(jnp.concatenate(gs, axis=1) + bx)
    log_a = (-LRU_C) * r * _softplus(-lam)
    a = jnp.exp(log_a)
    th = jnp.tanh(log_a)
    one_minus_a2 = -2.0 * th / (1.0 - th)
    return a, jnp.sqrt(one_minus_a2) * (i * xc)


def _scan8(a8, b8, h_prev):
    row = lax.broadcasted_iota(jnp.int32, a8.shape, 0)
    a, b = a8, b8
    for s in (1, 2, 4):
        ar = pltpu.roll(a, s, axis=0)
        br = pltpu.roll(b, s, axis=0)
        m = row >= s
        b = jnp.where(m, a * br + b, b)
        a = jnp.where(m, a * ar, a)
    return a * h_prev + b


def _rglru_kernel(xr_ref, gr_ref, h0_ref, buf_ref, cw_ref, cb_ref, wax_ref, ba_ref, bx_ref, lam_ref,
                  g_ref, hl_ref, nbuf_ref, a_s, b_s, hc_s, cc_s, *, tstride):
    @pl.when(pl.program_id(1) == 0)
    def _():
        cc_s[...] = buf_ref[0]
        hc_s[...] = jnp.broadcast_to(h0_ref[0], hc_s.shape)

    xr = xr_ref[0]
    rows = xr.shape[0]
    xc, ncarry = _causal_conv(xr, cc_s[...], cw_ref[...], cb_ref[...], tstride)
    cc_s[...] = ncarry
    nbuf_ref[0] = ncarry
    a, inp = _lru_gates(xc, wax_ref, ba_ref[...], bx_ref[...], lam_ref[...])
    if tstride == 1:
        a_s[...] = a
        b_s[...] = inp

        def body(k, h_prev):
            r0 = pl.multiple_of(k * SUBLANES, SUBLANES)
            h = _scan8(a_s[pl.ds(r0, SUBLANES), :], b_s[pl.ds(r0, SUBLANES), :], h_prev)
            b_s[pl.ds(r0, SUBLANES), :] = h
            return jnp.broadcast_to(h[SUBLANES - 1:SUBLANES, :], h.shape)

        h_last = lax.fori_loop(0, rows // SUBLANES, body, hc_s[...])
        hc_s[...] = h_last
        hl_ref[0] = h_last[0:1, :]
        hs = b_s[...]
    else:
        h = hc_s[...]
        pieces = []
        for s in range(rows // tstride):
            h = a[s * tstride:(s + 1) * tstride, :] * h + inp[s * tstride:(s + 1) * tstride, :]
            pieces.append(h)
        hc_s[...] = h
        hl_ref[0] = h
        hs = jnp.concatenate(pieces, axis=0)
    g_ref[0] = (hs * _gelu(gr_ref[0].astype(f32))).astype(g_ref.dtype)


def _rglru(xr, gr, h0, buf, cw, cb, wax, ba, bx, lam, *, tile, tstride):
    bsz, t_len, c = xr.shape
    hr, cr = h0.shape[1], buf.shape[1]
    hrows = SUBLANES if tstride == 1 else hr
    row_spec = pl.BlockSpec((1, tile, c), lambda b, t: (b, t, 0))
    scratch = [pltpu.VMEM((tile, c), f32), pltpu.VMEM((tile, c), f32),
               pltpu.VMEM((hrows, c), f32), pltpu.VMEM((cr, c), f32)]
    return pl.pallas_call(
        functools.partial(_rglru_kernel, tstride=tstride),
        grid=(bsz, t_len // tile),
        in_specs=[row_spec, row_spec,
                  pl.BlockSpec((1, hr, c), lambda b, t: (b, 0, 0)),
                  pl.BlockSpec((1, cr, c), lambda b, t: (b, 0, 0)),
                  _resident(cw.shape), _resident(cb.shape), _resident(wax.shape),
                  _resident(ba.shape), _resident(bx.shape), _resident(lam.shape)],
        out_specs=[row_spec,
                   pl.BlockSpec((1, hr, c), lambda b, t: (b, 0, 0)),
                   pl.BlockSpec((1, cr, c), lambda b, t: (b, 0, 0))],
        out_shape=[jax.ShapeDtypeStruct((bsz, t_len, c), ACT_DTYPE),
                   jax.ShapeDtypeStruct((bsz, hr, c), f32),
                   jax.ShapeDtypeStruct((bsz, cr, c), f32)],
        scratch_shapes=scratch, compiler_params=_cparams(2), name="rglru",
    )(xr, gr, h0, buf, cw, cb, wax, ba, bx, lam)


def _select_threshold(count_ge, count_tie, lo0, hi0, n_allowed, n_keys, topk):
    kf = float(topk)

    def n_active(done):
        return jnp.sum(jnp.where(done, 0.0, 1.0))

    c0 = count_ge(hi0)
    take = jnp.logical_and(n_allowed > kf, c0 >= kf)
    lo = jnp.where(take, hi0, lo0)
    cnt_lo = jnp.where(take, c0, n_allowed)
    cnt_hi = jnp.where(take, 0.0, c0)
    stalled = jnp.zeros_like(lo0)

    def bis_cond(st):
        return jnp.logical_and(st[-1] > 0.0, st[-2] < MAX_BISECT)

    def bis_body(st):
        lo, hi, cnt_lo, cnt_hi, stalled, it, _ = st
        live = jnp.logical_and(cnt_lo > kf, stalled <= 0.0)
        mid = lo + (hi - lo) * 0.5
        c = count_ge(mid)
        up = jnp.logical_and(live, c >= kf)
        dn = jnp.logical_and(live, c < kf)
        stall_now = jnp.logical_and(live, jnp.logical_or(mid <= lo, mid >= hi))
        lo = jnp.where(up, mid, lo)
        cnt_lo = jnp.where(up, c, cnt_lo)
        hi = jnp.where(dn, mid, hi)
        cnt_hi = jnp.where(dn, c, cnt_hi)
        stalled = jnp.where(stall_now, 1.0, stalled)
        done = jnp.logical_or(cnt_lo <= kf, stalled > 0.0)
        return lo, hi, cnt_lo, cnt_hi, stalled, it + 1, n_active(done)

    st = (lo, hi0, cnt_lo, cnt_hi, stalled, jnp.int32(0), n_active(cnt_lo <= kf))
    lo, _, cnt_lo, cnt_hi, _, _, _ = lax.while_loop(bis_cond, bis_body, st)
    t = lo

    need = cnt_lo > kf
    want = kf - cnt_hi

    def tie_cond(st):
        return jnp.logical_and(st[-1] > 0.0, st[-2] < MAX_TIE_BISECT)

    def tie_body(st):
        jl, jh, jf, found, it, _ = st
        live = found <= 0.0
        jm = jnp.floor((jl + jh) * 0.5)
        c = count_tie(t, jm)
        hit = jnp.logical_and(live, c == want)
        jf = jnp.where(hit, jm, jf)
        found = jnp.where(hit, 1.0, found)
        jl = jnp.where(jnp.logical_and(live, c < want), jm, jl)
        jh = jnp.where(jnp.logical_and(live, c > want), jm, jh)
        return jl, jh, jf, found, it + 1, n_active(found > 0.0)

    found0 = jnp.where(need, 0.0, 1.0)
    st = (jnp.zeros_like(t), jnp.zeros_like(t) + n_keys, jnp.full_like(t, NO_TIE_LIMIT), found0,
          jnp.int32(0), n_active(found0 > 0.0))
    _, _, jf, _, _, _ = lax.while_loop(tie_cond, tie_body, st)
    return t, jnp.where(need, jf, NO_TIE_LIMIT)


def _attn_prompt_kernel(qT_ref, qiT_ref, wiT_ref, kb_ref, vT_ref, kib_ref, o_ref,
                        s_ref, q2_ref, acc_ref, m_ref, l_ref, sa_ref, pb_ref, al_ref,
                        *, qb, kc, kc2, topk):
    i = pl.program_id(1)
    n_ih = wiT_ref.shape[1]
    di = qiT_ref.shape[1] // n_ih
    n_grp, dh = q2_ref.shape[0], q2_ref.shape[1]
    grp = q2_ref.shape[2] // qb
    c_last = (i * qb) // kc
    n_ch = c_last + 1
    n_ch2 = (n_ch * kc + kc2 - 1) // kc2

    qi_all = jnp.concatenate([qiT_ref[0, h * di:(h + 1) * di, :] for h in range(n_ih)], axis=1)
    wi = wiT_ref[0]
    kio = lax.broadcasted_iota(jnp.int32, (kc, qb), 0)
    qio = lax.broadcasted_iota(jnp.int32, (kc, qb), 1)

    def chunk_scores(c):
        k0 = pl.multiple_of(c * kc, kc)
        rel = jnp.dot(kib_ref[0, pl.ds(k0, kc), :], qi_all, preferred_element_type=f32)
        sc = wi[0:1, :] * jnp.maximum(rel[:, 0:qb], 0.0)
        for h in range(1, n_ih):
            sc = sc + wi[h:h + 1, :] * jnp.maximum(rel[:, h * qb:(h + 1) * qb], 0.0)
        return k0, sc

    def fold(x, op):
        return op(x.reshape(kc // SUBLANES, SUBLANES, qb), axis=0)

    def p1_body(c, carry):
        vmax, vmin = carry
        k0, sc = chunk_scores(c)
        s_ref[pl.ds(k0, kc), :] = sc
        return jnp.maximum(vmax, fold(sc, jnp.max)), jnp.minimum(vmin, fold(sc, jnp.min))

    vmax, vmin = lax.fori_loop(
        0, c_last, p1_body,
        (jnp.full((SUBLANES, qb), NEG, f32), jnp.full((SUBLANES, qb), BIG, f32)))
    k0, sc = chunk_scores(c_last)
    allowed = (kio + k0) <= (qio + i * qb)
    s_ref[pl.ds(k0, kc), :] = jnp.where(allowed, sc, NEG)
    vmax = jnp.maximum(vmax, fold(jnp.where(allowed, sc, NEG), jnp.max))
    vmin = jnp.minimum(vmin, fold(jnp.where(allowed, sc, BIG), jnp.min))

    def fill_body(c, carry):
        s_ref[pl.ds(pl.multiple_of(c * kc, kc), kc), :] = jnp.full((kc, qb), NEG, f32)
        return carry

    lax.fori_loop(n_ch, n_ch2 * (kc2 // kc), fill_body, 0)

    def csum(x):
        return jnp.sum(x.reshape(kc2 // COUNT_ROWS, COUNT_ROWS, qb), axis=0)

    def count_ge(x):
        def body(c, acc):
            blk = s_ref[pl.ds(pl.multiple_of(c * kc2, kc2), kc2), :]
            return acc + csum(jnp.where(blk >= x, 1.0, 0.0))
        acc = lax.fori_loop(0, n_ch2, body, jnp.zeros((COUNT_ROWS, qb), f32))
        return jnp.sum(acc, axis=0, keepdims=True)

    def count_tie(t, j):
        def body(c, acc):
            r0 = pl.multiple_of(c * kc2, kc2)
            blk = s_ref[pl.ds(r0, kc2), :]
            kidx = (lax.broadcasted_iota(jnp.int32, (kc2, qb), 0) + r0).astype(f32)
            hit = jnp.logical_and(blk == t, kidx < j)
            return acc + csum(jnp.where(hit, 1.0, 0.0))
        acc = lax.fori_loop(0, n_ch2, body, jnp.zeros((COUNT_ROWS, qb), f32))
        return jnp.sum(acc, axis=0, keepdims=True)

    n_allowed = (lax.broadcasted_iota(jnp.int32, (1, qb), 1) + (i * qb + 1)).astype(f32)
    t, jsel = _select_threshold(
        count_ge, count_tie,
        jnp.min(vmin, axis=0, keepdims=True), jnp.max(vmax, axis=0, keepdims=True),
        n_allowed, (n_ch * kc).astype(f32), topk)

    for g in range(n_grp):
        q2_ref[g] = jnp.concatenate(
            [qT_ref[0, (g * grp + j) * dh:(g * grp + j + 1) * dh, :] for j in range(grp)], axis=1)
    m_ref[...] = jnp.full(m_ref.shape, NEG, f32)
    l_ref[...] = jnp.zeros(l_ref.shape, f32)
    acc_ref[...] = jnp.zeros(acc_ref.shape, f32)

    def qk_stage(c, slot):
        k0 = pl.multiple_of(jnp.minimum(c, n_ch - 1) * kc, kc)
        for g in range(n_grp):
            sa_ref[slot, g] = jnp.dot(kb_ref[0, pl.ds(k0, kc), g * dh:(g + 1) * dh], q2_ref[g],
                                      preferred_element_type=f32)

    def softmax_stage(c, slot):
        valid = c < n_ch
        k0 = pl.multiple_of(jnp.minimum(c, n_ch - 1) * kc, kc)
        blk = s_ref[pl.ds(k0, kc), :]
        kidx = (kio + k0).astype(f32)
        t_c = jnp.where(valid, t, BIG)
        j_c = jnp.where(valid, jsel, -1.0)
        sel = jnp.logical_or(blk > t_c, jnp.logical_and(blk == t_c, kidx < j_c))
        bias = jnp.where(sel, 0.0, NEG)
        bias = jnp.concatenate([bias] * grp, axis=1)
        for g in range(n_grp):
            st = sa_ref[slot, g] + bias
            m_old = m_ref[g]
            m_new = jnp.maximum(m_old, jnp.max(st, axis=0, keepdims=True))
            alpha = jnp.exp2(m_old - m_new)
            p = jnp.exp2(st - m_new)
            l_ref[g] = alpha * l_ref[g] + jnp.sum(p, axis=0, keepdims=True)
            m_ref[g] = m_new
            pb_ref[slot, g] = p.astype(pb_ref.dtype)
            al_ref[slot, g] = alpha

    def pv_stage(c, slot):
        cc = jnp.clip(c, 0, n_ch - 1)
        for g in range(n_grp):
            pv = jnp.dot(vT_ref[0, cc, g * dh:(g + 1) * dh, :], pb_ref[slot, g],
                         preferred_element_type=f32)
            acc_ref[g] = al_ref[slot, g] * acc_ref[g] + pv

    qk_stage(0, 0)
    pb_ref[1] = jnp.zeros(pb_ref.shape[1:], pb_ref.dtype)
    al_ref[1] = jnp.ones(al_ref.shape[1:], f32)

    def p3_body(k, carry):
        for slot in range(2):
            c = 2 * k + slot
            pv_stage(c - 1, 1 - slot)
            softmax_stage(c, slot)
            qk_stage(c + 1, 1 - slot)
        return carry

    lax.fori_loop(0, (n_ch + 2) // 2, p3_body, 0)

    for g in range(n_grp):
        o = acc_ref[g] / l_ref[g]
        for j in range(grp):
            h = g * grp + j
            o_ref[0, :, h * dh:(h + 1) * dh] = o[:, j * qb:(j + 1) * qb].T.astype(o_ref.dtype)


def _attn_prompt(qT, qiT, wiT, kb, vT4, kib, *, t_len, topk):
    bsz, dq, t_pad = qT.shape
    qb, kc, kc2 = ATTN_QB, ATTN_KC, ATTN_KC2
    dkv = kb.shape[2]
    dh = dq // N_HEADS
    s_rows = -(-t_pad // kc2) * kc2
    kern = functools.partial(_attn_prompt_kernel, qb=qb, kc=kc, kc2=kc2, topk=topk)
    return pl.pallas_call(
        kern, grid=(bsz, -(-t_len // qb)),
        in_specs=[pl.BlockSpec((1, dq, qb), lambda b, i: (b, 0, i)),
                  pl.BlockSpec((1, qiT.shape[1], qb), lambda b, i: (b, 0, i)),
                  pl.BlockSpec((1, wiT.shape[1], qb), lambda b, i: (b, 0, i)),
                  pl.BlockSpec((1, t_pad, dkv), lambda b, i: (b, 0, 0), pipeline_mode=pl.Buffered(1)),
                  pl.BlockSpec((1,) + vT4.shape[1:], lambda b, i: (b, 0, 0, 0),
                               pipeline_mode=pl.Buffered(1)),
                  pl.BlockSpec((1, t_pad, kib.shape[2]), lambda b, i: (b, 0, 0),
                               pipeline_mode=pl.Buffered(1))],
        out_specs=pl.BlockSpec((1, qb, dq), lambda b, i: (b, i, 0)),
        out_shape=jax.ShapeDtypeStruct((bsz, t_pad, dq), ACT_DTYPE),
        scratch_shapes=[pltpu.VMEM((s_rows, qb), f32),
                        pltpu.VMEM((N_KV_HEADS, dh, KV_GROUP * qb), MXU_DTYPE),
                        pltpu.VMEM((N_KV_HEADS, dh, KV_GROUP * qb), f32),
                        pltpu.VMEM((N_KV_HEADS, 1, KV_GROUP * qb), f32),
                        pltpu.VMEM((N_KV_HEADS, 1, KV_GROUP * qb), f32),
                        pltpu.VMEM((2, N_KV_HEADS, kc, KV_GROUP * qb), f32),
                        pltpu.VMEM((2, N_KV_HEADS, kc, KV_GROUP * qb), MXU_DTYPE),
                        pltpu.VMEM((2, N_KV_HEADS, 1, KV_GROUP * qb), f32)],
        compiler_params=_cparams(2), name="attn_prompt",
    )(qT, qiT, wiT, kb, vT4, kib)


def _sattn_select_kernel(pt_ref, qi_ref, wi_ref, kin_ref, *rest, pg, ns, topk, n_new):
    del pt_ref
    pages = rest[:pg]
    s_ref, t_ref, j_ref = rest[pg:]
    st = pl.program_id(1)
    pgk = pg * PAGE_SIZE
    qi = qi_ref[0]
    wi = wi_ref[0]
    n_ih = qi.shape[0] // n_new

    def scores(keys_t):
        rel = jnp.dot(qi, keys_t, preferred_element_type=f32)
        rel = jnp.maximum(rel, 0.0) * wi
        return jnp.sum(rel.reshape(n_ih, n_new, keys_t.shape[1]), axis=0)

    kp = jnp.concatenate([p[...] for p in pages], axis=1).astype(MXU_DTYPE)
    s_ref[0, st] = scores(kp)

    @pl.when(st == ns - 1)
    def _():
        n_pad = kin_ref.shape[2]
        sn = scores(kin_ref[0])
        lane = lax.broadcasted_iota(jnp.int32, (n_new, n_pad), 1)
        qrow = lax.broadcasted_iota(jnp.int32, (n_new, n_pad), 0)
        ok_new = lane <= qrow
        s_ref[0, ns] = jnp.concatenate(
            [jnp.where(ok_new, sn, NEG), jnp.full((n_new, pgk - n_pad), NEG, f32)], axis=1)

        s_all = s_ref[0]
        past = s_all[:-1]
        hi0 = jnp.maximum(jnp.max(jnp.max(past, axis=0), axis=1, keepdims=True),
                          jnp.max(jnp.where(ok_new, sn, NEG), axis=1, keepdims=True))
        lo0 = jnp.minimum(jnp.min(jnp.min(past, axis=0), axis=1, keepdims=True),
                          jnp.min(jnp.where(ok_new, sn, BIG), axis=1, keepdims=True))
        kidx = (lax.broadcasted_iota(jnp.int32, s_all.shape, 0) * pgk
                + lax.broadcasted_iota(jnp.int32, s_all.shape, 2)).astype(f32)

        def total(x):
            return jnp.sum(jnp.sum(x, axis=0), axis=1, keepdims=True)

        def count_ge(x):
            return total(jnp.where(s_all >= x[None], 1.0, 0.0))

        def count_tie(t, j):
            hit = jnp.logical_and(s_all == t[None], kidx < j[None])
            return total(jnp.where(hit, 1.0, 0.0))

        q1 = lax.broadcasted_iota(jnp.int32, (n_new, 1), 0)
        n_allowed = (q1 + (ns * pgk + 1)).astype(f32)
        t, jsel = _select_threshold(count_ge, count_tie, lo0, hi0, n_allowed, float(ns * pgk + n_pad),
                                    topk)
        t_ref[0] = jnp.broadcast_to(t, t_ref.shape[1:])
        j_ref[0] = jnp.broadcast_to(jsel, j_ref.shape[1:])


def _sattn_attend_kernel(pt_ref, q_ref, s_ref, snew_ref, t_ref, j_ref, knew_ref, vnew_ref, *rest,
                         pg, ns, n_new):
    del pt_ref
    kpages, vpages = rest[:pg], rest[pg:2 * pg]
    o_ref, m_s, l_s, acc_s = rest[2 * pg:]
    st = pl.program_id(1)
    pgk = pg * PAGE_SIZE
    q = q_ref[0]
    dh = q.shape[1]
    rows_g = KV_GROUP * n_new
    t = t_ref[0][:, 0:1]
    jsel = j_ref[0][:, 0:1]

    @pl.when(st == 0)
    def _():
        m_s[...] = jnp.full(m_s.shape, NEG, f32)
        l_s[...] = jnp.zeros(l_s.shape, f32)
        acc_s[...] = jnp.zeros(acc_s.shape, f32)

    def update(sc, base, keys_of, vals_of):
        n = sc.shape[1]
        kidx = (lax.broadcasted_iota(jnp.int32, (n_new, n), 1) + base).astype(f32)
        sel = jnp.logical_or(sc > t, jnp.logical_and(sc == t, kidx < jsel))
        bias = jnp.where(sel, 0.0, NEG)
        bias = jnp.concatenate([bias] * KV_GROUP, axis=0)
        s = jnp.concatenate(
            [lax.dot_general(q[g * rows_g:(g + 1) * rows_g, :], keys_of(g), (((1,), (1,)), ((), ())),
                             preferred_element_type=f32) + bias for g in range(N_KV_HEADS)], axis=0)
        m_old = m_s[...]
        m_new = jnp.maximum(m_old, jnp.max(s, axis=1, keepdims=True))
        alpha = jnp.exp2(m_old - m_new)
        p = jnp.exp2(s - m_new)
        l_s[...] = alpha * l_s[...] + jnp.sum(p, axis=1, keepdims=True)
        m_s[...] = m_new
        pb = p.astype(MXU_DTYPE)
        pv = jnp.concatenate(
            [jnp.dot(pb[g * rows_g:(g + 1) * rows_g, :], vals_of(g), preferred_element_type=f32)
             for g in range(N_KV_HEADS)], axis=0)
        acc_s[...] = alpha * acc_s[...] + pv

    def paged(pages):
        def head_rows(g):
            return jnp.concatenate([p[pl.ds(g, PAGE_SIZE, stride=N_KV_HEADS), :] for p in pages],
                                   axis=0).astype(MXU_DTYPE)
        return head_rows

    def fresh(ref):
        return lambda g: ref[0][:, g * dh:(g + 1) * dh]

    update(s_ref[0, 0], st * pgk, paged(kpages), paged(vpages))

    @pl.when(st == ns - 1)
    def _():
        n_pad = knew_ref.shape[1]
        update(snew_ref[0, 0][:, :n_pad], ns * pgk, fresh(knew_ref), fresh(vnew_ref))
        o_ref[0] = (acc_s[...] / l_s[...]).astype(o_ref.dtype)


def _attn_sample(layer, page_table, cache_ikt, cache_k3, cache_v3, qi_hq, wi_hq, kin_t, q_hq, knew, vnew,
                 *, topk):
    bsz, n_pages = page_table.shape
    n_new = qi_hq.shape[1] // N_IDX_HEADS
    pg = _largest_tile(n_pages, SAMPLE_PAGES_PER_STEP, 1)
    ns = n_pages // pg
    pgk = pg * PAGE_SIZE
    di = cache_ikt.shape[2]
    dh = cache_k3.shape[3]
    dkv = knew.shape[2]
    n_pad = knew.shape[1]
    rows = q_hq.shape[1]

    def page_spec(shape, j):
        return pl.BlockSpec((None, None) + shape, lambda b, s, pt: (layer, pt[b, s * pg + j], 0, 0))

    s_all, thr, jsel = pl.pallas_call(
        functools.partial(_sattn_select_kernel, pg=pg, ns=ns, topk=topk, n_new=n_new),
        grid_spec=pltpu.PrefetchScalarGridSpec(
            num_scalar_prefetch=1, grid=(bsz, ns),
            in_specs=[pl.BlockSpec((1,) + qi_hq.shape[1:], lambda b, s, pt: (b, 0, 0)),
                      pl.BlockSpec((1,) + wi_hq.shape[1:], lambda b, s, pt: (b, 0, 0)),
                      pl.BlockSpec((1, di, n_pad), lambda b, s, pt: (b, 0, 0))]
            + [page_spec((di, PAGE_SIZE), j) for j in range(pg)],
            out_specs=[pl.BlockSpec((1, ns + 1, n_new, pgk), lambda b, s, pt: (b, 0, 0, 0)),
                       pl.BlockSpec((1, n_new, LANES), lambda b, s, pt: (b, 0, 0)),
                       pl.BlockSpec((1, n_new, LANES), lambda b, s, pt: (b, 0, 0))]),
        out_shape=[jax.ShapeDtypeStruct((bsz, ns + 1, n_new, pgk), f32),
                   jax.ShapeDtypeStruct((bsz, n_new, LANES), f32),
                   jax.ShapeDtypeStruct((bsz, n_new, LANES), f32)],
        compiler_params=_cparams(2), name="sattn_select",
    )(page_table, qi_hq, wi_hq, kin_t, *([cache_ikt] * pg))

    kv_page = (PAGE_SIZE * N_KV_HEADS, dh)
    return pl.pallas_call(
        functools.partial(_sattn_attend_kernel, pg=pg, ns=ns, n_new=n_new),
        grid_spec=pltpu.PrefetchScalarGridSpec(
            num_scalar_prefetch=1, grid=(bsz, ns),
            in_specs=[pl.BlockSpec((1, rows, dh), lambda b, s, pt: (b, 0, 0)),
                      pl.BlockSpec((1, 1, n_new, pgk), lambda b, s, pt: (b, s, 0, 0)),
                      pl.BlockSpec((1, 1, n_new, pgk), lambda b, s, pt: (b, ns, 0, 0)),
                      pl.BlockSpec((1, n_new, LANES), lambda b, s, pt: (b, 0, 0)),
                      pl.BlockSpec((1, n_new, LANES), lambda b, s, pt: (b, 0, 0)),
                      pl.BlockSpec((1, n_pad, dkv), lambda b, s, pt: (b, 0, 0)),
                      pl.BlockSpec((1, n_pad, dkv), lambda b, s, pt: (b, 0, 0))]
            + [page_spec(kv_page, j) for j in range(pg)] * 2,
            out_specs=pl.BlockSpec((1, rows, dh), lambda b, s, pt: (b, 0, 0)),
            scratch_shapes=[pltpu.VMEM((rows, 1), f32), pltpu.VMEM((rows, 1), f32),
                            pltpu.VMEM((rows, dh), f32)]),
        out_shape=jax.ShapeDtypeStruct((bsz, rows, dh), ACT_DTYPE),
        compiler_params=_cparams(2), name="sattn_attend",
    )(page_table, q_hq, s_all, s_all, thr, jsel, knew, vnew, *([cache_k3] * pg), *([cache_v3] * pg))


def _merge_kernel(g_ref, o_ref, ga_ref, gb_ref, x_ref, wa_ref, wb_ref, wo_ref, lg_ref, lb_ref, out_ref,
                  *, alpha):
    ya = jnp.dot(g_ref[0].astype(MXU_DTYPE), wa_ref[...], preferred_element_type=f32)
    yb = jnp.dot(o_ref[0].astype(MXU_DTYPE), wb_ref[...], preferred_element_type=f32)
    mixed = (jax.nn.sigmoid(ga_ref[0].astype(f32)) * ya + jax.nn.sigmoid(gb_ref[0].astype(f32)) * yb)
    mix = jnp.dot(mixed.astype(MXU_DTYPE), wo_ref[...], preferred_element_type=f32)
    out_ref[0] = _layer_norm(alpha * x_ref[0] + mix, lg_ref[...], lb_ref[...])


def _merge(g, o, ga, gb, x, wa, wb, wo, lg, lb, *, tile, alpha):
    bsz, t_len, d = x.shape
    row_spec = pl.BlockSpec((1, tile, d), lambda b, t: (b, t, 0))
    return pl.pallas_call(
        functools.partial(_merge_kernel, alpha=alpha), grid=(bsz, t_len // tile),
        in_specs=[row_spec] * 5 + [_resident(a.shape) for a in (wa, wb, wo, lg, lb)],
        out_specs=row_spec, out_shape=jax.ShapeDtypeStruct((bsz, t_len, d), f32),
        compiler_params=_cparams(2), name="merge",
    )(g, o, ga, gb, x, wa, wb, wo, lg, lb)


def _ffn_kernel(x_ref, buf_ref, wup_ref, cw_ref, cb_ref, wdn_ref, lg_ref, lb_ref, out_ref, nbuf_ref,
                carry_ref, *, alpha, tstride, ck):
    @pl.when(pl.program_id(1) == 0)
    def _():
        carry_ref[...] = buf_ref[0]

    x = x_ref[0]
    xb = x.astype(MXU_DTYPE)
    d_ff = wdn_ref.shape[0]
    acc = jnp.zeros(x.shape, f32)
    for c in range(d_ff // ck):
        halves = []
        for off in (c * ck, d_ff + c * ck):
            u = jnp.dot(xb, wup_ref[:, off:off + ck], preferred_element_type=f32)
            uc, ncarry = _causal_conv(u, carry_ref[:, off:off + ck], cw_ref[:, off:off + ck],
                                      cb_ref[:, off:off + ck], tstride)
            carry_ref[:, off:off + ck] = ncarry
            halves.append(uc)
        act = (_gelu(halves[0]) * halves[1]).astype(MXU_DTYPE)
        acc = acc + jnp.dot(act, wdn_ref[c * ck:(c + 1) * ck, :], preferred_element_type=f32)
    out_ref[0] = _layer_norm(alpha * x + acc, lg_ref[...], lb_ref[...])
    nbuf_ref[0] = carry_ref[...]


def _ffn(x, buf, wup, cw, cb, wdn, lg, lb, *, tile, alpha, tstride):
    bsz, t_len, d = x.shape
    cr, f2 = buf.shape[1], buf.shape[2]
    row_spec = pl.BlockSpec((1, tile, d), lambda b, t: (b, t, 0))
    buf_spec = pl.BlockSpec((1, cr, f2), lambda b, t: (b, 0, 0))
    return pl.pallas_call(
        functools.partial(_ffn_kernel, alpha=alpha, tstride=tstride, ck=512),
        grid=(bsz, t_len // tile),
        in_specs=[row_spec, buf_spec] + [_resident(a.shape) for a in (wup, cw, cb, wdn, lg, lb)],
        out_specs=[row_spec, buf_spec],
        out_shape=[jax.ShapeDtypeStruct((bsz, t_len, d), f32),
                   jax.ShapeDtypeStruct((bsz, cr, f2), f32)],
        scratch_shapes=[pltpu.VMEM((cr, f2), f32)],
        compiler_params=_cparams(2), name="ffn",
    )(x, buf, wup, cw, cb, wdn, lg, lb)


def _layer_weights(l, w_in, rnn_conv_w, rnn_conv_b, lru_wa, lru_ba, lru_wx, lru_bx, lru_lambda,
                   w_branch_a, w_branch_b, w_out, ln1_g, ln1_b, ffn_w_up, ffn_conv_w, ffn_conv_b,
                   ffn_w_down, ln2_g, ln2_b):
    d = w_in.shape[1]
    dh = d // N_HEADS
    di = (w_in.shape[2] - 4 * d - (N_HEADS + 2 * N_KV_HEADS) * dh - N_IDX_HEADS) // (N_IDX_HEADS + 1)
    sizes = (d, d, N_HEADS * dh, N_KV_HEADS * dh, N_KV_HEADS * dh, N_IDX_HEADS * di, di, N_IDX_HEADS, d, d)
    names = ("xr", "gr", "q", "k", "v", "qi", "ki", "wi", "ga", "gb")
    cols, off = {}, 0
    for name, size in zip(names, sizes):
        cols[name] = w_in[l][:, off:off + size].astype(MXU_DTYPE)
        off += size
    row = lambda v: v[l][None, :]
    return dict(
        cols=cols, dh=dh, di=di,
        cw=rnn_conv_w[l], cb=row(rnn_conv_b),
        wax=jnp.concatenate([lru_wa[l], lru_wx[l]], axis=2).astype(MXU_DTYPE),
        ba=row(lru_ba), bx=row(lru_bx), lam=row(lru_lambda),
        wa=w_branch_a[l].astype(MXU_DTYPE), wb=w_branch_b[l].astype(MXU_DTYPE),
        wo=w_out[l].astype(MXU_DTYPE), lg1=row(ln1_g), lb1=row(ln1_b),
        wup=ffn_w_up[l].astype(MXU_DTYPE), fcw=ffn_conv_w[l], fcb=row(ffn_conv_b),
        wdn=ffn_w_down[l].astype(MXU_DTYPE), lg2=row(ln2_g), lb2=row(ln2_b))


def _prompt_layer(x, w, alpha):
    bsz, t_len, d = x.shape
    dh, di, cols = w["dh"], w["di"], w["cols"]
    kc = ATTN_KC
    t_pad = -(-t_len // kc) * kc
    ptile = kc * _largest_tile(t_pad // kc, 3, 1)
    rtile = _largest_tile(t_len, 1024, 16)
    topk = min(TOPK_MAX, t_len // 4)
    one = lambda dtype: [(dtype, False)]
    std = [(cols["xr"], 1.0, one(f32)), (cols["gr"], 1.0, one(ACT_DTYPE)),
           (cols["k"], 1.0, [(f32, False), (MXU_DTYPE, True)]), (cols["v"], 1.0, one(f32)),
           (cols["ki"], 1.0, [(f32, False), (MXU_DTYPE, True)]),
           (cols["ga"], 1.0, one(ACT_DTYPE)), (cols["gb"], 1.0, one(ACT_DTYPE))]
    tr = [(cols["q"].T, dh ** -0.5 * LOG2E, MXU_DTYPE, False), (cols["qi"].T, di ** -0.5, MXU_DTYPE, False),
          (cols["wi"].T, N_IDX_HEADS ** -0.5, f32, False), (cols["v"].T, 1.0, MXU_DTYPE, True)]
    xr, gr, k, kb, v, ki, kib, ga, gb, qT, qiT, wiT, vT4 = _project(
        x, std, tr, tile=ptile, t_pad=t_pad, kc=kc)

    c_rnn, w_rnn = xr.shape[2], w["cw"].shape[0]
    g, h_last, nbuf = _rglru(
        xr, gr, jnp.zeros((bsz, 1, c_rnn), f32), jnp.zeros((bsz, SUBLANES, c_rnn), f32),
        w["cw"], w["cb"], w["wax"], w["ba"], w["bx"], w["lam"], tile=rtile, tstride=1)
    o = _attn_prompt(qT, qiT, wiT, kb, vT4, kib, t_len=t_len, topk=topk)
    x1 = _merge(g, o, ga, gb, x, w["wa"], w["wb"], w["wo"], w["lg1"], w["lb1"], tile=rtile, alpha=alpha)
    f2, w_ffn = w["fcw"].shape[1], w["fcw"].shape[0]
    x2, fbuf = _ffn(x1, jnp.zeros((bsz, SUBLANES, f2), f32), w["wup"], w["fcw"], w["fcb"], w["wdn"],
                    w["lg2"], w["lb2"], tile=_largest_tile(t_len, 512, 16), alpha=alpha, tstride=1)
    state = (k.reshape(bsz, t_len, N_KV_HEADS, dh), v.reshape(bsz, t_len, N_KV_HEADS, dh), ki,
             h_last[:, 0, :], nbuf[:, SUBLANES - (w_rnn - 1):, :], fbuf[:, SUBLANES - (w_ffn - 1):, :])
    return x2, state


def _to_time_major(a):
    a = jnp.swapaxes(a, 0, 1)
    return a.reshape((1, a.shape[0] * a.shape[1]) + a.shape[2:])


def _from_time_major(a, bsz):
    a = a.reshape((a.shape[1] // bsz, bsz) + a.shape[2:])
    return jnp.swapaxes(a, 0, 1)


def _sample_layer(x, w, alpha, layer, page_table, cache_ikt, cache_k3, cache_v3, h0, lru_buf, ffn_buf, bsz):
    rows, d = x.shape[1], x.shape[2]
    n_new = rows // bsz
    dh, di, cols = w["dh"], w["di"], w["cols"]
    past = page_table.shape[1] * PAGE_SIZE
    topk = min(TOPK_MAX, (past + n_new) // 4)
    one = lambda dtype: [(dtype, False)]
    std = [(cols["xr"], 1.0, one(f32)), (cols["gr"], 1.0, one(ACT_DTYPE)),
           (cols["q"], dh ** -0.5 * LOG2E, one(MXU_DTYPE)), (cols["k"], 1.0, one(f32)),
           (cols["v"], 1.0, one(f32)), (cols["qi"], di ** -0.5, one(MXU_DTYPE)),
           (cols["ki"], 1.0, one(f32)), (cols["wi"], N_IDX_HEADS ** -0.5, one(f32)),
           (cols["ga"], 1.0, one(ACT_DTYPE)), (cols["gb"], 1.0, one(ACT_DTYPE))]
    xr, gr, q, k, v, qi, ki, wi, ga, gb = _project(x, std, [], tile=rows, t_pad=rows, kc=ATTN_KC)

    g, h_last, nbuf = _rglru(xr, gr, h0[None], _to_time_major(lru_buf), w["cw"], w["cb"], w["wax"],
                             w["ba"], w["bx"], w["lam"], tile=rows, tstride=bsz)

    def heads_major(a, n_h):
        a = _from_time_major(a, bsz).reshape(bsz, n_new, n_h, -1)
        return jnp.swapaxes(a, 1, 2).reshape(bsz, n_h * n_new, -1)

    def pad_new(a):
        a = _from_time_major(a, bsz).astype(MXU_DTYPE)
        return jnp.pad(a, ((0, 0), (0, LANES - n_new), (0, 0)))

    kin_t = jnp.swapaxes(pad_new(ki), 1, 2)
    o = _attn_sample(layer, page_table, cache_ikt, cache_k3, cache_v3,
                     heads_major(qi, N_IDX_HEADS), heads_major(wi, N_IDX_HEADS), kin_t,
                     heads_major(q, N_HEADS), pad_new(k), pad_new(v), topk=topk)
    o = jnp.swapaxes(o.reshape(bsz, N_HEADS, n_new, dh), 1, 2).reshape(bsz, n_new, N_HEADS * dh)
    o = _to_time_major(o)

    x1 = _merge(g, o, ga, gb, x, w["wa"], w["wb"], w["wo"], w["lg1"], w["lb1"], tile=rows, alpha=alpha)
    x2, fbuf = _ffn(x1, _to_time_major(ffn_buf), w["wup"], w["fcw"], w["fcb"], w["wdn"], w["lg2"],
                    w["lb2"], tile=rows, alpha=alpha, tstride=bsz)
    state = (_from_time_major(k, bsz).reshape(bsz, n_new, N_KV_HEADS, dh),
             _from_time_major(v, bsz).reshape(bsz, n_new, N_KV_HEADS, dh),
             _from_time_major(ki, bsz), h_last[0], _from_time_major(nbuf, bsz), _from_time_major(fbuf, bsz))
    return x2, state


def kernel(x_prompt, x_sample, cache_k, cache_v, cache_idx_k, state_lru_h, state_lru_conv, state_ffn_conv,
           page_table, meta_tokens, w_in, rnn_conv_w, rnn_conv_b, lru_wa, lru_ba, lru_wx, lru_bx, lru_lambda,
           w_branch_a, w_branch_b, w_out, ln1_g, ln1_b, ffn_w_up, ffn_conv_w, ffn_conv_b, ffn_w_down,
           ln2_g, ln2_b):
    depth = w_in.shape[0]
    alpha = (2.0 * depth) ** 0.25
    bsz, _, d = x_prompt.shape
    dbsz = x_sample.shape[0]
    meta = jnp.broadcast_to(meta_tokens.astype(x_prompt.dtype), (bsz, N_META, d))
    xp = jnp.concatenate([meta, x_prompt], axis=1)
    xs = _to_time_major(x_sample)
    cache_k3 = cache_k.reshape(cache_k.shape[:2] + (-1, cache_k.shape[4]))
    cache_v3 = cache_v.reshape(cache_v.shape[:2] + (-1, cache_v.shape[4]))
    cache_ikt = jnp.swapaxes(cache_idx_k, 2, 3)
    p_states, s_states = [], []
    for l in range(depth):
        w = _layer_weights(l, w_in, rnn_conv_w, rnn_conv_b, lru_wa, lru_ba, lru_wx, lru_bx, lru_lambda,
                           w_branch_a, w_branch_b, w_out, ln1_g, ln1_b, ffn_w_up, ffn_conv_w, ffn_conv_b,
                           ffn_w_down, ln2_g, ln2_b)
        xp, st = _prompt_layer(xp, w, alpha)
        p_states.append(st)
        xs, st = _sample_layer(xs, w, alpha, l, page_table, cache_ikt, cache_k3, cache_v3,
                               state_lru_h[l], state_lru_conv[l], state_ffn_conv[l], dbsz)
        s_states.append(st)
    stack = lambda states, n: jnp.stack([st[n] for st in states])
    return ((xp[:, N_META:], _from_time_major(xs, dbsz))
            + tuple(stack(p_states, n) for n in range(6))
            + tuple(stack(s_states, n) for n in range(6)))
```

```python
import functools

import jax
import jax.numpy as jnp
from jax import lax
from jax.experimental import pallas as pl
from jax.experimental.pallas import tpu as pltpu

f32 = jnp.float32

N_META = 16
N_RNN_BLOCKS = 8
LRU_C = 8.0
N_HEADS = 8
N_KV_HEADS = 4
KV_GROUP = N_HEADS // N_KV_HEADS
N_IDX_HEADS = 8
TOPK_MAX = 256
PAGE_SIZE = 128
LN_EPS = 1e-5

NEG = -1e30
BIG = 1e30
NO_TIE_LIMIT = 1e9
LOG2E = 1.4426950408889634

MXU_DTYPE = jnp.bfloat16
ACT_DTYPE = jnp.bfloat16

V7X_VMEM_BYTES = 64 * 1024 * 1024
VMEM_LIMIT = V7X_VMEM_BYTES * 7 // 8
LANES = 128
SUBLANES = 8

ATTN_QB = 128
ATTN_KC = 256
ATTN_KC2 = 1024
COUNT_CHAINS = 8
COUNT_ROWS = COUNT_CHAINS * SUBLANES
MAX_BISECT = 4096
MAX_TIE_BISECT = 64
SAMPLE_PAGES_PER_STEP = 16


def _cparams(n_grid):
    return pltpu.CompilerParams(
        dimension_semantics=("arbitrary",) * n_grid, vmem_limit_bytes=VMEM_LIMIT)


def _resident(shape):
    nd = len(shape)
    return pl.BlockSpec(tuple(shape), lambda *_: (0,) * nd, pipeline_mode=pl.Buffered(1))


def _largest_tile(n, limit, multiple):
    best = None
    for d in range(multiple, min(n, limit) + 1, multiple):
        if n % d == 0:
            best = d
    assert best is not None, (n, limit, multiple)
    return best


def _gelu(x):
    return jax.nn.gelu(x, approximate=True)


def _softplus(x):
    return jnp.maximum(x, 0.0) + jnp.log1p(jnp.exp(-jnp.abs(x)))


def _layer_norm(y, g, b):
    mu = jnp.mean(y, axis=-1, keepdims=True)
    d = y - mu
    var = jnp.mean(d * d, axis=-1, keepdims=True)
    return d * lax.rsqrt(var + LN_EPS) * g + b


def _causal_conv(u, carry, w, b, tstride):
    width = w.shape[0]
    rows = u.shape[0]
    y = b + w[width - 1:width, :] * u
    if tstride % SUBLANES == 0:
        ext = jnp.concatenate([carry, u], axis=0)
        for j in range(width - 1):
            y = y + w[j:j + 1, :] * ext[j * tstride:j * tstride + rows, :]
        return y, ext[rows:, :]
    assert tstride == 1 and carry.shape[0] == SUBLANES
    row = lax.broadcasted_iota(jnp.int32, (SUBLANES, u.shape[1]), 0)
    for s in range(1, width):
        rolled = pltpu.roll(u, s, axis=0)
        head = jnp.where(row < s, pltpu.roll(carry, s, axis=0), rolled[0:SUBLANES, :])
        shifted = jnp.concatenate([head, rolled[SUBLANES:, :]], axis=0)
        y = y + w[width - 1 - s:width - s, :] * shifted
    return y, u[rows - SUBLANES:, :]


def _proj_kernel(x_ref, *refs, n_outs, n_tr, scales, t_valid, tile, kc):
    n_w = len(n_outs) + n_tr
    w_refs = refs[:n_w]
    o_refs = list(refs[n_w:])
    t0 = pl.program_id(1) * tile
    row = lax.broadcasted_iota(jnp.int32, (tile, 1), 0) + t0
    xb = jnp.where(row < t_valid, x_ref[0], 0.0).astype(MXU_DTYPE)
    for n, n_out in enumerate(n_outs):
        y = jnp.dot(xb, w_refs[n][...], preferred_element_type=f32)
        if scales[n] != 1.0:
            y = y * scales[n]
        for _ in range(n_out):
            o = o_refs.pop(0)
            o[0] = y.astype(o.dtype)
    for n in range(len(n_outs), n_w):
        y = lax.dot_general(w_refs[n][...], xb, (((1,), (1,)), ((), ())),
                            preferred_element_type=f32)
        if scales[n] != 1.0:
            y = y * scales[n]
        o = o_refs.pop(0)
        if len(o.shape) == 4:
            for cc in range(tile // kc):
                o[0, cc] = y[:, cc * kc:(cc + 1) * kc].astype(o.dtype)
        else:
            o[0] = y.astype(o.dtype)


def _project(x, std, tr, *, tile, t_pad, kc):
    bsz, t_len, d = x.shape
    n_t = t_pad // tile
    in_specs = [pl.BlockSpec((1, tile, d), lambda b, t: (b, t, 0))]
    out_shapes, out_specs, scales = [], [], []
    for w, scale, outs in std:
        in_specs.append(_resident(w.shape))
        n = w.shape[1]
        for dtype, padded in outs:
            out_shapes.append(jax.ShapeDtypeStruct((bsz, t_pad if padded else t_len, n), dtype))
            out_specs.append(pl.BlockSpec((1, tile, n), lambda b, t: (b, t, 0)))
        scales.append(scale)
    for w, scale, dtype, chunked in tr:
        in_specs.append(_resident(w.shape))
        n = w.shape[0]
        if chunked:
            out_shapes.append(jax.ShapeDtypeStruct((bsz, t_pad // kc, n, kc), dtype))
            out_specs.append(pl.BlockSpec((1, tile // kc, n, kc), lambda b, t: (b, t, 0, 0)))
        else:
            out_shapes.append(jax.ShapeDtypeStruct((bsz, n, t_pad), dtype))
            out_specs.append(pl.BlockSpec((1, n, tile), lambda b, t: (b, 0, t)))
        scales.append(scale)
    kern = functools.partial(_proj_kernel, n_outs=tuple(len(outs) for _, _, outs in std), n_tr=len(tr),
                             scales=tuple(scales), t_valid=t_len, tile=tile, kc=kc)
    return pl.pallas_call(
        kern, grid=(bsz, n_t), in_specs=in_specs, out_specs=out_specs, out_shape=out_shapes,
        compiler_params=_cparams(2), name="proj",
    )(x, *[w for w, *_ in std], *[w for w, *_ in tr])


def _lru_gates(xc, wax_ref, ba, bx, lam):
    blk = wax_ref.shape[1]
    xcb = xc.astype(MXU_DTYPE)
    rs, gs = [], []
    for n in range(wax_ref.shape[0]):
        y = jnp.dot(xcb[:, n * blk:(n + 1) * blk], wax_ref[n], preferred_element_type=f32)
        rs.append(y[:, :blk])
        gs.append(y[:, blk:])
    r = jax.nn.sigmoid(jnp.concatenate(rs, axis=1) + ba)
    i = jax.nn.sigmoid(jnp.concatenate(gs, axis=1) + bx)
    log_a = (-LRU_C) * r * _softplus(-lam)
    a = jnp.exp(log_a)
    th = jnp.tanh(log_a)
    one_minus_a2 = -2.0 * th / (1.0 - th)
    return a, jnp.sqrt(one_minus_a2) * (i * xc)


def _scan8(a8, b8, h_prev):
    row = lax.broadcasted_iota(jnp.int32, a8.shape, 0)
    a, b = a8, b8
    for s in (1, 2, 4):
        ar = pltpu.roll(a, s, axis=0)
        br = pltpu.roll(b, s, axis=0)
        m = row >= s
        b = jnp.where(m, a * br + b, b)
        a = jnp.where(m, a * ar, a)
    return a * h_prev + b


def _rglru_kernel(xr_ref, gr_ref, h0_ref, buf_ref, cw_ref, cb_ref, wax_ref, ba_ref, bx_ref, lam_ref,
                  g_ref, hl_ref, nbuf_ref, a_s, b_s, hc_s, cc_s, *, tstride):
    @pl.when(pl.program_id(1) == 0)
    def _():
        cc_s[...] = buf_ref[0]
        hc_s[...] = jnp.broadcast_to(h0_ref[0], hc_s.shape)

    xr = xr_ref[0]
    rows = xr.shape[0]
    xc, ncarry = _causal_conv(xr, cc_s[...], cw_ref[...], cb_ref[...], tstride)
    cc_s[...] = ncarry
    nbuf_ref[0] = ncarry
    a, inp = _lru_gates(xc, wax_ref, ba_ref[...], bx_ref[...], lam_ref[...])
    if tstride == 1:
        a_s[...] = a
        b_s[...] = inp

        def body(k, h_prev):
            r0 = pl.multiple_of(k * SUBLANES, SUBLANES)
            h = _scan8(a_s[pl.ds(r0, SUBLANES), :], b_s[pl.ds(r0, SUBLANES), :], h_prev)
            b_s[pl.ds(r0, SUBLANES), :] = h
            return jnp.broadcast_to(h[SUBLANES - 1:SUBLANES, :], h.shape)

        h_last = lax.fori_loop(0, rows // SUBLANES, body, hc_s[...])
        hc_s[...] = h_last
        hl_ref[0] = h_last[0:1, :]
        hs = b_s[...]
    else:
        h = hc_s[...]
        pieces = []
        for s in range(rows // tstride):
            h = a[s * tstride:(s + 1) * tstride, :] * h + inp[s * tstride:(s + 1) * tstride, :]
            pieces.append(h)
        hc_s[...] = h
        hl_ref[0] = h
        hs = jnp.concatenate(pieces, axis=0)
    g_ref[0] = (hs * _gelu(gr_ref[0].astype(f32))).astype(g_ref.dtype)


def _rglru(xr, gr, h0, buf, cw, cb, wax, ba, bx, lam, *, tile, tstride):
    bsz, t_len, c = xr.shape
    hr, cr = h0.shape[1], buf.shape[1]
    hrows = SUBLANES if tstride == 1 else hr
    row_spec = pl.BlockSpec((1, tile, c), lambda b, t: (b, t, 0))
    scratch = [pltpu.VMEM((tile, c), f32), pltpu.VMEM((tile, c), f32),
               pltpu.VMEM((hrows, c), f32), pltpu.VMEM((cr, c), f32)]
    return pl.pallas_call(
        functools.partial(_rglru_kernel, tstride=tstride),
        grid=(bsz, t_len // tile),
        in_specs=[row_spec, row_spec,
                  pl.BlockSpec((1, hr, c), lambda b, t: (b, 0, 0)),
                  pl.BlockSpec((1, cr, c), lambda b, t: (b, 0, 0)),
                  _resident(cw.shape), _resident(cb.shape), _resident(wax.shape),
                  _resident(ba.shape), _resident(bx.shape), _resident(lam.shape)],
        out_specs=[row_spec,
                   pl.BlockSpec((1, hr, c), lambda b, t: (b, 0, 0)),
                   pl.BlockSpec((1, cr, c), lambda b, t: (b, 0, 0))],
        out_shape=[jax.ShapeDtypeStruct((bsz, t_len, c), ACT_DTYPE),
                   jax.ShapeDtypeStruct((bsz, hr, c), f32),
                   jax.ShapeDtypeStruct((bsz, cr, c), f32)],
        scratch_shapes=scratch, compiler_params=_cparams(2), name="rglru",
    )(xr, gr, h0, buf, cw, cb, wax, ba, bx, lam)


def _select_threshold(count_ge, count_tie, lo0, hi0, n_allowed, n_keys, topk):
    kf = float(topk)

    def n_active(done):
        return jnp.sum(jnp.where(done, 0.0, 1.0))

    c0 = count_ge(hi0)
    take = jnp.logical_and(n_allowed > kf, c0 >= kf)
    lo = jnp.where(take, hi0, lo0)
    cnt_lo = jnp.where(take, c0, n_allowed)
    cnt_hi = jnp.where(take, 0.0, c0)
    stalled = jnp.zeros_like(lo0)

    def bis_cond(st):
        return jnp.logical_and(st[-1] > 0.0, st[-2] < MAX_BISECT)

    def bis_body(st):
        lo, hi, cnt_lo, cnt_hi, stalled, it, _ = st
        live = jnp.logical_and(cnt_lo > kf, stalled <= 0.0)
        n_live = jnp.sum(jnp.where(live, 1.0, 0.0))
        mid = lo + (hi - lo) * 0.5
        c = count_ge(mid)
        up = jnp.logical_and(live, c >= kf)
        dn = jnp.logical_and(live, c < kf)
        stall_now = jnp.logical_and(live, jnp.logical_or(mid <= lo, mid >= hi))
        lo = jnp.where(up, mid, lo)
        cnt_lo = jnp.where(up, c, cnt_lo)
        hi = jnp.where(dn, mid, hi)
        cnt_hi = jnp.where(dn, c, cnt_hi)
        stalled = jnp.where(stall_now, 1.0, stalled)
        return lo, hi, cnt_lo, cnt_hi, stalled, it + 1, n_live

    st = (lo, hi0, cnt_lo, cnt_hi, stalled, jnp.int32(0), n_active(cnt_lo <= kf))
    lo, _, cnt_lo, cnt_hi, _, _, _ = lax.while_loop(bis_cond, bis_body, st)
    t = lo

    need = cnt_lo > kf
    want = kf - cnt_hi

    def tie_cond(st):
        return jnp.logical_and(st[-1] > 0.0, st[-2] < MAX_TIE_BISECT)

    def tie_body(st):
        jl, jh, jf, found, it, _ = st
        live = found <= 0.0
        jm = jnp.floor((jl + jh) * 0.5)
        c = count_tie(t, jm)
        hit = jnp.logical_and(live, c == want)
        jf = jnp.where(hit, jm, jf)
        found = jnp.where(hit, 1.0, found)
        jl = jnp.where(jnp.logical_and(live, c < want), jm, jl)
        jh = jnp.where(jnp.logical_and(live, c > want), jm, jh)
        return jl, jh, jf, found, it + 1, n_active(found > 0.0)

    found0 = jnp.where(need, 0.0, 1.0)
    st = (jnp.zeros_like(t), jnp.zeros_like(t) + n_keys, jnp.full_like(t, NO_TIE_LIMIT), found0,
          jnp.int32(0), n_active(found0 > 0.0))
    _, _, jf, _, _, _ = lax.while_loop(tie_cond, tie_body, st)
    return t, jnp.where(need, jf, NO_TIE_LIMIT)


def _attn_prompt_kernel(qT_ref, qiT_ref, wiT_ref, kb_ref, vT_ref, kib_ref, o_ref,
                        s_ref, q2_ref, acc_ref, m_ref, l_ref, sa_ref, pb_ref, al_ref,
                        *, qb, kc, kc2, topk):
    i = pl.program_id(1)
    n_ih = wiT_ref.shape[1]
    di = qiT_ref.shape[1] // n_ih
    n_grp, dh = q2_ref.shape[0], q2_ref.shape[1]
    grp = q2_ref.shape[2] // qb
    c_last = (i * qb) // kc
    n_ch = c_last + 1
    n_ch2 = (n_ch * kc + kc2 - 1) // kc2

    qi_all = jnp.concatenate([qiT_ref[0, h * di:(h + 1) * di, :] for h in range(n_ih)], axis=1)
    wi = wiT_ref[0]
    kio = lax.broadcasted_iota(jnp.int32, (kc, qb), 0)
    qio = lax.broadcasted_iota(jnp.int32, (kc, qb), 1)

    def chunk_scores(c):
        k0 = pl.multiple_of(c * kc, kc)
        rel = jnp.dot(kib_ref[0, pl.ds(k0, kc), :], qi_all, preferred_element_type=f32)
        sc = wi[0:1, :] * jnp.maximum(rel[:, 0:qb], 0.0)
        for h in range(1, n_ih):
            sc = sc + wi[h:h + 1, :] * jnp.maximum(rel[:, h * qb:(h + 1) * qb], 0.0)
        return k0, sc

    def fold(x, op):
        return op(x.reshape(kc // SUBLANES, SUBLANES, qb), axis=0)

    def p1_body(k, carry):
        vmax, vmin = carry
        for c in (2 * k, jnp.minimum(2 * k + 1, c_last - 1)):
            k0, sc = chunk_scores(c)
            s_ref[pl.ds(k0, kc), :] = sc
            vmax = jnp.maximum(vmax, fold(sc, jnp.max))
            vmin = jnp.minimum(vmin, fold(sc, jnp.min))
        return vmax, vmin

    vmax, vmin = lax.fori_loop(
        0, (c_last + 1) // 2, p1_body,
        (jnp.full((SUBLANES, qb), NEG, f32), jnp.full((SUBLANES, qb), BIG, f32)))
    k0, sc = chunk_scores(c_last)
    allowed = (kio + k0) <= (qio + i * qb)
    s_ref[pl.ds(k0, kc), :] = jnp.where(allowed, sc, NEG)
    vmax = jnp.maximum(vmax, fold(jnp.where(allowed, sc, NEG), jnp.max))
    vmin = jnp.minimum(vmin, fold(jnp.where(allowed, sc, BIG), jnp.min))

    def fill_body(c, carry):
        s_ref[pl.ds(pl.multiple_of(c * kc, kc), kc), :] = jnp.full((kc, qb), NEG, f32)
        return carry

    lax.fori_loop(n_ch, n_ch2 * (kc2 // kc), fill_body, 0)

    def count_slabs(hit_fn):
        def body(c, acc):
            r0 = pl.multiple_of(c * kc2, kc2)
            for s in range(kc2 // COUNT_ROWS):
                r = pl.multiple_of(r0 + s * COUNT_ROWS, COUNT_ROWS)
                acc = jnp.where(hit_fn(s_ref[pl.ds(r, COUNT_ROWS), :], r), acc + 1.0, acc)
            return acc
        acc = lax.fori_loop(0, n_ch2, body, jnp.zeros((COUNT_ROWS, qb), f32))
        return jnp.sum(acc, axis=0, keepdims=True)

    def count_ge(x):
        xb = jnp.broadcast_to(x, (COUNT_ROWS, qb))
        return count_slabs(lambda blk, r: blk >= xb)

    def count_tie(t, j):
        tb = jnp.broadcast_to(t, (COUNT_ROWS, qb))
        row = lax.broadcasted_iota(jnp.int32, (COUNT_ROWS, qb), 0).astype(f32)

        def hit(blk, r):
            return jnp.logical_and(blk == tb, row < j - r.astype(f32))
        return count_slabs(hit)

    n_allowed = (lax.broadcasted_iota(jnp.int32, (1, qb), 1) + (i * qb + 1)).astype(f32)
    t, jsel = _select_threshold(
        count_ge, count_tie,
        jnp.min(vmin, axis=0, keepdims=True), jnp.max(vmax, axis=0, keepdims=True),
        n_allowed, (n_ch * kc).astype(f32), topk)

    for g in range(n_grp):
        q2_ref[g] = jnp.concatenate(
            [qT_ref[0, (g * grp + j) * dh:(g * grp + j + 1) * dh, :] for j in range(grp)], axis=1)
    m_ref[...] = jnp.full(m_ref.shape, NEG, f32)
    l_ref[...] = jnp.zeros(l_ref.shape, f32)
    acc_ref[...] = jnp.zeros(acc_ref.shape, f32)

    def qk_stage(c, slot):
        k0 = pl.multiple_of(jnp.minimum(c, n_ch - 1) * kc, kc)
        for g in range(n_grp):
            sa_ref[slot, g] = jnp.dot(kb_ref[0, pl.ds(k0, kc), g * dh:(g + 1) * dh], q2_ref[g],
                                      preferred_element_type=f32)

    def softmax_stage(c, slot):
        valid = c < n_ch
        k0 = pl.multiple_of(jnp.minimum(c, n_ch - 1) * kc, kc)
        blk = s_ref[pl.ds(k0, kc), :]
        kidx = (kio + k0).astype(f32)
        t_c = jnp.where(valid, t, BIG)
        j_c = jnp.where(valid, jsel, -1.0)
        sel = jnp.logical_or(blk > t_c, jnp.logical_and(blk == t_c, kidx < j_c))
        bias = jnp.where(sel, 0.0, NEG)
        bias = jnp.concatenate([bias] * grp, axis=1)
        for g in range(n_grp):
            st = sa_ref[slot, g] + bias
            m_old = m_ref[g]
            m_new = jnp.maximum(m_old, jnp.max(st, axis=0, keepdims=True))
            alpha = jnp.exp2(m_old - m_new)
            p = jnp.exp2(st - m_new)
            l_ref[g] = alpha * l_ref[g] + jnp.sum(p, axis=0, keepdims=True)
            m_ref[g] = m_new
            pb_ref[slot, g] = p.astype(pb_ref.dtype)
            al_ref[slot, g] = alpha

    def pv_stage(c, slot):
        cc = jnp.clip(c, 0, n_ch - 1)
        for g in range(n_grp):
            pv = jnp.dot(vT_ref[0, cc, g * dh:(g + 1) * dh, :], pb_ref[slot, g],
                         preferred_element_type=f32)
            acc_ref[g] = al_ref[slot, g] * acc_ref[g] + pv

    qk_stage(0, 0)
    pb_ref[1] = jnp.zeros(pb_ref.shape[1:], pb_ref.dtype)
    al_ref[1] = jnp.ones(al_ref.shape[1:], f32)

    def p3_body(k, carry):
        for slot in range(2):
            c = 2 * k + slot
            pv_stage(c - 1, 1 - slot)
            softmax_stage(c, slot)
            qk_stage(c + 1, 1 - slot)
        return carry

    lax.fori_loop(0, (n_ch + 2) // 2, p3_body, 0)

    for g in range(n_grp):
        o = acc_ref[g] / l_ref[g]
        for j in range(grp):
            h = g * grp + j
            o_ref[0, :, h * dh:(h + 1) * dh] = o[:, j * qb:(j + 1) * qb].T.astype(o_ref.dtype)


def _attn_prompt(qT, qiT, wiT, kb, vT4, kib, *, t_len, topk):
    bsz, dq, t_pad = qT.shape
    qb, kc, kc2 = ATTN_QB, ATTN_KC, ATTN_KC2
    dkv = kb.shape[2]
    dh = dq // N_HEADS
    s_rows = -(-t_pad // kc2) * kc2
    kern = functools.partial(_attn_prompt_kernel, qb=qb, kc=kc, kc2=kc2, topk=topk)
    return pl.pallas_call(
        kern, grid=(bsz, -(-t_len // qb)),
        in_specs=[pl.BlockSpec((1, dq, qb), lambda b, i: (b, 0, i)),
                  pl.BlockSpec((1, qiT.shape[1], qb), lambda b, i: (b, 0, i)),
                  pl.BlockSpec((1, wiT.shape[1], qb), lambda b, i: (b, 0, i)),
                  pl.BlockSpec((1, t_pad, dkv), lambda b, i: (b, 0, 0), pipeline_mode=pl.Buffered(1)),
                  pl.BlockSpec((1,) + vT4.shape[1:], lambda b, i: (b, 0, 0, 0),
                               pipeline_mode=pl.Buffered(1)),
                  pl.BlockSpec((1, t_pad, kib.shape[2]), lambda b, i: (b, 0, 0),
                               pipeline_mode=pl.Buffered(1))],
        out_specs=pl.BlockSpec((1, qb, dq), lambda b, i: (b, i, 0)),
        out_shape=jax.ShapeDtypeStruct((bsz, t_pad, dq), ACT_DTYPE),
        scratch_shapes=[pltpu.VMEM((s_rows, qb), f32),
                        pltpu.VMEM((N_KV_HEADS, dh, KV_GROUP * qb), MXU_DTYPE),
                        pltpu.VMEM((N_KV_HEADS, dh, KV_GROUP * qb), f32),
                        pltpu.VMEM((N_KV_HEADS, 1, KV_GROUP * qb), f32),
                        pltpu.VMEM((N_KV_HEADS, 1, KV_GROUP * qb), f32),
                        pltpu.VMEM((2, N_KV_HEADS, kc, KV_GROUP * qb), f32),
                        pltpu.VMEM((2, N_KV_HEADS, kc, KV_GROUP * qb), MXU_DTYPE),
                        pltpu.VMEM((2, N_KV_HEADS, 1, KV_GROUP * qb), f32)],
        compiler_params=_cparams(2), name="attn_prompt",
    )(qT, qiT, wiT, kb, vT4, kib)


def _sattn_select_kernel(pt_ref, qi_ref, wi_ref, kin_ref, *rest, pg, ns, topk, n_new):
    del pt_ref
    pages = rest[:pg]
    s_ref, t_ref, j_ref = rest[pg:]
    st = pl.program_id(1)
    pgk = pg * PAGE_SIZE
    qi = qi_ref[0]
    wi = wi_ref[0]
    n_ih = qi.shape[0] // n_new

    def scores(keys_t):
        rel = jnp.dot(qi, keys_t, preferred_element_type=f32)
        rel = jnp.maximum(rel, 0.0) * wi
        return jnp.sum(rel.reshape(n_ih, n_new, keys_t.shape[1]), axis=0)

    kp = jnp.concatenate([p[...] for p in pages], axis=1).astype(MXU_DTYPE)
    s_ref[0, st] = scores(kp)

    @pl.when(st == ns - 1)
    def _():
        n_pad = kin_ref.shape[2]
        sn = scores(kin_ref[0])
        lane = lax.broadcasted_iota(jnp.int32, (n_new, n_pad), 1)
        qrow = lax.broadcasted_iota(jnp.int32, (n_new, n_pad), 0)
        ok_new = lane <= qrow
        s_ref[0, ns] = jnp.concatenate(
            [jnp.where(ok_new, sn, NEG), jnp.full((n_new, pgk - n_pad), NEG, f32)], axis=1)

        s_all = s_ref[0]
        past = s_all[:-1]
        hi0 = jnp.maximum(jnp.max(jnp.max(past, axis=0), axis=1, keepdims=True),
                          jnp.max(jnp.where(ok_new, sn, NEG), axis=1, keepdims=True))
        lo0 = jnp.minimum(jnp.min(jnp.min(past, axis=0), axis=1, keepdims=True),
                          jnp.min(jnp.where(ok_new, sn, BIG), axis=1, keepdims=True))
        kidx = (lax.broadcasted_iota(jnp.int32, s_all.shape, 0) * pgk
                + lax.broadcasted_iota(jnp.int32, s_all.shape, 2)).astype(f32)

        def total(x):
            return jnp.sum(jnp.sum(x, axis=0), axis=1, keepdims=True)

        def count_ge(x):
            return total(jnp.where(s_all >= x[None], 1.0, 0.0))

        def count_tie(t, j):
            hit = jnp.logical_and(s_all == t[None], kidx < j[None])
            return total(jnp.where(hit, 1.0, 0.0))

        q1 = lax.broadcasted_iota(jnp.int32, (n_new, 1), 0)
        n_allowed = (q1 + (ns * pgk + 1)).astype(f32)
        t, jsel = _select_threshold(count_ge, count_tie, lo0, hi0, n_allowed, float(ns * pgk + n_pad),
                                    topk)
        t_ref[0] = jnp.broadcast_to(t, t_ref.shape[1:])
        j_ref[0] = jnp.broadcast_to(jsel, j_ref.shape[1:])


def _sattn_attend_kernel(pt_ref, q_ref, s_ref, snew_ref, t_ref, j_ref, knew_ref, vnew_ref, *rest,
                         pg, ns, n_new):
    del pt_ref
    kpages, vpages = rest[:pg], rest[pg:2 * pg]
    o_ref, m_s, l_s, acc_s = rest[2 * pg:]
    st = pl.program_id(1)
    pgk = pg * PAGE_SIZE
    q = q_ref[0]
    dh = q.shape[1]
    rows_g = KV_GROUP * n_new
    t = t_ref[0][:, 0:1]
    jsel = j_ref[0][:, 0:1]

    @pl.when(st == 0)
    def _():
        m_s[...] = jnp.full(m_s.shape, NEG, f32)
        l_s[...] = jnp.zeros(l_s.shape, f32)
        acc_s[...] = jnp.zeros(acc_s.shape, f32)

    def update(sc, base, keys_of, vals_of):
        n = sc.shape[1]
        kidx = (lax.broadcasted_iota(jnp.int32, (n_new, n), 1) + base).astype(f32)
        sel = jnp.logical_or(sc > t, jnp.logical_and(sc == t, kidx < jsel))
        bias = jnp.where(sel, 0.0, NEG)
        bias = jnp.concatenate([bias] * KV_GROUP, axis=0)
        s = jnp.concatenate(
            [lax.dot_general(q[g * rows_g:(g + 1) * rows_g, :], keys_of(g), (((1,), (1,)), ((), ())),
                             preferred_element_type=f32) + bias for g in range(N_KV_HEADS)], axis=0)
        m_old = m_s[...]
        m_new = jnp.maximum(m_old, jnp.max(s, axis=1, keepdims=True))
        alpha = jnp.exp2(m_old - m_new)
        p = jnp.exp2(s - m_new)
        l_s[...] = alpha * l_s[...] + jnp.sum(p, axis=1, keepdims=True)
        m_s[...] = m_new
        pb = p.astype(MXU_DTYPE)
        pv = jnp.concatenate(
            [jnp.dot(pb[g * rows_g:(g + 1) * rows_g, :], vals_of(g), preferred_element_type=f32)
             for g in range(N_KV_HEADS)], axis=0)
        acc_s[...] = alpha * acc_s[...] + pv

    def paged(pages):
        def head_rows(g):
            return jnp.concatenate([p[pl.ds(g, PAGE_SIZE, stride=N_KV_HEADS), :] for p in pages],
                                   axis=0).astype(MXU_DTYPE)
        return head_rows

    def fresh(ref):
        return lambda g: ref[0][:, g * dh:(g + 1) * dh]

    update(s_ref[0, 0], st * pgk, paged(kpages), paged(vpages))

    @pl.when(st == ns - 1)
    def _():
        n_pad = knew_ref.shape[1]
        update(snew_ref[0, 0][:, :n_pad], ns * pgk, fresh(knew_ref), fresh(vnew_ref))
        o_ref[0] = (acc_s[...] / l_s[...]).astype(o_ref.dtype)


def _attn_sample(layer, page_table, cache_ikt, cache_k3, cache_v3, qi_hq, wi_hq, kin_t, q_hq, knew, vnew,
                 *, topk):
    bsz, n_pages = page_table.shape
    n_new = qi_hq.shape[1] // N_IDX_HEADS
    pg = _largest_tile(n_pages, SAMPLE_PAGES_PER_STEP, 1)
    ns = n_pages // pg
    pgk = pg * PAGE_SIZE
    di = cache_ikt.shape[2]
    dh = cache_k3.shape[3]
    dkv = knew.shape[2]
    n_pad = knew.shape[1]
    rows = q_hq.shape[1]

    def page_spec(shape, j):
        return pl.BlockSpec((None, None) + shape, lambda b, s, pt: (layer, pt[b, s * pg + j], 0, 0))

    s_all, thr, jsel = pl.pallas_call(
        functools.partial(_sattn_select_kernel, pg=pg, ns=ns, topk=topk, n_new=n_new),
        grid_spec=pltpu.PrefetchScalarGridSpec(
            num_scalar_prefetch=1, grid=(bsz, ns),
            in_specs=[pl.BlockSpec((1,) + qi_hq.shape[1:], lambda b, s, pt: (b, 0, 0)),
                      pl.BlockSpec((1,) + wi_hq.shape[1:], lambda b, s, pt: (b, 0, 0)),
                      pl.BlockSpec((1, di, n_pad), lambda b, s, pt: (b, 0, 0))]
            + [page_spec((di, PAGE_SIZE), j) for j in range(pg)],
            out_specs=[pl.BlockSpec((1, ns + 1, n_new, pgk), lambda b, s, pt: (b, 0, 0, 0)),
                       pl.BlockSpec((1, n_new, LANES), lambda b, s, pt: (b, 0, 0)),
                       pl.BlockSpec((1, n_new, LANES), lambda b, s, pt: (b, 0, 0))]),
        out_shape=[jax.ShapeDtypeStruct((bsz, ns + 1, n_new, pgk), f32),
                   jax.ShapeDtypeStruct((bsz, n_new, LANES), f32),
                   jax.ShapeDtypeStruct((bsz, n_new, LANES), f32)],
        compiler_params=_cparams(2), name="sattn_select",
    )(page_table, qi_hq, wi_hq, kin_t, *([cache_ikt] * pg))

    kv_page = (PAGE_SIZE * N_KV_HEADS, dh)
    return pl.pallas_call(
        functools.partial(_sattn_attend_kernel, pg=pg, ns=ns, n_new=n_new),
        grid_spec=pltpu.PrefetchScalarGridSpec(
            num_scalar_prefetch=1, grid=(bsz, ns),
            in_specs=[pl.BlockSpec((1, rows, dh), lambda b, s, pt: (b, 0, 0)),
                      pl.BlockSpec((1, 1, n_new, pgk), lambda b, s, pt: (b, s, 0, 0)),
                      pl.BlockSpec((1, 1, n_new, pgk), lambda b, s, pt: (b, ns, 0, 0)),
                      pl.BlockSpec((1, n_new, LANES), lambda b, s, pt: (b, 0, 0)),
                      pl.BlockSpec((1, n_new, LANES), lambda b, s, pt: (b, 0, 0)),
                      pl.BlockSpec((1, n_pad, dkv), lambda b, s, pt: (b, 0, 0)),
                      pl.BlockSpec((1, n_pad, dkv), lambda b, s, pt: (b, 0, 0))]
            + [page_spec(kv_page, j) for j in range(pg)] * 2,
            out_specs=pl.BlockSpec((1, rows, dh), lambda b, s, pt: (b, 0, 0)),
            scratch_shapes=[pltpu.VMEM((rows, 1), f32), pltpu.VMEM((rows, 1), f32),
                            pltpu.VMEM((rows, dh), f32)]),
        out_shape=jax.ShapeDtypeStruct((bsz, rows, dh), ACT_DTYPE),
        compiler_params=_cparams(2), name="sattn_attend",
    )(page_table, q_hq, s_all, s_all, thr, jsel, knew, vnew, *([cache_k3] * pg), *([cache_v3] * pg))


def _merge_kernel(g_ref, o_ref, ga_ref, gb_ref, x_ref, wa_ref, wb_ref, wo_ref, lg_ref, lb_ref, out_ref,
                  *, alpha):
    ya = jnp.dot(g_ref[0].astype(MXU_DTYPE), wa_ref[...], preferred_element_type=f32)
    yb = jnp.dot(o_ref[0].astype(MXU_DTYPE), wb_ref[...], preferred_element_type=f32)
    mixed = (jax.nn.sigmoid(ga_ref[0].astype(f32)) * ya + jax.nn.sigmoid(gb_ref[0].astype(f32)) * yb)
    mix = jnp.dot(mixed.astype(MXU_DTYPE), wo_ref[...], preferred_element_type=f32)
    out_ref[0] = _layer_norm(alpha * x_ref[0] + mix, lg_ref[...], lb_ref[...])


def _merge(g, o, ga, gb, x, wa, wb, wo, lg, lb, *, tile, alpha):
    bsz, t_len, d = x.shape
    row_spec = pl.BlockSpec((1, tile, d), lambda b, t: (b, t, 0))
    return pl.pallas_call(
        functools.partial(_merge_kernel, alpha=alpha), grid=(bsz, t_len // tile),
        in_specs=[row_spec] * 5 + [_resident(a.shape) for a in (wa, wb, wo, lg, lb)],
        out_specs=row_spec, out_shape=jax.ShapeDtypeStruct((bsz, t_len, d), f32),
        compiler_params=_cparams(2), name="merge",
    )(g, o, ga, gb, x, wa, wb, wo, lg, lb)


def _ffn_kernel(x_ref, buf_ref, wup_ref, cw_ref, cb_ref, wdn_ref, lg_ref, lb_ref, out_ref, nbuf_ref,
                carry_ref, *, alpha, tstride, ck):
    @pl.when(pl.program_id(1) == 0)
    def _():
        carry_ref[...] = buf_ref[0]

    x = x_ref[0]
    xb = x.astype(MXU_DTYPE)
    d_ff = wdn_ref.shape[0]
    acc = jnp.zeros(x.shape, f32)
    for c in range(d_ff // ck):
        halves = []
        for off in (c * ck, d_ff + c * ck):
            u = jnp.dot(xb, wup_ref[:, off:off + ck], preferred_element_type=f32)
            uc, ncarry = _causal_conv(u, carry_ref[:, off:off + ck], cw_ref[:, off:off + ck],
                                      cb_ref[:, off:off + ck], tstride)
            carry_ref[:, off:off + ck] = ncarry
            halves.append(uc)
        act = (_gelu(halves[0]) * halves[1]).astype(MXU_DTYPE)
        acc = acc + jnp.dot(act, wdn_ref[c * ck:(c + 1) * ck, :], preferred_element_type=f32)
    out_ref[0] = _layer_norm(alpha * x + acc, lg_ref[...], lb_ref[...])
    nbuf_ref[0] = carry_ref[...]


def _ffn(x, buf, wup, cw, cb, wdn, lg, lb, *, tile, alpha, tstride):
    bsz, t_len, d = x.shape
    cr, f2 = buf.shape[1], buf.shape[2]
    row_spec = pl.BlockSpec((1, tile, d), lambda b, t: (b, t, 0))
    buf_spec = pl.BlockSpec((1, cr, f2), lambda b, t: (b, 0, 0))
    return pl.pallas_call(
        functools.partial(_ffn_kernel, alpha=alpha, tstride=tstride, ck=512),
        grid=(bsz, t_len // tile),
        in_specs=[row_spec, buf_spec] + [_resident(a.shape) for a in (wup, cw, cb, wdn, lg, lb)],
        out_specs=[row_spec, buf_spec],
        out_shape=[jax.ShapeDtypeStruct((bsz, t_len, d), f32),
                   jax.ShapeDtypeStruct((bsz, cr, f2), f32)],
        scratch_shapes=[pltpu.VMEM((cr, f2), f32)],
        compiler_params=_cparams(2), name="ffn",
    )(x, buf, wup, cw, cb, wdn, lg, lb)


def _layer_weights(l, w_in, rnn_conv_w, rnn_conv_b, lru_wa, lru_ba, lru_wx, lru_bx, lru_lambda,
                   w_branch_a, w_branch_b, w_out, ln1_g, ln1_b, ffn_w_up, ffn_conv_w, ffn_conv_b,
                   ffn_w_down, ln2_g, ln2_b):
    d = w_in.shape[1]
    dh = d // N_HEADS
    di = (w_in.shape[2] - 4 * d - (N_HEADS + 2 * N_KV_HEADS) * dh - N_IDX_HEADS) // (N_IDX_HEADS + 1)
    sizes = (d, d, N_HEADS * dh, N_KV_HEADS * dh, N_KV_HEADS * dh, N_IDX_HEADS * di, di, N_IDX_HEADS, d, d)
    names = ("xr", "gr", "q", "k", "v", "qi", "ki", "wi", "ga", "gb")
    cols, off = {}, 0
    for name, size in zip(names, sizes):
        cols[name] = w_in[l][:, off:off + size].astype(MXU_DTYPE)
        off += size
    row = lambda v: v[l][None, :]
    return dict(
        cols=cols, dh=dh, di=di,
        cw=rnn_conv_w[l], cb=row(rnn_conv_b),
        wax=jnp.concatenate([lru_wa[l], lru_wx[l]], axis=2).astype(MXU_DTYPE),
        ba=row(lru_ba), bx=row(lru_bx), lam=row(lru_lambda),
        wa=w_branch_a[l].astype(MXU_DTYPE), wb=w_branch_b[l].astype(MXU_DTYPE),
        wo=w_out[l].astype(MXU_DTYPE), lg1=row(ln1_g), lb1=row(ln1_b),
        wup=ffn_w_up[l].astype(MXU_DTYPE), fcw=ffn_conv_w[l], fcb=row(ffn_conv_b),
        wdn=ffn_w_down[l].astype(MXU_DTYPE), lg2=row(ln2_g), lb2=row(ln2_b))


def _prompt_layer(x, w, alpha):
    bsz, t_len, d = x.shape
    dh, di, cols = w["dh"], w["di"], w["cols"]
    kc = ATTN_KC
    t_pad = -(-t_len // kc) * kc
    ptile = kc * _largest_tile(t_pad // kc, 3, 1)
    rtile = _largest_tile(t_len, 1024, 16)
    topk = min(TOPK_MAX, t_len // 4)
    one = lambda dtype: [(dtype, False)]
    std = [(cols["xr"], 1.0, one(f32)), (cols["gr"], 1.0, one(ACT_DTYPE)),
           (cols["k"], 1.0, [(f32, False), (MXU_DTYPE, True)]), (cols["v"], 1.0, one(f32)),
           (cols["ki"], 1.0, [(f32, False), (MXU_DTYPE, True)]),
           (cols["ga"], 1.0, one(ACT_DTYPE)), (cols["gb"], 1.0, one(ACT_DTYPE))]
    tr = [(cols["q"].T, dh ** -0.5 * LOG2E, MXU_DTYPE, False), (cols["qi"].T, di ** -0.5, MXU_DTYPE, False),
          (cols["wi"].T, N_IDX_HEADS ** -0.5, f32, False), (cols["v"].T, 1.0, MXU_DTYPE, True)]
    xr, gr, k, kb, v, ki, kib, ga, gb, qT, qiT, wiT, vT4 = _project(
        x, std, tr, tile=ptile, t_pad=t_pad, kc=kc)

    c_rnn, w_rnn = xr.shape[2], w["cw"].shape[0]
    g, h_last, nbuf = _rglru(
        xr, gr, jnp.zeros((bsz, 1, c_rnn), f32), jnp.zeros((bsz, SUBLANES, c_rnn), f32),
        w["cw"], w["cb"], w["wax"], w["ba"], w["bx"], w["lam"], tile=rtile, tstride=1)
    o = _attn_prompt(qT, qiT, wiT, kb, vT4, kib, t_len=t_len, topk=topk)
    x1 = _merge(g, o, ga, gb, x, w["wa"], w["wb"], w["wo"], w["lg1"], w["lb1"], tile=rtile, alpha=alpha)
    f2, w_ffn = w["fcw"].shape[1], w["fcw"].shape[0]
    x2, fbuf = _ffn(x1, jnp.zeros((bsz, SUBLANES, f2), f32), w["wup"], w["fcw"], w["fcb"], w["wdn"],
                    w["lg2"], w["lb2"], tile=_largest_tile(t_len, 512, 16), alpha=alpha, tstride=1)
    state = (k.reshape(bsz, t_len, N_KV_HEADS, dh), v.reshape(bsz, t_len, N_KV_HEADS, dh), ki,
             h_last[:, 0, :], nbuf[:, SUBLANES - (w_rnn - 1):, :], fbuf[:, SUBLANES - (w_ffn - 1):, :])
    return x2, state


def _to_time_major(a):
    a = jnp.swapaxes(a, 0, 1)
    return a.reshape((1, a.shape[0] * a.shape[1]) + a.shape[2:])


def _from_time_major(a, bsz):
    a = a.reshape((a.shape[1] // bsz, bsz) + a.shape[2:])
    return jnp.swapaxes(a, 0, 1)


def _sample_layer(x, w, alpha, layer, page_table, cache_ikt, cache_k3, cache_v3, h0, lru_buf, ffn_buf, bsz):
    rows, d = x.shape[1], x.shape[2]
    n_new = rows // bsz
    dh, di, cols = w["dh"], w["di"], w["cols"]
    past = page_table.shape[1] * PAGE_SIZE
    topk = min(TOPK_MAX, (past + n_new) // 4)
    one = lambda dtype: [(dtype, False)]
    std = [(cols["xr"], 1.0, one(f32)), (cols["gr"], 1.0, one(ACT_DTYPE)),
           (cols["q"], dh ** -0.5 * LOG2E, one(MXU_DTYPE)), (cols["k"], 1.0, one(f32)),
           (cols["v"], 1.0, one(f32)), (cols["qi"], di ** -0.5, one(MXU_DTYPE)),
           (cols["ki"], 1.0, one(f32)), (cols["wi"], N_IDX_HEADS ** -0.5, one(f32)),
           (cols["ga"], 1.0, one(ACT_DTYPE)), (cols["gb"], 1.0, one(ACT_DTYPE))]
    xr, gr, q, k, v, qi, ki, wi, ga, gb = _project(x, std, [], tile=rows, t_pad=rows, kc=ATTN_KC)

    g, h_last, nbuf = _rglru(xr, gr, h0[None], _to_time_major(lru_buf), w["cw"], w["cb"], w["wax"],
                             w["ba"], w["bx"], w["lam"], tile=rows, tstride=bsz)

    def heads_major(a, n_h):
        a = _from_time_major(a, bsz).reshape(bsz, n_new, n_h, -1)
        return jnp.swapaxes(a, 1, 2).reshape(bsz, n_h * n_new, -1)

    def pad_new(a):
        a = _from_time_major(a, bsz).astype(MXU_DTYPE)
        return jnp.pad(a, ((0, 0), (0, LANES - n_new), (0, 0)))

    kin_t = jnp.swapaxes(pad_new(ki), 1, 2)
    o = _attn_sample(layer, page_table, cache_ikt, cache_k3, cache_v3,
                     heads_major(qi, N_IDX_HEADS), heads_major(wi, N_IDX_HEADS), kin_t,
                     heads_major(q, N_HEADS), pad_new(k), pad_new(v), topk=topk)
    o = jnp.swapaxes(o.reshape(bsz, N_HEADS, n_new, dh), 1, 2).reshape(bsz, n_new, N_HEADS * dh)
    o = _to_time_major(o)

    x1 = _merge(g, o, ga, gb, x, w["wa"], w["wb"], w["wo"], w["lg1"], w["lb1"], tile=rows, alpha=alpha)
    x2, fbuf = _ffn(x1, _to_time_major(ffn_buf), w["wup"], w["fcw"], w["fcb"], w["wdn"], w["lg2"],
                    w["lb2"], tile=rows, alpha=alpha, tstride=bsz)
    state = (_from_time_major(k, bsz).reshape(bsz, n_new, N_KV_HEADS, dh),
             _from_time_major(v, bsz).reshape(bsz, n_new, N_KV_HEADS, dh),
             _from_time_major(ki, bsz), h_last[0], _from_time_major(nbuf, bsz), _from_time_major(fbuf, bsz))
    return x2, state


def kernel(x_prompt, x_sample, cache_k, cache_v, cache_idx_k, state_lru_h, state_lru_conv, state_ffn_conv,
           page_table, meta_tokens, w_in, rnn_conv_w, rnn_conv_b, lru_wa, lru_ba, lru_wx, lru_bx, lru_lambda,
           w_branch_a, w_branch_b, w_out, ln1_g, ln1_b, ffn_w_up, ffn_conv_w, ffn_conv_b, ffn_w_down,
           ln2_g, ln2_b):
    depth = w_in.shape[0]
    alpha = (2.0 * depth) ** 0.25
    bsz, _, d = x_prompt.shape
    dbsz = x_sample.shape[0]
    meta = jnp.broadcast_to(meta_tokens.astype(x_prompt.dtype), (bsz, N_META, d))
    xp = jnp.concatenate([meta, x_prompt], axis=1)
    xs = _to_time_major(x_sample)
    cache_k3 = cache_k.reshape(cache_k.shape[:2] + (-1, cache_k.shape[4]))
    cache_v3 = cache_v.reshape(cache_v.shape[:2] + (-1, cache_v.shape[4]))
    cache_ikt = jnp.swapaxes(cache_idx_k, 2, 3)
    p_states, s_states = [], []
    for l in range(depth):
        w = _layer_weights(l, w_in, rnn_conv_w, rnn_conv_b, lru_wa, lru_ba, lru_wx, lru_bx, lru_lambda,
                           w_branch_a, w_branch_b, w_out, ln1_g, ln1_b, ffn_w_up, ffn_conv_w, ffn_conv_b,
                           ffn_w_down, ln2_g, ln2_b)
        xp, st = _prompt_layer(xp, w, alpha)
        p_states.append(st)
        xs, st = _sample_layer(xs, w, alpha, l, page_table, cache_ikt, cache_k3, cache_v3,
                               state_lru_h[l], state_lru_conv[l], state_ffn_conv[l], dbsz)
        s_states.append(st)
    stack = lambda states, n: jnp.stack([st[n] for st in states])
    return ((xp[:, N_META:], _from_time_major(xs, dbsz))
            + tuple(stack(p_states, n) for n in range(6))
            + tuple(stack(s_states, n) for n in range(6)))
```

```python
import functools

import jax
import jax.numpy as jnp
from jax import lax
from jax.experimental import pallas as pl
from jax.experimental.pallas import tpu as pltpu

f32 = jnp.float32

N_META = 16
N_RNN_BLOCKS = 8
LRU_C = 8.0
N_HEADS = 8
N_KV_HEADS = 4
KV_GROUP = N_HEADS // N_KV_HEADS
N_IDX_HEADS = 8
TOPK_MAX = 256
PAGE_SIZE = 128
LN_EPS = 1e-5

NEG = -1e30
BIG = 1e30
NO_TIE_LIMIT = 1e9
LOG2E = 1.4426950408889634
F32_TINY = float(jnp.finfo(jnp.float32).tiny)

MXU_DTYPE = jnp.bfloat16
ACT_DTYPE = jnp.bfloat16

V7X_VMEM_BYTES = 64 * 1024 * 1024
VMEM_LIMIT = V7X_VMEM_BYTES * 7 // 8
LANES = 128
SUBLANES = 8

ATTN_QB = 256
ATTN_KC = 256
ATTN_KC2 = 1024
COUNT_CHAINS = 8
COUNT_ROWS = COUNT_CHAINS * SUBLANES
MAX_BISECT = 4096
MAX_TIE_BISECT = 64
SAMPLE_PAGES_PER_STEP = 16


def _cparams(n_grid):
    return pltpu.CompilerParams(
        dimension_semantics=("arbitrary",) * n_grid, vmem_limit_bytes=VMEM_LIMIT)


def _resident(shape):
    nd = len(shape)
    return pl.BlockSpec(tuple(shape), lambda *_: (0,) * nd, pipeline_mode=pl.Buffered(1))


def _largest_tile(n, limit, multiple):
    best = None
    for d in range(multiple, min(n, limit) + 1, multiple):
        if n % d == 0:
            best = d
    assert best is not None, (n, limit, multiple)
    return best


def _gelu(x):
    return jax.nn.gelu(x, approximate=True)


def _softplus(x):
    return jnp.maximum(x, 0.0) + jnp.log1p(jnp.exp(-jnp.abs(x)))


def _layer_norm(y, g, b):
    mu = jnp.mean(y, axis=-1, keepdims=True)
    d = y - mu
    var = jnp.mean(d * d, axis=-1, keepdims=True)
    return d * lax.rsqrt(var + LN_EPS) * g + b


def _causal_conv(u, carry, w, b, tstride):
    width = w.shape[0]
    rows = u.shape[0]
    y = b + w[width - 1:width, :] * u
    if tstride % SUBLANES == 0:
        ext = jnp.concatenate([carry, u], axis=0)
        for j in range(width - 1):
            y = y + w[j:j + 1, :] * ext[j * tstride:j * tstride + rows, :]
        return y, ext[rows:, :]
    assert tstride == 1 and carry.shape[0] == SUBLANES
    row = lax.broadcasted_iota(jnp.int32, (SUBLANES, u.shape[1]), 0)
    for s in range(1, width):
        rolled = pltpu.roll(u, s, axis=0)
        head = jnp.where(row < s, pltpu.roll(carry, s, axis=0), rolled[0:SUBLANES, :])
        shifted = jnp.concatenate([head, rolled[SUBLANES:, :]], axis=0)
        y = y + w[width - 1 - s:width - s, :] * shifted
    return y, u[rows - SUBLANES:, :]


def _proj_kernel(x_ref, *refs, n_outs, n_tr, scales, t_valid, tile, kc):
    n_w = len(n_outs) + n_tr
    w_refs = refs[:n_w]
    o_refs = list(refs[n_w:])
    t0 = pl.program_id(1) * tile
    row = lax.broadcasted_iota(jnp.int32, (tile, 1), 0) + t0
    xb = jnp.where(row < t_valid, x_ref[0], 0.0).astype(MXU_DTYPE)
    for n, n_out in enumerate(n_outs):
        y = jnp.dot(xb, w_refs[n][...], preferred_element_type=f32)
        if scales[n] != 1.0:
            y = y * scales[n]
        for _ in range(n_out):
            o = o_refs.pop(0)
            o[0] = y.astype(o.dtype)
    for n in range(len(n_outs), n_w):
        y = lax.dot_general(w_refs[n][...], xb, (((1,), (1,)), ((), ())),
                            preferred_element_type=f32)
        if scales[n] != 1.0:
            y = y * scales[n]
        o = o_refs.pop(0)
        if len(o.shape) == 4:
            for cc in range(tile // kc):
                o[0, cc] = y[:, cc * kc:(cc + 1) * kc].astype(o.dtype)
        else:
            o[0] = y.astype(o.dtype)


def _project(x, std, tr, *, tile, t_pad, kc):
    bsz, t_len, d = x.shape
    n_t = t_pad // tile
    in_specs = [pl.BlockSpec((1, tile, d), lambda b, t: (b, t, 0))]
    out_shapes, out_specs, scales = [], [], []
    for w, scale, outs in std:
        in_specs.append(_resident(w.shape))
        n = w.shape[1]
        for dtype, padded in outs:
            out_shapes.append(jax.ShapeDtypeStruct((bsz, t_pad if padded else t_len, n), dtype))
            out_specs.append(pl.BlockSpec((1, tile, n), lambda b, t: (b, t, 0)))
        scales.append(scale)
    for w, scale, dtype, chunked in tr:
        in_specs.append(_resident(w.shape))
        n = w.shape[0]
        if chunked:
            out_shapes.append(jax.ShapeDtypeStruct((bsz, t_pad // kc, n, kc), dtype))
            out_specs.append(pl.BlockSpec((1, tile // kc, n, kc), lambda b, t: (b, t, 0, 0)))
        else:
            out_shapes.append(jax.ShapeDtypeStruct((bsz, n, t_pad), dtype))
            out_specs.append(pl.BlockSpec((1, n, tile), lambda b, t: (b, 0, t)))
        scales.append(scale)
    kern = functools.partial(_proj_kernel, n_outs=tuple(len(outs) for _, _, outs in std), n_tr=len(tr),
                             scales=tuple(scales), t_valid=t_len, tile=tile, kc=kc)
    return pl.pallas_call(
        kern, grid=(bsz, n_t), in_specs=in_specs, out_specs=out_specs, out_shape=out_shapes,
        compiler_params=_cparams(2), name="proj",
    )(x, *[w for w, *_ in std], *[w for w, *_ in tr])


def _lru_gates(xc, wax_ref, ba, bx, lam):
    blk = wax_ref.shape[1]
    xcb = xc.astype(MXU_DTYPE)
    rs, gs = [], []
    for n in range(wax_ref.shape[0]):
        y = jnp.dot(xcb[:, n * blk:(n + 1) * blk], wax_ref[n], preferred_element_type=f32)
        rs.append(y[:, :blk])
        gs.append(y[:, blk:])
    r = jax.nn.sigmoid(jnp.concatenate(rs, axis=1) + ba)
    i = jax.nn.sigmoid(jnp.concatenate(gs, axis=1) + bx)
    log_a = (-LRU_C) * r * _softplus(-lam)
    a = jnp.exp(log_a)
    th = jnp.tanh(log_a)
    one_minus_a2 = -2.0 * th / (1.0 - th)
    return a, jnp.sqrt(one_minus_a2) * (i * xc)


def _scan8(a8, b8, h_prev):
    row = lax.broadcasted_iota(jnp.int32, a8.shape, 0)
    a, b = a8, b8
    for s in (1, 2, 4):
        ar = pltpu.roll(a, s, axis=0)
        br = pltpu.roll(b, s, axis=0)
        m = row >= s
        b = jnp.where(m, a * br + b, b)
        a = jnp.where(m, a * ar, a)
    return a * h_prev + b


def _rglru_kernel(xr_ref, gr_ref, h0_ref, buf_ref, cw_ref, cb_ref, wax_ref, ba_ref, bx_ref, lam_ref,
                  g_ref, hl_ref, nbuf_ref, a_s, b_s, hc_s, cc_s, *, tstride):
    @pl.when(pl.program_id(1) == 0)
    def _():
        cc_s[...] = buf_ref[0]
        hc_s[...] = jnp.broadcast_to(h0_ref[0], hc_s.shape)

    xr = xr_ref[0]
    rows = xr.shape[0]
    xc, ncarry = _causal_conv(xr, cc_s[...], cw_ref[...], cb_ref[...], tstride)
    cc_s[...] = ncarry
    nbuf_ref[0] = ncarry
    a, inp = _lru_gates(xc, wax_ref, ba_ref[...], bx_ref[...], lam_ref[...])
    if tstride == 1:
        a_s[...] = a
        b_s[...] = inp

        def body(k, h_prev):
            r0 = pl.multiple_of(k * SUBLANES, SUBLANES)
            h = _scan8(a_s[pl.ds(r0, SUBLANES), :], b_s[pl.ds(r0, SUBLANES), :], h_prev)
            b_s[pl.ds(r0, SUBLANES), :] = h
            return jnp.broadcast_to(h[SUBLANES - 1:SUBLANES, :], h.shape)

        h_last = lax.fori_loop(0, rows // SUBLANES, body, hc_s[...])
        hc_s[...] = h_last
        hl_ref[0] = h_last[0:1, :]
        hs = b_s[...]
    else:
        h = hc_s[...]
        pieces = []
        for s in range(rows // tstride):
            h = a[s * tstride:(s + 1) * tstride, :] * h + inp[s * tstride:(s + 1) * tstride, :]
            pieces.append(h)
        hc_s[...] = h
        hl_ref[0] = h
        hs = jnp.concatenate(pieces, axis=0)
    g_ref[0] = (hs * _gelu(gr_ref[0].astype(f32))).astype(g_ref.dtype)


def _rglru(xr, gr, h0, buf, cw, cb, wax, ba, bx, lam, *, tile, tstride):
    bsz, t_len, c = xr.shape
    hr, cr = h0.shape[1], buf.shape[1]
    hrows = SUBLANES if tstride == 1 else hr
    row_spec = pl.BlockSpec((1, tile, c), lambda b, t: (b, t, 0))
    scratch = [pltpu.VMEM((tile, c), f32), pltpu.VMEM((tile, c), f32),
               pltpu.VMEM((hrows, c), f32), pltpu.VMEM((cr, c), f32)]
    return pl.pallas_call(
        functools.partial(_rglru_kernel, tstride=tstride),
        grid=(bsz, t_len // tile),
        in_specs=[row_spec, row_spec,
                  pl.BlockSpec((1, hr, c), lambda b, t: (b, 0, 0)),
                  pl.BlockSpec((1, cr, c), lambda b, t: (b, 0, 0)),
                  _resident(cw.shape), _resident(cb.shape), _resident(wax.shape),
                  _resident(ba.shape), _resident(bx.shape), _resident(lam.shape)],
        out_specs=[row_spec,
                   pl.BlockSpec((1, hr, c), lambda b, t: (b, 0, 0)),
                   pl.BlockSpec((1, cr, c), lambda b, t: (b, 0, 0))],
        out_shape=[jax.ShapeDtypeStruct((bsz, t_len, c), ACT_DTYPE),
                   jax.ShapeDtypeStruct((bsz, hr, c), f32),
                   jax.ShapeDtypeStruct((bsz, cr, c), f32)],
        scratch_shapes=scratch, compiler_params=_cparams(2), name="rglru",
    )(xr, gr, h0, buf, cw, cb, wax, ba, bx, lam)


def _select_threshold(count_ge, count_tie, lo0, hi0, n_allowed, n_keys, topk):
    kf = float(topk)

    def n_active(done):
        return jnp.sum(jnp.where(done, 0.0, 1.0))

    c0 = count_ge(hi0)
    take = jnp.logical_and(n_allowed > kf, c0 >= kf)
    lo = jnp.where(take, hi0, lo0)
    cnt_lo = jnp.where(take, c0, n_allowed)
    cnt_hi = jnp.where(take, 0.0, c0)
    stalled = jnp.zeros_like(lo0)

    def bis_cond(st):
        return jnp.logical_and(st[-1] > 0.0, st[-2] < MAX_BISECT)

    def bis_body(st):
        lo, hi, cnt_lo, cnt_hi, stalled, it, _ = st
        live = jnp.logical_and(cnt_lo > kf, stalled <= 0.0)
        n_live = jnp.sum(jnp.where(live, 1.0, 0.0))
        mid = lo + (hi - lo) * 0.5
        mid = jnp.where(jnp.logical_and(lo < 0.0, hi > 0.0), 0.0, mid)
        mid = jnp.where(jnp.logical_and(lo == 0.0, hi > F32_TINY), F32_TINY, mid)
        mid = jnp.where(jnp.logical_and(hi == 0.0, lo < -F32_TINY), -F32_TINY, mid)
        c = count_ge(mid)
        up = jnp.logical_and(live, c >= kf)
        dn = jnp.logical_and(live, c < kf)
        stall_now = jnp.logical_and(live, jnp.logical_or(mid <= lo, mid >= hi))
        lo = jnp.where(up, mid, lo)
        cnt_lo = jnp.where(up, c, cnt_lo)
        hi = jnp.where(dn, mid, hi)
        cnt_hi = jnp.where(dn, c, cnt_hi)
        stalled = jnp.where(stall_now, 1.0, stalled)
        return lo, hi, cnt_lo, cnt_hi, stalled, it + 1, n_live

    st = (lo, hi0, cnt_lo, cnt_hi, stalled, jnp.int32(0), n_active(cnt_lo <= kf))
    lo, _, cnt_lo, cnt_hi, _, _, _ = lax.while_loop(bis_cond, bis_body, st)
    t = lo

    need = cnt_lo > kf
    want = kf - cnt_hi

    def tie_cond(st):
        return jnp.logical_and(st[-1] > 0.0, st[-2] < MAX_TIE_BISECT)

    def tie_body(st):
        jl, jh, jf, found, it, _ = st
        live = found <= 0.0
        jm = jnp.floor((jl + jh) * 0.5)
        c = count_tie(t, jm)
        hit = jnp.logical_and(live, c == want)
        jf = jnp.where(hit, jm, jf)
        found = jnp.where(hit, 1.0, found)
        jl = jnp.where(jnp.logical_and(live, c < want), jm, jl)
        jh = jnp.where(jnp.logical_and(live, c > want), jm, jh)
        return jl, jh, jf, found, it + 1, n_active(found > 0.0)

    found0 = jnp.where(need, 0.0, 1.0)
    st = (jnp.zeros_like(t), jnp.zeros_like(t) + n_keys, jnp.full_like(t, NO_TIE_LIMIT), found0,
          jnp.int32(0), n_active(found0 > 0.0))
    _, _, jf, _, _, _ = lax.while_loop(tie_cond, tie_body, st)
    return t, jnp.where(need, jf, NO_TIE_LIMIT)


def _attn_prompt_kernel(qT_ref, qiT_ref, wiT_ref, kb_ref, vT_ref, kib_ref, o_ref,
                        s_ref, q2_ref, acc_ref, m_ref, l_ref, sa_ref, pb_ref, al_ref,
                        *, qb, kc, kc2, topk):
    i = pl.program_id(1)
    n_ih = wiT_ref.shape[1]
    di = qiT_ref.shape[1] // n_ih
    n_grp, dh = q2_ref.shape[0], q2_ref.shape[1]
    grp = q2_ref.shape[2] // qb
    c_last = (i * qb) // kc
    n_ch = c_last + 1
    n_ch2 = (n_ch * kc + kc2 - 1) // kc2

    qi_all = jnp.concatenate([qiT_ref[0, h * di:(h + 1) * di, :] for h in range(n_ih)], axis=1)
    wi = wiT_ref[0]
    kio = lax.broadcasted_iota(jnp.int32, (kc, qb), 0)
    qio = lax.broadcasted_iota(jnp.int32, (kc, qb), 1)

    def chunk_scores(c):
        k0 = pl.multiple_of(c * kc, kc)
        rel = jnp.dot(kib_ref[0, pl.ds(k0, kc), :], qi_all, preferred_element_type=f32)
        sc = wi[0:1, :] * jnp.maximum(rel[:, 0:qb], 0.0)
        for h in range(1, n_ih):
            sc = sc + wi[h:h + 1, :] * jnp.maximum(rel[:, h * qb:(h + 1) * qb], 0.0)
        return k0, sc

    def fold(x, op):
        return op(x.reshape(kc // SUBLANES, SUBLANES, qb), axis=0)

    def p1_body(k, carry):
        vmax, vmin = carry
        for c in (2 * k, jnp.minimum(2 * k + 1, c_last - 1)):
            k0, sc = chunk_scores(c)
            s_ref[pl.ds(k0, kc), :] = sc
            vmax = jnp.maximum(vmax, fold(sc, jnp.max))
            vmin = jnp.minimum(vmin, fold(sc, jnp.min))
        return vmax, vmin

    vmax, vmin = lax.fori_loop(
        0, (c_last + 1) // 2, p1_body,
        (jnp.full((SUBLANES, qb), NEG, f32), jnp.full((SUBLANES, qb), BIG, f32)))
    k0, sc = chunk_scores(c_last)
    allowed = (kio + k0) <= (qio + i * qb)
    s_ref[pl.ds(k0, kc), :] = jnp.where(allowed, sc, NEG)
    vmax = jnp.maximum(vmax, fold(jnp.where(allowed, sc, NEG), jnp.max))
    vmin = jnp.minimum(vmin, fold(jnp.where(allowed, sc, BIG), jnp.min))

    def fill_body(c, carry):
        s_ref[pl.ds(pl.multiple_of(c * kc, kc), kc), :] = jnp.full((kc, qb), NEG, f32)
        return carry

    lax.fori_loop(n_ch, n_ch2 * (kc2 // kc), fill_body, 0)

    def count_slabs(hit_fn):
        def body(c, acc):
            r0 = pl.multiple_of(c * kc2, kc2)
            for s in range(kc2 // COUNT_ROWS):
                r = pl.multiple_of(r0 + s * COUNT_ROWS, COUNT_ROWS)
                acc = jnp.where(hit_fn(s_ref[pl.ds(r, COUNT_ROWS), :], r), acc + 1.0, acc)
            return acc
        acc = lax.fori_loop(0, n_ch2, body, jnp.zeros((COUNT_ROWS, qb), f32))
        return jnp.sum(acc, axis=0, keepdims=True)

    def count_ge(x):
        xb = jnp.broadcast_to(x, (COUNT_ROWS, qb))
        return count_slabs(lambda blk, r: blk >= xb)

    def count_tie(t, j):
        tb = jnp.broadcast_to(t, (COUNT_ROWS, qb))
        row = lax.broadcasted_iota(jnp.int32, (COUNT_ROWS, qb), 0).astype(f32)

        def hit(blk, r):
            return jnp.logical_and(blk == tb, row < j - r.astype(f32))
        return count_slabs(hit)

    n_allowed = (lax.broadcasted_iota(jnp.int32, (1, qb), 1) + (i * qb + 1)).astype(f32)
    t, jsel = _select_threshold(
        count_ge, count_tie,
        jnp.min(vmin, axis=0, keepdims=True), jnp.max(vmax, axis=0, keepdims=True),
        n_allowed, (n_ch * kc).astype(f32), topk)

    for g in range(n_grp):
        q2_ref[g] = jnp.concatenate(
            [qT_ref[0, (g * grp + j) * dh:(g * grp + j + 1) * dh, :] for j in range(grp)], axis=1)
    m_ref[...] = jnp.full(m_ref.shape, NEG, f32)
    l_ref[...] = jnp.zeros(l_ref.shape, f32)
    acc_ref[...] = jnp.zeros(acc_ref.shape, f32)

    def qk_stage(c, slot):
        k0 = pl.multiple_of(jnp.minimum(c, n_ch - 1) * kc, kc)
        for g in range(n_grp):
            sa_ref[slot, g] = jnp.dot(kb_ref[0, pl.ds(k0, kc), g * dh:(g + 1) * dh], q2_ref[g],
                                      preferred_element_type=f32)

    def softmax_stage(c, slot):
        valid = c < n_ch
        k0 = pl.multiple_of(jnp.minimum(c, n_ch - 1) * kc, kc)
        blk = s_ref[pl.ds(k0, kc), :]
        kidx = (kio + k0).astype(f32)
        t_c = jnp.where(valid, t, BIG)
        j_c = jnp.where(valid, jsel, -1.0)
        sel = jnp.logical_or(blk > t_c, jnp.logical_and(blk == t_c, kidx < j_c))
        bias = jnp.where(sel, 0.0, NEG)
        bias = jnp.concatenate([bias] * grp, axis=1)
        for g in range(n_grp):
            st = sa_ref[slot, g] + bias
            m_old = m_ref[g]
            m_new = jnp.maximum(m_old, jnp.max(st, axis=0, keepdims=True))
            alpha = jnp.exp2(m_old - m_new)
            p = jnp.exp2(st - m_new)
            l_ref[g] = alpha * l_ref[g] + jnp.sum(p, axis=0, keepdims=True)
            m_ref[g] = m_new
            pb_ref[slot, g] = p.astype(pb_ref.dtype)
            al_ref[slot, g] = alpha

    def pv_stage(c, slot):
        cc = jnp.clip(c, 0, n_ch - 1)
        for g in range(n_grp):
            pv = jnp.dot(vT_ref[0, cc, g * dh:(g + 1) * dh, :], pb_ref[slot, g],
                         preferred_element_type=f32)
            acc_ref[g] = al_ref[slot, g] * acc_ref[g] + pv

    qk_stage(0, 0)
    pb_ref[1] = jnp.zeros(pb_ref.shape[1:], pb_ref.dtype)
    al_ref[1] = jnp.ones(al_ref.shape[1:], f32)

    def p3_body(k, carry):
        for slot in range(2):
            c = 2 * k + slot
            pv_stage(c - 1, 1 - slot)
            softmax_stage(c, slot)
            qk_stage(c + 1, 1 - slot)
        return carry

    lax.fori_loop(0, (n_ch + 2) // 2, p3_body, 0)

    for g in range(n_grp):
        o = acc_ref[g] / l_ref[g]
        for j in range(grp):
            h = g * grp + j
            o_ref[0, :, h * dh:(h + 1) * dh] = o[:, j * qb:(j + 1) * qb].T.astype(o_ref.dtype)


def _attn_prompt(qT, qiT, wiT, kb, vT4, kib, *, t_len, topk):
    bsz, dq, t_pad = qT.shape
    qb, kc, kc2 = ATTN_QB, ATTN_KC, ATTN_KC2
    dkv = kb.shape[2]
    dh = dq // N_HEADS
    s_rows = -(-t_pad // kc2) * kc2
    kern = functools.partial(_attn_prompt_kernel, qb=qb, kc=kc, kc2=kc2, topk=topk)
    return pl.pallas_call(
        kern, grid=(bsz, -(-t_len // qb)),
        in_specs=[pl.BlockSpec((1, dq, qb), lambda b, i: (b, 0, i)),
                  pl.BlockSpec((1, qiT.shape[1], qb), lambda b, i: (b, 0, i)),
                  pl.BlockSpec((1, wiT.shape[1], qb), lambda b, i: (b, 0, i)),
                  pl.BlockSpec((1, t_pad, dkv), lambda b, i: (b, 0, 0), pipeline_mode=pl.Buffered(1)),
                  pl.BlockSpec((1,) + vT4.shape[1:], lambda b, i: (b, 0, 0, 0),
                               pipeline_mode=pl.Buffered(1)),
                  pl.BlockSpec((1, t_pad, kib.shape[2]), lambda b, i: (b, 0, 0),
                               pipeline_mode=pl.Buffered(1))],
        out_specs=pl.BlockSpec((1, qb, dq), lambda b, i: (b, i, 0)),
        out_shape=jax.ShapeDtypeStruct((bsz, t_pad, dq), ACT_DTYPE),
        scratch_shapes=[pltpu.VMEM((s_rows, qb), f32),
                        pltpu.VMEM((N_KV_HEADS, dh, KV_GROUP * qb), MXU_DTYPE),
                        pltpu.VMEM((N_KV_HEADS, dh, KV_GROUP * qb), f32),
                        pltpu.VMEM((N_KV_HEADS, 1, KV_GROUP * qb), f32),
                        pltpu.VMEM((N_KV_HEADS, 1, KV_GROUP * qb), f32),
                        pltpu.VMEM((2, N_KV_HEADS, kc, KV_GROUP * qb), f32),
                        pltpu.VMEM((2, N_KV_HEADS, kc, KV_GROUP * qb), MXU_DTYPE),
                        pltpu.VMEM((2, N_KV_HEADS, 1, KV_GROUP * qb), f32)],
        compiler_params=_cparams(2), name="attn_prompt",
    )(qT, qiT, wiT, kb, vT4, kib)


def _sattn_select_kernel(pt_ref, qi_ref, wi_ref, kin_ref, *rest, pg, ns, topk, n_new):
    del pt_ref
    pages = rest[:pg]
    s_ref, t_ref, j_ref = rest[pg:]
    st = pl.program_id(1)
    pgk = pg * PAGE_SIZE
    qi = qi_ref[0]
    wi = wi_ref[0]
    n_ih = qi.shape[0] // n_new

    def scores(keys_t):
        rel = jnp.dot(qi, keys_t, preferred_element_type=f32)
        rel = jnp.maximum(rel, 0.0) * wi
        return jnp.sum(rel.reshape(n_ih, n_new, keys_t.shape[1]), axis=0)

    kp = jnp.concatenate([p[...] for p in pages], axis=1).astype(MXU_DTYPE)
    s_ref[0, st] = scores(kp)

    @pl.when(st == ns - 1)
    def _():
        n_pad = kin_ref.shape[2]
        sn = scores(kin_ref[0])
        lane = lax.broadcasted_iota(jnp.int32, (n_new, n_pad), 1)
        qrow = lax.broadcasted_iota(jnp.int32, (n_new, n_pad), 0)
        ok_new = lane <= qrow
        s_ref[0, ns] = jnp.concatenate(
            [jnp.where(ok_new, sn, NEG), jnp.full((n_new, pgk - n_pad), NEG, f32)], axis=1)

        s_all = s_ref[0]
        past = s_all[:-1]
        hi0 = jnp.maximum(jnp.max(jnp.max(past, axis=0), axis=1, keepdims=True),
                          jnp.max(jnp.where(ok_new, sn, NEG), axis=1, keepdims=True))
        lo0 = jnp.minimum(jnp.min(jnp.min(past, axis=0), axis=1, keepdims=True),
                          jnp.min(jnp.where(ok_new, sn, BIG), axis=1, keepdims=True))
        kidx = (lax.broadcasted_iota(jnp.int32, s_all.shape, 0) * pgk
                + lax.broadcasted_iota(jnp.int32, s_all.shape, 2)).astype(f32)

        def total(x):
            return jnp.sum(jnp.sum(x, axis=0), axis=1, keepdims=True)

        def count_ge(x):
            return total(jnp.where(s_all >= x[None], 1.0, 0.0))

        def count_tie(t, j):
            hit = jnp.logical_and(s_all == t[None], kidx < j[None])
            return total(jnp.where(hit, 1.0, 0.0))

        q1 = lax.broadcasted_iota(jnp.int32, (n_new, 1), 0)
        n_allowed = (q1 + (ns * pgk + 1)).astype(f32)
        t, jsel = _select_threshold(count_ge, count_tie, lo0, hi0, n_allowed, float(ns * pgk + n_pad),
                                    topk)
        t_ref[0] = jnp.broadcast_to(t, t_ref.shape[1:])
        j_ref[0] = jnp.broadcast_to(jsel, j_ref.shape[1:])


def _sattn_attend_kernel(pt_ref, q_ref, s_ref, snew_ref, t_ref, j_ref, knew_ref, vnew_ref, *rest,
                         pg, ns, n_new):
    del pt_ref
    kpages, vpages = rest[:pg], rest[pg:2 * pg]
    o_ref, m_s, l_s, acc_s = rest[2 * pg:]
    st = pl.program_id(1)
    pgk = pg * PAGE_SIZE
    q = q_ref[0]
    dh = q.shape[1]
    rows_g = KV_GROUP * n_new
    t = t_ref[0][:, 0:1]
    jsel = j_ref[0][:, 0:1]

    @pl.when(st == 0)
    def _():
        m_s[...] = jnp.full(m_s.shape, NEG, f32)
        l_s[...] = jnp.zeros(l_s.shape, f32)
        acc_s[...] = jnp.zeros(acc_s.shape, f32)

    def update(sc, base, keys_of, vals_of):
        n = sc.shape[1]
        kidx = (lax.broadcasted_iota(jnp.int32, (n_new, n), 1) + base).astype(f32)
        sel = jnp.logical_or(sc > t, jnp.logical_and(sc == t, kidx < jsel))
        bias = jnp.where(sel, 0.0, NEG)
        bias = jnp.concatenate([bias] * KV_GROUP, axis=0)
        s = jnp.concatenate(
            [lax.dot_general(q[g * rows_g:(g + 1) * rows_g, :], keys_of(g), (((1,), (1,)), ((), ())),
                             preferred_element_type=f32) + bias for g in range(N_KV_HEADS)], axis=0)
        m_old = m_s[...]
        m_new = jnp.maximum(m_old, jnp.max(s, axis=1, keepdims=True))
        alpha = jnp.exp2(m_old - m_new)
        p = jnp.exp2(s - m_new)
        l_s[...] = alpha * l_s[...] + jnp.sum(p, axis=1, keepdims=True)
        m_s[...] = m_new
        pb = p.astype(MXU_DTYPE)
        pv = jnp.concatenate(
            [jnp.dot(pb[g * rows_g:(g + 1) * rows_g, :], vals_of(g), preferred_element_type=f32)
             for g in range(N_KV_HEADS)], axis=0)
        acc_s[...] = alpha * acc_s[...] + pv

    def paged(pages):
        def head_rows(g):
            return jnp.concatenate([p[pl.ds(g, PAGE_SIZE, stride=N_KV_HEADS), :] for p in pages],
                                   axis=0).astype(MXU_DTYPE)
        return head_rows

    def fresh(ref):
        return lambda g: ref[0][:, g * dh:(g + 1) * dh]

    update(s_ref[0, 0], st * pgk, paged(kpages), paged(vpages))

    @pl.when(st == ns - 1)
    def _():
        n_pad = knew_ref.shape[1]
        update(snew_ref[0, 0][:, :n_pad], ns * pgk, fresh(knew_ref), fresh(vnew_ref))
        o_ref[0] = (acc_s[...] / l_s[...]).astype(o_ref.dtype)


def _attn_sample(layer, page_table, cache_ikt, cache_k3, cache_v3, qi_hq, wi_hq, kin_t, q_hq, knew, vnew,
                 *, topk):
    bsz, n_pages = page_table.shape
    n_new = qi_hq.shape[1] // N_IDX_HEADS
    pg = _largest_tile(n_pages, SAMPLE_PAGES_PER_STEP, 1)
    ns = n_pages // pg
    pgk = pg * PAGE_SIZE
    di = cache_ikt.shape[2]
    dh = cache_k3.shape[3]
    dkv = knew.shape[2]
    n_pad = knew.shape[1]
    rows = q_hq.shape[1]

    def page_spec(shape, j):
        return pl.BlockSpec((None, None) + shape, lambda b, s, pt: (layer, pt[b, s * pg + j], 0, 0))

    s_all, thr, jsel = pl.pallas_call(
        functools.partial(_sattn_select_kernel, pg=pg, ns=ns, topk=topk, n_new=n_new),
        grid_spec=pltpu.PrefetchScalarGridSpec(
            num_scalar_prefetch=1, grid=(bsz, ns),
            in_specs=[pl.BlockSpec((1,) + qi_hq.shape[1:], lambda b, s, pt: (b, 0, 0)),
                      pl.BlockSpec((1,) + wi_hq.shape[1:], lambda b, s, pt: (b, 0, 0)),
                      pl.BlockSpec((1, di, n_pad), lambda b, s, pt: (b, 0, 0))]
            + [page_spec((di, PAGE_SIZE), j) for j in range(pg)],
            out_specs=[pl.BlockSpec((1, ns + 1, n_new, pgk), lambda b, s, pt: (b, 0, 0, 0)),
                       pl.BlockSpec((1, n_new, LANES), lambda b, s, pt: (b, 0, 0)),
                       pl.BlockSpec((1, n_new, LANES), lambda b, s, pt: (b, 0, 0))]),
        out_shape=[jax.ShapeDtypeStruct((bsz, ns + 1, n_new, pgk), f32),
                   jax.ShapeDtypeStruct((bsz, n_new, LANES), f32),
                   jax.ShapeDtypeStruct((bsz, n_new, LANES), f32)],
        compiler_params=_cparams(2), name="sattn_select",
    )(page_table, qi_hq, wi_hq, kin_t, *([cache_ikt] * pg))

    kv_page = (PAGE_SIZE * N_KV_HEADS, dh)
    return pl.pallas_call(
        functools.partial(_sattn_attend_kernel, pg=pg, ns=ns, n_new=n_new),
        grid_spec=pltpu.PrefetchScalarGridSpec(
            num_scalar_prefetch=1, grid=(bsz, ns),
            in_specs=[pl.BlockSpec((1, rows, dh), lambda b, s, pt: (b, 0, 0)),
                      pl.BlockSpec((1, 1, n_new, pgk), lambda b, s, pt: (b, s, 0, 0)),
                      pl.BlockSpec((1, 1, n_new, pgk), lambda b, s, pt: (b, ns, 0, 0)),
                      pl.BlockSpec((1, n_new, LANES), lambda b, s, pt: (b, 0, 0)),
                      pl.BlockSpec((1, n_new, LANES), lambda b, s, pt: (b, 0, 0)),
                      pl.BlockSpec((1, n_pad, dkv), lambda b, s, pt: (b, 0, 0)),
                      pl.BlockSpec((1, n_pad, dkv), lambda b, s, pt: (b, 0, 0))]
            + [page_spec(kv_page, j) for j in range(pg)] * 2,
            out_specs=pl.BlockSpec((1, rows, dh), lambda b, s, pt: (b, 0, 0)),
            scratch_shapes=[pltpu.VMEM((rows, 1), f32), pltpu.VMEM((rows, 1), f32),
                            pltpu.VMEM((rows, dh), f32)]),
        out_shape=jax.ShapeDtypeStruct((bsz, rows, dh), ACT_DTYPE),
        compiler_params=_cparams(2), name="sattn_attend",
    )(page_table, q_hq, s_all, s_all, thr, jsel, knew, vnew, *([cache_k3] * pg), *([cache_v3] * pg))


def _merge_kernel(g_ref, o_ref, ga_ref, gb_ref, x_ref, wa_ref, wb_ref, wo_ref, lg_ref, lb_ref, out_ref,
                  *, alpha):
    ya = jnp.dot(g_ref[0].astype(MXU_DTYPE), wa_ref[...], preferred_element_type=f32)
    yb = jnp.dot(o_ref[0].astype(MXU_DTYPE), wb_ref[...], preferred_element_type=f32)
    mixed = (jax.nn.sigmoid(ga_ref[0].astype(f32)) * ya + jax.nn.sigmoid(gb_ref[0].astype(f32)) * yb)
    mix = jnp.dot(mixed.astype(MXU_DTYPE), wo_ref[...], preferred_element_type=f32)
    out_ref[0] = _layer_norm(alpha * x_ref[0] + mix, lg_ref[...], lb_ref[...])


def _merge(g, o, ga, gb, x, wa, wb, wo, lg, lb, *, tile, alpha):
    bsz, t_len, d = x.shape
    row_spec = pl.BlockSpec((1, tile, d), lambda b, t: (b, t, 0))
    return pl.pallas_call(
        functools.partial(_merge_kernel, alpha=alpha), grid=(bsz, t_len // tile),
        in_specs=[row_spec] * 5 + [_resident(a.shape) for a in (wa, wb, wo, lg, lb)],
        out_specs=row_spec, out_shape=jax.ShapeDtypeStruct((bsz, t_len, d), f32),
        compiler_params=_cparams(2), name="merge",
    )(g, o, ga, gb, x, wa, wb, wo, lg, lb)


def _ffn_kernel(x_ref, buf_ref, wup_ref, cw_ref, cb_ref, wdn_ref, lg_ref, lb_ref, out_ref, nbuf_ref,
                carry_ref, act_ref, *, alpha, tstride, ck):
    @pl.when(pl.program_id(1) == 0)
    def _():
        carry_ref[...] = buf_ref[0]

    x = x_ref[0]
    xb = x.astype(MXU_DTYPE)
    d_ff = wdn_ref.shape[0]
    for c in range(d_ff // ck):
        halves = []
        for off in (c * ck, d_ff + c * ck):
            u = jnp.dot(xb, wup_ref[:, off:off + ck], preferred_element_type=f32)
            uc, ncarry = _causal_conv(u, carry_ref[:, off:off + ck], cw_ref[:, off:off + ck],
                                      cb_ref[:, off:off + ck], tstride)
            carry_ref[:, off:off + ck] = ncarry
            halves.append(uc)
        act_ref[:, c * ck:(c + 1) * ck] = (_gelu(halves[0]) * halves[1]).astype(act_ref.dtype)
    down = jnp.dot(act_ref[...], wdn_ref[...], preferred_element_type=f32)
    out_ref[0] = _layer_norm(alpha * x + down, lg_ref[...], lb_ref[...])
    nbuf_ref[0] = carry_ref[...]


def _ffn(x, buf, wup, cw, cb, wdn, lg, lb, *, tile, alpha, tstride):
    bsz, t_len, d = x.shape
    cr, f2 = buf.shape[1], buf.shape[2]
    row_spec = pl.BlockSpec((1, tile, d), lambda b, t: (b, t, 0))
    buf_spec = pl.BlockSpec((1, cr, f2), lambda b, t: (b, 0, 0))
    return pl.pallas_call(
        functools.partial(_ffn_kernel, alpha=alpha, tstride=tstride, ck=512),
        grid=(bsz, t_len // tile),
        in_specs=[row_spec, buf_spec] + [_resident(a.shape) for a in (wup, cw, cb, wdn, lg, lb)],
        out_specs=[row_spec, buf_spec],
        out_shape=[jax.ShapeDtypeStruct((bsz, t_len, d), f32),
                   jax.ShapeDtypeStruct((bsz, cr, f2), f32)],
        scratch_shapes=[pltpu.VMEM((cr, f2), f32), pltpu.VMEM((tile, wdn.shape[0]), MXU_DTYPE)],
        compiler_params=_cparams(2), name="ffn",
    )(x, buf, wup, cw, cb, wdn, lg, lb)


def _layer_weights(l, w_in, rnn_conv_w, rnn_conv_b, lru_wa, lru_ba, lru_wx, lru_bx, lru_lambda,
                   w_branch_a, w_branch_b, w_out, ln1_g, ln1_b, ffn_w_up, ffn_conv_w, ffn_conv_b,
                   ffn_w_down, ln2_g, ln2_b):
    d = w_in.shape[1]
    dh = d // N_HEADS
    di = (w_in.shape[2] - 4 * d - (N_HEADS + 2 * N_KV_HEADS) * dh - N_IDX_HEADS) // (N_IDX_HEADS + 1)
    sizes = (d, d, N_HEADS * dh, N_KV_HEADS * dh, N_KV_HEADS * dh, N_IDX_HEADS * di, di, N_IDX_HEADS, d, d)
    names = ("xr", "gr", "q", "k", "v", "qi", "ki", "wi", "ga", "gb")
    cols, off = {}, 0
    for name, size in zip(names, sizes):
        cols[name] = w_in[l][:, off:off + size].astype(MXU_DTYPE)
        off += size
    row = lambda v: v[l][None, :]
    return dict(
        cols=cols, dh=dh, di=di,
        cw=rnn_conv_w[l], cb=row(rnn_conv_b),
        wax=jnp.concatenate([lru_wa[l], lru_wx[l]], axis=2).astype(MXU_DTYPE),
        ba=row(lru_ba), bx=row(lru_bx), lam=row(lru_lambda),
        wa=w_branch_a[l].astype(MXU_DTYPE), wb=w_branch_b[l].astype(MXU_DTYPE),
        wo=w_out[l].astype(MXU_DTYPE), lg1=row(ln1_g), lb1=row(ln1_b),
        wup=ffn_w_up[l].astype(MXU_DTYPE), fcw=ffn_conv_w[l], fcb=row(ffn_conv_b),
        wdn=ffn_w_down[l].astype(MXU_DTYPE), lg2=row(ln2_g), lb2=row(ln2_b))


def _prompt_layer(x, w, alpha):
    bsz, t_len, d = x.shape
    dh, di, cols = w["dh"], w["di"], w["cols"]
    kc = ATTN_KC
    t_pad = -(-t_len // kc) * kc
    ptile = kc * _largest_tile(t_pad // kc, 3, 1)
    rtile = _largest_tile(t_len, 1024, 16)
    topk = min(TOPK_MAX, t_len // 4)
    one = lambda dtype: [(dtype, False)]
    std = [(cols["xr"], 1.0, one(f32)), (cols["gr"], 1.0, one(ACT_DTYPE)),
           (cols["k"], 1.0, [(f32, False), (MXU_DTYPE, True)]), (cols["v"], 1.0, one(f32)),
           (cols["ki"], 1.0, [(f32, False), (MXU_DTYPE, True)]),
           (cols["ga"], 1.0, one(ACT_DTYPE)), (cols["gb"], 1.0, one(ACT_DTYPE))]
    tr = [(cols["q"].T, dh ** -0.5 * LOG2E, MXU_DTYPE, False), (cols["qi"].T, di ** -0.5, MXU_DTYPE, False),
          (cols["wi"].T, N_IDX_HEADS ** -0.5, f32, False), (cols["v"].T, 1.0, MXU_DTYPE, True)]
    xr, gr, k, kb, v, ki, kib, ga, gb, qT, qiT, wiT, vT4 = _project(
        x, std, tr, tile=ptile, t_pad=t_pad, kc=kc)

    c_rnn, w_rnn = xr.shape[2], w["cw"].shape[0]
    g, h_last, nbuf = _rglru(
        xr, gr, jnp.zeros((bsz, 1, c_rnn), f32), jnp.zeros((bsz, SUBLANES, c_rnn), f32),
        w["cw"], w["cb"], w["wax"], w["ba"], w["bx"], w["lam"], tile=rtile, tstride=1)
    o = _attn_prompt(qT, qiT, wiT, kb, vT4, kib, t_len=t_len, topk=topk)
    x1 = _merge(g, o, ga, gb, x, w["wa"], w["wb"], w["wo"], w["lg1"], w["lb1"], tile=rtile, alpha=alpha)
    f2, w_ffn = w["fcw"].shape[1], w["fcw"].shape[0]
    x2, fbuf = _ffn(x1, jnp.zeros((bsz, SUBLANES, f2), f32), w["wup"], w["fcw"], w["fcb"], w["wdn"],
                    w["lg2"], w["lb2"], tile=_largest_tile(t_len, 512, 16), alpha=alpha, tstride=1)
    state = (k.reshape(bsz, t_len, N_KV_HEADS, dh), v.reshape(bsz, t_len, N_KV_HEADS, dh), ki,
             h_last[:, 0, :], nbuf[:, SUBLANES - (w_rnn - 1):, :], fbuf[:, SUBLANES - (w_ffn - 1):, :])
    return x2, state


def _to_time_major(a):
    a = jnp.swapaxes(a, 0, 1)
    return a.reshape((1, a.shape[0] * a.shape[1]) + a.shape[2:])


def _from_time_major(a, bsz):
    a = a.reshape((a.shape[1] // bsz, bsz) + a.shape[2:])
    return jnp.swapaxes(a, 0, 1)


def _sample_layer(x, w, alpha, layer, page_table, cache_ikt, cache_k3, cache_v3, h0, lru_buf, ffn_buf, bsz):
    rows, d = x.shape[1], x.shape[2]
    n_new = rows // bsz
    dh, di, cols = w["dh"], w["di"], w["cols"]
    past = page_table.shape[1] * PAGE_SIZE
    topk = min(TOPK_MAX, (past + n_new) // 4)
    one = lambda dtype: [(dtype, False)]
    std = [(cols["xr"], 1.0, one(f32)), (cols["gr"], 1.0, one(ACT_DTYPE)),
           (cols["q"], dh ** -0.5 * LOG2E, one(MXU_DTYPE)), (cols["k"], 1.0, one(f32)),
           (cols["v"], 1.0, one(f32)), (cols["qi"], di ** -0.5, one(MXU_DTYPE)),
           (cols["ki"], 1.0, one(f32)), (cols["wi"], N_IDX_HEADS ** -0.5, one(f32)),
           (cols["ga"], 1.0, one(ACT_DTYPE)), (cols["gb"], 1.0, one(ACT_DTYPE))]
    xr, gr, q, k, v, qi, ki, wi, ga, gb = _project(x, std, [], tile=rows, t_pad=rows, kc=ATTN_KC)

    g, h_last, nbuf = _rglru(xr, gr, h0[None], _to_time_major(lru_buf), w["cw"], w["cb"], w["wax"],
                             w["ba"], w["bx"], w["lam"], tile=rows, tstride=bsz)

    def heads_major(a, n_h):
        a = _from_time_major(a, bsz).reshape(bsz, n_new, n_h, -1)
        return jnp.swapaxes(a, 1, 2).reshape(bsz, n_h * n_new, -1)

    def pad_new(a):
        a = _from_time_major(a, bsz).astype(MXU_DTYPE)
        return jnp.pad(a, ((0, 0), (0, LANES - n_new), (0, 0)))

    kin_t = jnp.swapaxes(pad_new(ki), 1, 2)
    o = _attn_sample(layer, page_table, cache_ikt, cache_k3, cache_v3,
                     heads_major(qi, N_IDX_HEADS), heads_major(wi, N_IDX_HEADS), kin_t,
                     heads_major(q, N_HEADS), pad_new(k), pad_new(v), topk=topk)
    o = jnp.swapaxes(o.reshape(bsz, N_HEADS, n_new, dh), 1, 2).reshape(bsz, n_new, N_HEADS * dh)
    o = _to_time_major(o)

    x1 = _merge(g, o, ga, gb, x, w["wa"], w["wb"], w["wo"], w["lg1"], w["lb1"], tile=rows, alpha=alpha)
    x2, fbuf = _ffn(x1, _to_time_major(ffn_buf), w["wup"], w["fcw"], w["fcb"], w["wdn"], w["lg2"],
                    w["lb2"], tile=rows, alpha=alpha, tstride=bsz)
    state = (_from_time_major(k, bsz).reshape(bsz, n_new, N_KV_HEADS, dh),
             _from_time_major(v, bsz).reshape(bsz, n_new, N_KV_HEADS, dh),
             _from_time_major(ki, bsz), h_last[0], _from_time_major(nbuf, bsz), _from_time_major(fbuf, bsz))
    return x2, state


def kernel(x_prompt, x_sample, cache_k, cache_v, cache_idx_k, state_lru_h, state_lru_conv, state_ffn_conv,
           page_table, meta_tokens, w_in, rnn_conv_w, rnn_conv_b, lru_wa, lru_ba, lru_wx, lru_bx, lru_lambda,
           w_branch_a, w_branch_b, w_out, ln1_g, ln1_b, ffn_w_up, ffn_conv_w, ffn_conv_b, ffn_w_down,
           ln2_g, ln2_b):
    depth = w_in.shape[0]
    alpha = (2.0 * depth) ** 0.25
    bsz, _, d = x_prompt.shape
    dbsz = x_sample.shape[0]
    meta = jnp.broadcast_to(meta_tokens.astype(x_prompt.dtype), (bsz, N_META, d))
    xp = jnp.concatenate([meta, x_prompt], axis=1)
    xs = _to_time_major(x_sample)
    cache_k3 = cache_k.reshape(cache_k.shape[:2] + (-1, cache_k.shape[4]))
    cache_v3 = cache_v.reshape(cache_v.shape[:2] + (-1, cache_v.shape[4]))
    cache_ikt = jnp.swapaxes(cache_idx_k, 2, 3)
    p_states, s_states = [], []
    for l in range(depth):
        w = _layer_weights(l, w_in, rnn_conv_w, rnn_conv_b, lru_wa, lru_ba, lru_wx, lru_bx, lru_lambda,
                           w_branch_a, w_branch_b, w_out, ln1_g, ln1_b, ffn_w_up, ffn_conv_w, ffn_conv_b,
                           ffn_w_down, ln2_g, ln2_b)
        xp, st = _prompt_layer(xp, w, alpha)
        p_states.append(st)
        xs, st = _sample_layer(xs, w, alpha, l, page_table, cache_ikt, cache_k3, cache_v3,
                               state_lru_h[l], state_lru_conv[l], state_ffn_conv[l], dbsz)
        s_states.append(st)
    stack = lambda states, n: jnp.stack([st[n] for st in states])
    return ((xp[:, N_META:], _from_time_major(xs, dbsz))
            + tuple(stack(p_states, n) for n in range(6))
            + tuple(stack(s_states, n) for n in range(6)))
```

```python
import functools

import jax
import jax.numpy as jnp
from jax import lax
from jax.experimental import pallas as pl
from jax.experimental.pallas import tpu as pltpu

f32 = jnp.float32

N_META = 16
N_RNN_BLOCKS = 8
LRU_C = 8.0
N_HEADS = 8
N_KV_HEADS = 4
KV_GROUP = N_HEADS // N_KV_HEADS
N_IDX_HEADS = 8
TOPK_MAX = 256
PAGE_SIZE = 128
LN_EPS = 1e-5

NEG = -1e30
BIG = 1e30
NO_TIE_LIMIT = 1e9
LOG2E = 1.4426950408889634
F32_TINY = float(jnp.finfo(jnp.float32).tiny)

MXU_DTYPE = jnp.bfloat16
ACT_DTYPE = jnp.bfloat16

V7X_VMEM_BYTES = 64 * 1024 * 1024
VMEM_LIMIT = V7X_VMEM_BYTES * 7 // 8
LANES = 128
SUBLANES = 8

ATTN_QB = 256
ATTN_KC = 256
ATTN_KC2 = 1024
COUNT_CHAINS = 8
COUNT_ROWS = COUNT_CHAINS * SUBLANES
DENOM_ROWS = 16
DENOM_FLOOR = 2.0 ** -90
MAX_BISECT = 4096
MAX_TIE_BISECT = 64
SAMPLE_PAGES_PER_STEP = 16


def _cparams(n_grid):
    return pltpu.CompilerParams(
        dimension_semantics=("arbitrary",) * n_grid, vmem_limit_bytes=VMEM_LIMIT)


def _resident(shape):
    nd = len(shape)
    return pl.BlockSpec(tuple(shape), lambda *_: (0,) * nd, pipeline_mode=pl.Buffered(1))


def _largest_tile(n, limit, multiple):
    best = None
    for d in range(multiple, min(n, limit) + 1, multiple):
        if n % d == 0:
            best = d
    assert best is not None, (n, limit, multiple)
    return best


def _gelu(x):
    return jax.nn.gelu(x, approximate=True)


def _sigmoid(x):
    return 0.5 * jnp.tanh(0.5 * x) + 0.5


def _softplus(x):
    return jnp.maximum(x, 0.0) + jnp.log1p(jnp.exp(-jnp.abs(x)))


def _layer_norm(y, g, b):
    mu = jnp.mean(y, axis=-1, keepdims=True)
    d = y - mu
    var = jnp.mean(d * d, axis=-1, keepdims=True)
    return d * lax.rsqrt(var + LN_EPS) * g + b


def _causal_conv(u, carry, w, b, tstride):
    width = w.shape[0]
    rows = u.shape[0]
    y = b + w[width - 1:width, :] * u
    if tstride % SUBLANES == 0:
        ext = jnp.concatenate([carry, u], axis=0)
        for j in range(width - 1):
            y = y + w[j:j + 1, :] * ext[j * tstride:j * tstride + rows, :]
        return y, ext[rows:, :]
    assert tstride == 1 and carry.shape[0] == SUBLANES
    row = lax.broadcasted_iota(jnp.int32, (SUBLANES, u.shape[1]), 0)
    for s in range(1, width):
        rolled = pltpu.roll(u, s, axis=0)
        head = jnp.where(row < s, pltpu.roll(carry, s, axis=0), rolled[0:SUBLANES, :])
        shifted = jnp.concatenate([head, rolled[SUBLANES:, :]], axis=0)
        y = y + w[width - 1 - s:width - s, :] * shifted
    return y, u[rows - SUBLANES:, :]


def _proj_kernel(x_ref, *refs, n_outs, n_tr, scales, t_valid, tile, kc):
    n_w = len(n_outs) + n_tr
    w_refs = refs[:n_w]
    o_refs = list(refs[n_w:])
    t0 = pl.program_id(1) * tile
    row = lax.broadcasted_iota(jnp.int32, (tile, 1), 0) + t0
    xb = jnp.where(row < t_valid, x_ref[0], 0.0).astype(MXU_DTYPE)
    for n, n_out in enumerate(n_outs):
        y = jnp.dot(xb, w_refs[n][...], preferred_element_type=f32)
        if scales[n] != 1.0:
            y = y * scales[n]
        for _ in range(n_out):
            o = o_refs.pop(0)
            heads = y.shape[1] // o.shape[2]
            if heads == 1:
                o[0] = y.astype(o.dtype)
            else:
                for g in range(heads):
                    o[0, pl.ds(g, tile, stride=heads), :] = (
                        y[:, g * o.shape[2]:(g + 1) * o.shape[2]].astype(o.dtype))
    for n in range(len(n_outs), n_w):
        y = lax.dot_general(w_refs[n][...], xb, (((1,), (1,)), ((), ())),
                            preferred_element_type=f32)
        if scales[n] != 1.0:
            y = y * scales[n]
        o = o_refs.pop(0)
        if len(o.shape) == 4:
            for cc in range(tile // kc):
                o[0, cc] = y[:, cc * kc:(cc + 1) * kc].astype(o.dtype)
        else:
            o[0] = y.astype(o.dtype)


def _project(x, std, tr, *, tile, t_pad, kc):
    bsz, t_len, d = x.shape
    n_t = t_pad // tile
    in_specs = [pl.BlockSpec((1, tile, d), lambda b, t: (b, t, 0))]
    out_shapes, out_specs, scales = [], [], []
    for w, scale, outs in std:
        in_specs.append(_resident(w.shape))
        n = w.shape[1]
        for dtype, padded, heads in outs:
            rows = t_pad if padded else t_len
            out_shapes.append(jax.ShapeDtypeStruct((bsz, rows * heads, n // heads), dtype))
            out_specs.append(pl.BlockSpec((1, tile * heads, n // heads), lambda b, t: (b, t, 0)))
        scales.append(scale)
    for w, scale, dtype, chunked in tr:
        in_specs.append(_resident(w.shape))
        n = w.shape[0]
        if chunked:
            out_shapes.append(jax.ShapeDtypeStruct((bsz, t_pad // kc, n, kc), dtype))
            out_specs.append(pl.BlockSpec((1, tile // kc, n, kc), lambda b, t: (b, t, 0, 0)))
        else:
            out_shapes.append(jax.ShapeDtypeStruct((bsz, n, t_pad), dtype))
            out_specs.append(pl.BlockSpec((1, n, tile), lambda b, t: (b, 0, t)))
        scales.append(scale)
    kern = functools.partial(_proj_kernel, n_outs=tuple(len(outs) for _, _, outs in std), n_tr=len(tr),
                             scales=tuple(scales), t_valid=t_len, tile=tile, kc=kc)
    return pl.pallas_call(
        kern, grid=(bsz, n_t), in_specs=in_specs, out_specs=out_specs, out_shape=out_shapes,
        compiler_params=_cparams(2), name="proj",
    )(x, *[w for w, *_ in std], *[w for w, *_ in tr])


def _lru_gates(xc, wax_ref, ba, bx, lam):
    blk = wax_ref.shape[1]
    xcb = xc.astype(MXU_DTYPE)
    rs, gs = [], []
    for n in range(wax_ref.shape[0]):
        y = jnp.dot(xcb[:, n * blk:(n + 1) * blk], wax_ref[n], preferred_element_type=f32)
        rs.append(y[:, :blk])
        gs.append(y[:, blk:])
    r = _sigmoid(jnp.concatenate(rs, axis=1) + ba)
    i = _sigmoid(jnp.concatenate(gs, axis=1) + bx)
    log_a = (-LRU_C) * r * _softplus(-lam)
    a = jnp.exp(log_a)
    th = jnp.tanh(log_a)
    one_minus_a2 = -2.0 * th / (1.0 - th)
    root = one_minus_a2 * lax.rsqrt(jnp.maximum(one_minus_a2, F32_TINY))
    return a, root * (i * xc)


def _scan8(a8, b8, h_prev):
    row = lax.broadcasted_iota(jnp.int32, a8.shape, 0)
    a, b = a8, b8
    for s in (1, 2, 4):
        ar = pltpu.roll(a, s, axis=0)
        br = pltpu.roll(b, s, axis=0)
        m = row >= s
        b = jnp.where(m, a * br + b, b)
        a = jnp.where(m, a * ar, a)
    return a * h_prev + b


def _rglru_kernel(xr_ref, gr_ref, h0_ref, buf_ref, cw_ref, cb_ref, wax_ref, ba_ref, bx_ref, lam_ref,
                  g_ref, hl_ref, nbuf_ref, a_s, b_s, hc_s, cc_s, *, tstride):
    @pl.when(pl.program_id(1) == 0)
    def _():
        cc_s[...] = buf_ref[0]
        hc_s[...] = jnp.broadcast_to(h0_ref[0], hc_s.shape)

    xr = xr_ref[0]
    rows = xr.shape[0]
    xc, ncarry = _causal_conv(xr, cc_s[...], cw_ref[...], cb_ref[...], tstride)
    cc_s[...] = ncarry
    nbuf_ref[0] = ncarry
    a, inp = _lru_gates(xc, wax_ref, ba_ref[...], bx_ref[...], lam_ref[...])
    if tstride == 1:
        a_s[...] = a
        b_s[...] = inp

        def body(k, h_prev):
            r0 = pl.multiple_of(k * SUBLANES, SUBLANES)
            h = _scan8(a_s[pl.ds(r0, SUBLANES), :], b_s[pl.ds(r0, SUBLANES), :], h_prev)
            b_s[pl.ds(r0, SUBLANES), :] = h
            return jnp.broadcast_to(h[SUBLANES - 1:SUBLANES, :], h.shape)

        h_last = lax.fori_loop(0, rows // SUBLANES, body, hc_s[...])
        hc_s[...] = h_last
        hl_ref[0] = h_last[0:1, :]
        hs = b_s[...]
    else:
        h = hc_s[...]
        pieces = []
        for s in range(rows // tstride):
            h = a[s * tstride:(s + 1) * tstride, :] * h + inp[s * tstride:(s + 1) * tstride, :]
            pieces.append(h)
        hc_s[...] = h
        hl_ref[0] = h
        hs = jnp.concatenate(pieces, axis=0)
    g_ref[0] = (hs * _gelu(gr_ref[0].astype(f32))).astype(g_ref.dtype)


def _rglru(xr, gr, h0, buf, cw, cb, wax, ba, bx, lam, *, tile, tstride):
    bsz, t_len, c = xr.shape
    hr, cr = h0.shape[1], buf.shape[1]
    hrows = SUBLANES if tstride == 1 else hr
    row_spec = pl.BlockSpec((1, tile, c), lambda b, t: (b, t, 0))
    scratch = [pltpu.VMEM((tile, c), f32), pltpu.VMEM((tile, c), f32),
               pltpu.VMEM((hrows, c), f32), pltpu.VMEM((cr, c), f32)]
    return pl.pallas_call(
        functools.partial(_rglru_kernel, tstride=tstride),
        grid=(bsz, t_len // tile),
        in_specs=[row_spec, row_spec,
                  pl.BlockSpec((1, hr, c), lambda b, t: (b, 0, 0)),
                  pl.BlockSpec((1, cr, c), lambda b, t: (b, 0, 0)),
                  _resident(cw.shape), _resident(cb.shape), _resident(wax.shape),
                  _resident(ba.shape), _resident(bx.shape), _resident(lam.shape)],
        out_specs=[row_spec,
                   pl.BlockSpec((1, hr, c), lambda b, t: (b, 0, 0)),
                   pl.BlockSpec((1, cr, c), lambda b, t: (b, 0, 0))],
        out_shape=[jax.ShapeDtypeStruct((bsz, t_len, c), ACT_DTYPE),
                   jax.ShapeDtypeStruct((bsz, hr, c), f32),
                   jax.ShapeDtypeStruct((bsz, cr, c), f32)],
        scratch_shapes=scratch, compiler_params=_cparams(2), name="rglru",
    )(xr, gr, h0, buf, cw, cb, wax, ba, bx, lam)


def _select_threshold(count_ge, count_tie, lo0, hi0, n_allowed, n_keys, topk):
    kf = float(topk)

    def n_active(done):
        return jnp.sum(jnp.where(done, 0.0, 1.0))

    c0 = count_ge(hi0)
    take = jnp.logical_and(n_allowed > kf, c0 >= kf)
    lo = jnp.where(take, hi0, lo0)
    cnt_lo = jnp.where(take, c0, n_allowed)
    cnt_hi = jnp.where(take, 0.0, c0)
    stalled = jnp.zeros_like(lo0)

    def bis_cond(st):
        return jnp.logical_and(st[-1] > 0.0, st[-2] < MAX_BISECT)

    def bis_body(st):
        lo, hi, cnt_lo, cnt_hi, stalled, it, _ = st
        live = jnp.logical_and(cnt_lo > kf, stalled <= 0.0)
        n_live = jnp.sum(jnp.where(live, 1.0, 0.0))
        mid = lo + (hi - lo) * 0.5
        mid = jnp.where(jnp.logical_and(lo < 0.0, hi > 0.0), 0.0, mid)
        mid = jnp.where(jnp.logical_and(lo == 0.0, hi > F32_TINY), F32_TINY, mid)
        mid = jnp.where(jnp.logical_and(hi == 0.0, lo < -F32_TINY), -F32_TINY, mid)
        c = count_ge(mid)
        up = jnp.logical_and(live, c >= kf)
        dn = jnp.logical_and(live, c < kf)
        stall_now = jnp.logical_and(live, jnp.logical_or(mid <= lo, mid >= hi))
        lo = jnp.where(up, mid, lo)
        cnt_lo = jnp.where(up, c, cnt_lo)
        hi = jnp.where(dn, mid, hi)
        cnt_hi = jnp.where(dn, c, cnt_hi)
        stalled = jnp.where(stall_now, 1.0, stalled)
        return lo, hi, cnt_lo, cnt_hi, stalled, it + 1, n_live

    st = (lo, hi0, cnt_lo, cnt_hi, stalled, jnp.int32(0), n_active(cnt_lo <= kf))
    lo, _, cnt_lo, cnt_hi, _, _, _ = lax.while_loop(bis_cond, bis_body, st)
    t = lo

    need = cnt_lo > kf
    want = kf - cnt_hi

    def tie_cond(st):
        return jnp.logical_and(st[-1] > 0.0, st[-2] < MAX_TIE_BISECT)

    def tie_body(st):
        jl, jh, jf, found, it, _ = st
        live = found <= 0.0
        jm = jnp.floor((jl + jh) * 0.5)
        c = count_tie(t, jm)
        hit = jnp.logical_and(live, c == want)
        jf = jnp.where(hit, jm, jf)
        found = jnp.where(hit, 1.0, found)
        jl = jnp.where(jnp.logical_and(live, c < want), jm, jl)
        jh = jnp.where(jnp.logical_and(live, c > want), jm, jh)
        return jl, jh, jf, found, it + 1, n_active(found > 0.0)

    found0 = jnp.where(need, 0.0, 1.0)
    st = (jnp.zeros_like(t), jnp.zeros_like(t) + n_keys, jnp.full_like(t, NO_TIE_LIMIT), found0,
          jnp.int32(0), n_active(found0 > 0.0))
    _, _, jf, _, _, _ = lax.while_loop(tie_cond, tie_body, st)
    return t, jnp.where(need, jf, NO_TIE_LIMIT)


def _attn_prompt_kernel(qT_ref, qiT_ref, wiT_ref, kb_ref, vT_ref, kib_ref, o_ref,
                        s_ref, q2_ref, acc_ref, m_ref, sa_ref, pb_ref, al_ref,
                        *, qb, kc, kc2, topk):
    i = pl.program_id(1)
    n_ih = wiT_ref.shape[1]
    di = qiT_ref.shape[1] // n_ih
    n_grp, dh = q2_ref.shape[0], q2_ref.shape[1]
    grp = q2_ref.shape[2] // qb
    c_last = (i * qb) // kc
    n_ch = c_last + 1
    n_ch2 = (n_ch * kc + kc2 - 1) // kc2

    qi_all = jnp.concatenate([qiT_ref[0, h * di:(h + 1) * di, :] for h in range(n_ih)], axis=1)
    wi = wiT_ref[0]
    kio = lax.broadcasted_iota(jnp.int32, (kc, qb), 0)
    qio = lax.broadcasted_iota(jnp.int32, (kc, qb), 1)

    def chunk_scores(c):
        k0 = pl.multiple_of(c * kc, kc)
        rel = jnp.dot(kib_ref[0, pl.ds(k0, kc), :], qi_all, preferred_element_type=f32)
        sc = wi[0:1, :] * jnp.maximum(rel[:, 0:qb], 0.0)
        for h in range(1, n_ih):
            sc = sc + wi[h:h + 1, :] * jnp.maximum(rel[:, h * qb:(h + 1) * qb], 0.0)
        return k0, sc

    def fold(x, op):
        return op(x.reshape(kc // SUBLANES, SUBLANES, qb), axis=0)

    def p1_body(k, carry):
        vmax, vmin = carry
        for c in (2 * k, jnp.minimum(2 * k + 1, c_last - 1)):
            k0, sc = chunk_scores(c)
            s_ref[pl.ds(k0, kc), :] = sc
            vmax = jnp.maximum(vmax, fold(sc, jnp.max))
            vmin = jnp.minimum(vmin, fold(sc, jnp.min))
        return vmax, vmin

    vmax, vmin = lax.fori_loop(
        0, (c_last + 1) // 2, p1_body,
        (jnp.full((SUBLANES, qb), NEG, f32), jnp.full((SUBLANES, qb), BIG, f32)))
    k0, sc = chunk_scores(c_last)
    allowed = (kio + k0) <= (qio + i * qb)
    s_ref[pl.ds(k0, kc), :] = jnp.where(allowed, sc, NEG)
    vmax = jnp.maximum(vmax, fold(jnp.where(allowed, sc, NEG), jnp.max))
    vmin = jnp.minimum(vmin, fold(jnp.where(allowed, sc, BIG), jnp.min))

    def fill_body(c, carry):
        s_ref[pl.ds(pl.multiple_of(c * kc, kc), kc), :] = jnp.full((kc, qb), NEG, f32)
        return carry

    lax.fori_loop(n_ch, n_ch2 * (kc2 // kc), fill_body, 0)

    def count_slabs(hit_fn):
        def body(c, acc):
            r0 = pl.multiple_of(c * kc2, kc2)
            for s in range(kc2 // COUNT_ROWS):
                r = pl.multiple_of(r0 + s * COUNT_ROWS, COUNT_ROWS)
                acc = jnp.where(hit_fn(s_ref[pl.ds(r, COUNT_ROWS), :], r), acc + 1.0, acc)
            return acc
        acc = lax.fori_loop(0, n_ch2, body, jnp.zeros((COUNT_ROWS, qb), f32))
        return jnp.sum(acc, axis=0, keepdims=True)

    def count_ge(x):
        xb = jnp.broadcast_to(x, (COUNT_ROWS, qb))
        return count_slabs(lambda blk, r: blk >= xb)

    def count_tie(t, j):
        tb = jnp.broadcast_to(t, (COUNT_ROWS, qb))
        row = lax.broadcasted_iota(jnp.int32, (COUNT_ROWS, qb), 0).astype(f32)

        def hit(blk, r):
            return jnp.logical_and(blk == tb, row < j - r.astype(f32))
        return count_slabs(hit)

    n_allowed = (lax.broadcasted_iota(jnp.int32, (1, qb), 1) + (i * qb + 1)).astype(f32)
    t, jsel = _select_threshold(
        count_ge, count_tie,
        jnp.min(vmin, axis=0, keepdims=True), jnp.max(vmax, axis=0, keepdims=True),
        n_allowed, (n_ch * kc).astype(f32), topk)

    for g in range(n_grp):
        q2_ref[g] = jnp.concatenate(
            [qT_ref[0, (g * grp + j) * dh:(g * grp + j + 1) * dh, :] for j in range(grp)], axis=1)
    ones_rows = jnp.ones((DENOM_ROWS, kc), pb_ref.dtype)

    def attend(exact):
        m_ref[...] = jnp.full(m_ref.shape, NEG, f32)
        acc_ref[...] = jnp.zeros(acc_ref.shape, f32)

        def qk_stage(c, slot, g):
            k0 = pl.multiple_of(jnp.minimum(c, n_ch - 1) * kc, kc)
            sa_ref[slot, g] = jnp.dot(kb_ref[0, pl.ds(k0, kc), g * dh:(g + 1) * dh], q2_ref[g],
                                      preferred_element_type=f32)

        def selection_mask(c):
            valid = c < n_ch
            k0 = pl.multiple_of(jnp.minimum(c, n_ch - 1) * kc, kc)
            blk = s_ref[pl.ds(k0, kc), :]
            kidx = (kio + k0).astype(f32)
            t_c = jnp.where(valid, t, BIG)
            j_c = jnp.where(valid, jsel, -1.0)
            sel = jnp.logical_or(blk > t_c, jnp.logical_and(blk == t_c, kidx < j_c))
            if exact:
                return jnp.concatenate([jnp.where(sel, 0.0, NEG)] * grp, axis=1)
            return jnp.concatenate([jnp.where(sel, 1.0, 0.0).astype(pb_ref.dtype)] * grp, axis=1)

        def softmax_stage(mask, slot, g):
            st = sa_ref[slot, g]
            if exact:
                st = st + mask
            m_old = m_ref[g]
            m_new = jnp.maximum(m_old, jnp.max(st, axis=0, keepdims=True))
            p = jnp.exp2(st - m_new).astype(pb_ref.dtype)
            pb_ref[slot, g] = p if exact else p * mask
            al_ref[slot, g] = jnp.exp2(m_old - m_new)
            m_ref[g] = m_new

        def pv_stage(c, slot, g):
            cc = jnp.clip(c, 0, n_ch - 1)
            v_aug = jnp.concatenate([vT_ref[0, cc, g * dh:(g + 1) * dh, :], ones_rows], axis=0)
            pv = jnp.dot(v_aug, pb_ref[slot, g], preferred_element_type=f32)
            acc_ref[g] = al_ref[slot, g] * acc_ref[g] + pv

        for g in range(n_grp):
            qk_stage(0, 0, g)
        pb_ref[1] = jnp.zeros(pb_ref.shape[1:], pb_ref.dtype)
        al_ref[1] = jnp.ones(al_ref.shape[1:], f32)

        def p3_body(k, carry):
            for slot in range(2):
                c = 2 * k + slot
                mask = selection_mask(c)
                for g in range(n_grp):
                    pv_stage(c - 1, 1 - slot, g)
                    qk_stage(c + 1, 1 - slot, g)
                    softmax_stage(mask, slot, g)
            return carry

        lax.fori_loop(0, (n_ch + 2) // 2, p3_body, 0)

    attend(exact=False)
    denom_min = jnp.min(acc_ref[:, dh:dh + 1, :])

    @pl.when(jnp.logical_not(denom_min >= DENOM_FLOOR))
    def _():
        attend(exact=True)

    for g in range(n_grp):
        o = acc_ref[g, 0:dh, :] / acc_ref[g, dh:dh + 1, :]
        for j in range(grp):
            h = g * grp + j
            o_ref[0, :, h * dh:(h + 1) * dh] = o[:, j * qb:(j + 1) * qb].T.astype(o_ref.dtype)


def _attn_prompt(qT, qiT, wiT, kb, vT4, kib, *, t_len, topk):
    bsz, dq, t_pad = qT.shape
    qb, kc, kc2 = ATTN_QB, ATTN_KC, ATTN_KC2
    dkv = kb.shape[2]
    dh = dq // N_HEADS
    s_rows = -(-t_pad // kc2) * kc2
    kern = functools.partial(_attn_prompt_kernel, qb=qb, kc=kc, kc2=kc2, topk=topk)
    return pl.pallas_call(
        kern, grid=(bsz, -(-t_len // qb)),
        in_specs=[pl.BlockSpec((1, dq, qb), lambda b, i: (b, 0, i)),
                  pl.BlockSpec((1, qiT.shape[1], qb), lambda b, i: (b, 0, i)),
                  pl.BlockSpec((1, wiT.shape[1], qb), lambda b, i: (b, 0, i)),
                  pl.BlockSpec((1, t_pad, dkv), lambda b, i: (b, 0, 0), pipeline_mode=pl.Buffered(1)),
                  pl.BlockSpec((1,) + vT4.shape[1:], lambda b, i: (b, 0, 0, 0),
                               pipeline_mode=pl.Buffered(1)),
                  pl.BlockSpec((1, t_pad, kib.shape[2]), lambda b, i: (b, 0, 0),
                               pipeline_mode=pl.Buffered(1))],
        out_specs=pl.BlockSpec((1, qb, dq), lambda b, i: (b, i, 0)),
        out_shape=jax.ShapeDtypeStruct((bsz, t_pad, dq), ACT_DTYPE),
        scratch_shapes=[pltpu.VMEM((s_rows, qb), f32),
                        pltpu.VMEM((N_KV_HEADS, dh, KV_GROUP * qb), MXU_DTYPE),
                        pltpu.VMEM((N_KV_HEADS, dh + DENOM_ROWS, KV_GROUP * qb), f32),
                        pltpu.VMEM((N_KV_HEADS, 1, KV_GROUP * qb), f32),
                        pltpu.VMEM((2, N_KV_HEADS, kc, KV_GROUP * qb), f32),
                        pltpu.VMEM((2, N_KV_HEADS, kc, KV_GROUP * qb), MXU_DTYPE),
                        pltpu.VMEM((2, N_KV_HEADS, 1, KV_GROUP * qb), f32)],
        compiler_params=_cparams(2), name="attn_prompt",
    )(qT, qiT, wiT, kb, vT4, kib)


def _sattn_select_kernel(pt_ref, qi_ref, wi_ref, kin_ref, *rest, pg, ns, topk, n_new):
    del pt_ref
    pages = rest[:pg]
    s_ref, t_ref, j_ref = rest[pg:]
    st = pl.program_id(1)
    pgk = pg * PAGE_SIZE
    qi = qi_ref[0]
    wi = wi_ref[0]
    n_ih = qi.shape[0] // n_new

    def scores(keys_t):
        rel = jnp.dot(qi, keys_t, preferred_element_type=f32)
        rel = jnp.maximum(rel, 0.0) * wi
        return jnp.sum(rel.reshape(n_ih, n_new, keys_t.shape[1]), axis=0)

    kp = jnp.concatenate([p[...] for p in pages], axis=1).astype(MXU_DTYPE)
    s_ref[0, st] = scores(kp)

    @pl.when(st == ns - 1)
    def _():
        n_pad = kin_ref.shape[2]
        sn = scores(kin_ref[0])
        lane = lax.broadcasted_iota(jnp.int32, (n_new, n_pad), 1)
        qrow = lax.broadcasted_iota(jnp.int32, (n_new, n_pad), 0)
        ok_new = lane <= qrow
        s_ref[0, ns] = jnp.concatenate(
            [jnp.where(ok_new, sn, NEG), jnp.full((n_new, pgk - n_pad), NEG, f32)], axis=1)

        s_all = s_ref[0]
        past = s_all[:-1]
        hi0 = jnp.maximum(jnp.max(jnp.max(past, axis=0), axis=1, keepdims=True),
                          jnp.max(jnp.where(ok_new, sn, NEG), axis=1, keepdims=True))
        lo0 = jnp.minimum(jnp.min(jnp.min(past, axis=0), axis=1, keepdims=True),
                          jnp.min(jnp.where(ok_new, sn, BIG), axis=1, keepdims=True))
        kidx = (lax.broadcasted_iota(jnp.int32, s_all.shape, 0) * pgk
                + lax.broadcasted_iota(jnp.int32, s_all.shape, 2)).astype(f32)

        def total(x):
            return jnp.sum(jnp.sum(x, axis=0), axis=1, keepdims=True)

        def count_ge(x):
            return total(jnp.where(s_all >= x[None], 1.0, 0.0))

        def count_tie(t, j):
            hit = jnp.logical_and(s_all == t[None], kidx < j[None])
            return total(jnp.where(hit, 1.0, 0.0))

        q1 = lax.broadcasted_iota(jnp.int32, (n_new, 1), 0)
        n_allowed = (q1 + (ns * pgk + 1)).astype(f32)
        t, jsel = _select_threshold(count_ge, count_tie, lo0, hi0, n_allowed, float(ns * pgk + n_pad),
                                    topk)
        t_ref[0] = jnp.broadcast_to(t, t_ref.shape[1:])
        j_ref[0] = jnp.broadcast_to(jsel, j_ref.shape[1:])


def _sattn_attend_kernel(pt_ref, q_ref, s_ref, snew_ref, t_ref, j_ref, knew_ref, vnew_ref, *rest,
                         pg, ns, n_new):
    del pt_ref
    kpages, vpages = rest[:pg], rest[pg:2 * pg]
    o_ref, m_s, l_s, acc_s = rest[2 * pg:]
    st = pl.program_id(1)
    pgk = pg * PAGE_SIZE
    q = q_ref[0]
    dh = q.shape[1]
    rows_g = KV_GROUP * n_new
    t = t_ref[0][:, 0:1]
    jsel = j_ref[0][:, 0:1]

    @pl.when(st == 0)
    def _():
        m_s[...] = jnp.full(m_s.shape, NEG, f32)
        l_s[...] = jnp.zeros(l_s.shape, f32)
        acc_s[...] = jnp.zeros(acc_s.shape, f32)

    def update(sc, base, keys_of, vals_of):
        n = sc.shape[1]
        kidx = (lax.broadcasted_iota(jnp.int32, (n_new, n), 1) + base).astype(f32)
        sel = jnp.logical_or(sc > t, jnp.logical_and(sc == t, kidx < jsel))
        bias = jnp.where(sel, 0.0, NEG)
        bias = jnp.concatenate([bias] * KV_GROUP, axis=0)
        s = jnp.concatenate(
            [lax.dot_general(q[g * rows_g:(g + 1) * rows_g, :], keys_of(g), (((1,), (1,)), ((), ())),
                             preferred_element_type=f32) + bias for g in range(N_KV_HEADS)], axis=0)
        m_old = m_s[...]
        m_new = jnp.maximum(m_old, jnp.max(s, axis=1, keepdims=True))
        alpha = jnp.exp2(m_old - m_new)
        p = jnp.exp2(s - m_new)
        l_s[...] = alpha * l_s[...] + jnp.sum(p, axis=1, keepdims=True)
        m_s[...] = m_new
        pb = p.astype(MXU_DTYPE)
        pv = jnp.concatenate(
            [jnp.dot(pb[g * rows_g:(g + 1) * rows_g, :], vals_of(g), preferred_element_type=f32)
             for g in range(N_KV_HEADS)], axis=0)
        acc_s[...] = alpha * acc_s[...] + pv

    def paged(pages):
        def head_rows(g):
            return jnp.concatenate([p[pl.ds(g, PAGE_SIZE, stride=N_KV_HEADS), :] for p in pages],
                                   axis=0).astype(MXU_DTYPE)
        return head_rows

    def fresh(ref):
        return lambda g: ref[0][:, g * dh:(g + 1) * dh]

    update(s_ref[0, 0], st * pgk, paged(kpages), paged(vpages))

    @pl.when(st == ns - 1)
    def _():
        n_pad = knew_ref.shape[1]
        update(snew_ref[0, 0][:, :n_pad], ns * pgk, fresh(knew_ref), fresh(vnew_ref))
        o_ref[0] = (acc_s[...] / l_s[...]).astype(o_ref.dtype)


def _attn_sample(layer, page_table, cache_ikt, cache_k3, cache_v3, qi_hq, wi_hq, kin_t, q_hq, knew, vnew,
                 *, topk):
    bsz, n_pages = page_table.shape
    n_new = qi_hq.shape[1] // N_IDX_HEADS
    pg = _largest_tile(n_pages, SAMPLE_PAGES_PER_STEP, 1)
    ns = n_pages // pg
    pgk = pg * PAGE_SIZE
    di = cache_ikt.shape[2]
    dh = cache_k3.shape[3]
    dkv = knew.shape[2]
    n_pad = knew.shape[1]
    rows = q_hq.shape[1]

    def page_spec(shape, j):
        return pl.BlockSpec((None, None) + shape, lambda b, s, pt: (layer, pt[b, s * pg + j], 0, 0))

    s_all, thr, jsel = pl.pallas_call(
        functools.partial(_sattn_select_kernel, pg=pg, ns=ns, topk=topk, n_new=n_new),
        grid_spec=pltpu.PrefetchScalarGridSpec(
            num_scalar_prefetch=1, grid=(bsz, ns),
            in_specs=[pl.BlockSpec((1,) + qi_hq.shape[1:], lambda b, s, pt: (b, 0, 0)),
                      pl.BlockSpec((1,) + wi_hq.shape[1:], lambda b, s, pt: (b, 0, 0)),
                      pl.BlockSpec((1, di, n_pad), lambda b, s, pt: (b, 0, 0))]
            + [page_spec((di, PAGE_SIZE), j) for j in range(pg)],
            out_specs=[pl.BlockSpec((1, ns + 1, n_new, pgk), lambda b, s, pt: (b, 0, 0, 0)),
                       pl.BlockSpec((1, n_new, LANES), lambda b, s, pt: (b, 0, 0)),
                       pl.BlockSpec((1, n_new, LANES), lambda b, s, pt: (b, 0, 0))]),
        out_shape=[jax.ShapeDtypeStruct((bsz, ns + 1, n_new, pgk), f32),
                   jax.ShapeDtypeStruct((bsz, n_new, LANES), f32),
                   jax.ShapeDtypeStruct((bsz, n_new, LANES), f32)],
        compiler_params=_cparams(2), name="sattn_select",
    )(page_table, qi_hq, wi_hq, kin_t, *([cache_ikt] * pg))

    kv_page = (PAGE_SIZE * N_KV_HEADS, dh)
    return pl.pallas_call(
        functools.partial(_sattn_attend_kernel, pg=pg, ns=ns, n_new=n_new),
        grid_spec=pltpu.PrefetchScalarGridSpec(
            num_scalar_prefetch=1, grid=(bsz, ns),
            in_specs=[pl.BlockSpec((1, rows, dh), lambda b, s, pt: (b, 0, 0)),
                      pl.BlockSpec((1, 1, n_new, pgk), lambda b, s, pt: (b, s, 0, 0)),
                      pl.BlockSpec((1, 1, n_new, pgk), lambda b, s, pt: (b, ns, 0, 0)),
                      pl.BlockSpec((1, n_new, LANES), lambda b, s, pt: (b, 0, 0)),
                      pl.BlockSpec((1, n_new, LANES), lambda b, s, pt: (b, 0, 0)),
                      pl.BlockSpec((1, n_pad, dkv), lambda b, s, pt: (b, 0, 0)),
                      pl.BlockSpec((1, n_pad, dkv), lambda b, s, pt: (b, 0, 0))]
            + [page_spec(kv_page, j) for j in range(pg)] * 2,
            out_specs=pl.BlockSpec((1, rows, dh), lambda b, s, pt: (b, 0, 0)),
            scratch_shapes=[pltpu.VMEM((rows, 1), f32), pltpu.VMEM((rows, 1), f32),
                            pltpu.VMEM((rows, dh), f32)]),
        out_shape=jax.ShapeDtypeStruct((bsz, rows, dh), ACT_DTYPE),
        compiler_params=_cparams(2), name="sattn_attend",
    )(page_table, q_hq, s_all, s_all, thr, jsel, knew, vnew, *([cache_k3] * pg), *([cache_v3] * pg))


def _merge_kernel(g_ref, o_ref, ga_ref, gb_ref, x_ref, wa_ref, wb_ref, wo_ref, lg_ref, lb_ref, out_ref,
                  *, alpha):
    ya = jnp.dot(g_ref[0].astype(MXU_DTYPE), wa_ref[...], preferred_element_type=f32)
    yb = jnp.dot(o_ref[0].astype(MXU_DTYPE), wb_ref[...], preferred_element_type=f32)
    mixed = _sigmoid(ga_ref[0].astype(f32)) * ya + _sigmoid(gb_ref[0].astype(f32)) * yb
    mix = jnp.dot(mixed.astype(MXU_DTYPE), wo_ref[...], preferred_element_type=f32)
    out_ref[0] = _layer_norm(alpha * x_ref[0] + mix, lg_ref[...], lb_ref[...])


def _merge(g, o, ga, gb, x, wa, wb, wo, lg, lb, *, tile, alpha):
    bsz, t_len, d = x.shape
    row_spec = pl.BlockSpec((1, tile, d), lambda b, t: (b, t, 0))
    return pl.pallas_call(
        functools.partial(_merge_kernel, alpha=alpha), grid=(bsz, t_len // tile),
        in_specs=[row_spec] * 5 + [_resident(a.shape) for a in (wa, wb, wo, lg, lb)],
        out_specs=row_spec, out_shape=jax.ShapeDtypeStruct((bsz, t_len, d), f32),
        compiler_params=_cparams(2), name="merge",
    )(g, o, ga, gb, x, wa, wb, wo, lg, lb)


def _ffn_kernel(x_ref, buf_ref, wup_ref, cw_ref, cb_ref, wdn_ref, lg_ref, lb_ref, out_ref, nbuf_ref,
                carry_ref, act_ref, *, alpha, tstride, ck):
    @pl.when(pl.program_id(1) == 0)
    def _():
        carry_ref[...] = buf_ref[0]

    x = x_ref[0]
    xb = x.astype(MXU_DTYPE)
    d_ff = wdn_ref.shape[0]
    for c in range(d_ff // ck):
        halves = []
        for off in (c * ck, d_ff + c * ck):
            u = jnp.dot(xb, wup_ref[:, off:off + ck], preferred_element_type=f32)
            uc, ncarry = _causal_conv(u, carry_ref[:, off:off + ck], cw_ref[:, off:off + ck],
                                      cb_ref[:, off:off + ck], tstride)
            carry_ref[:, off:off + ck] = ncarry
            halves.append(uc)
        act_ref[:, c * ck:(c + 1) * ck] = (_gelu(halves[0]) * halves[1]).astype(act_ref.dtype)
    down = jnp.dot(act_ref[...], wdn_ref[...], preferred_element_type=f32)
    out_ref[0] = _layer_norm(alpha * x + down, lg_ref[...], lb_ref[...])
    nbuf_ref[0] = carry_ref[...]


def _ffn(x, buf, wup, cw, cb, wdn, lg, lb, *, tile, alpha, tstride):
    bsz, t_len, d = x.shape
    cr, f2 = buf.shape[1], buf.shape[2]
    row_spec = pl.BlockSpec((1, tile, d), lambda b, t: (b, t, 0))
    buf_spec = pl.BlockSpec((1, cr, f2), lambda b, t: (b, 0, 0))
    return pl.pallas_call(
        functools.partial(_ffn_kernel, alpha=alpha, tstride=tstride, ck=512),
        grid=(bsz, t_len // tile),
        in_specs=[row_spec, buf_spec] + [_resident(a.shape) for a in (wup, cw, cb, wdn, lg, lb)],
        out_specs=[row_spec, buf_spec],
        out_shape=[jax.ShapeDtypeStruct((bsz, t_len, d), f32),
                   jax.ShapeDtypeStruct((bsz, cr, f2), f32)],
        scratch_shapes=[pltpu.VMEM((cr, f2), f32), pltpu.VMEM((tile, wdn.shape[0]), MXU_DTYPE)],
        compiler_params=_cparams(2), name="ffn",
    )(x, buf, wup, cw, cb, wdn, lg, lb)


def _layer_weights(l, w_in, rnn_conv_w, rnn_conv_b, lru_wa, lru_ba, lru_wx, lru_bx, lru_lambda,
                   w_branch_a, w_branch_b, w_out, ln1_g, ln1_b, ffn_w_up, ffn_conv_w, ffn_conv_b,
                   ffn_w_down, ln2_g, ln2_b):
    d = w_in.shape[1]
    dh = d // N_HEADS
    di = (w_in.shape[2] - 4 * d - (N_HEADS + 2 * N_KV_HEADS) * dh - N_IDX_HEADS) // (N_IDX_HEADS + 1)
    sizes = (d, d, N_HEADS * dh, N_KV_HEADS * dh, N_KV_HEADS * dh, N_IDX_HEADS * di, di, N_IDX_HEADS, d, d)
    names = ("xr", "gr", "q", "k", "v", "qi", "ki", "wi", "ga", "gb")
    cols, off = {}, 0
    for name, size in zip(names, sizes):
        cols[name] = w_in[l][:, off:off + size].astype(MXU_DTYPE)
        off += size
    row = lambda v: v[l][None, :]
    return dict(
        cols=cols, dh=dh, di=di,
        cw=rnn_conv_w[l], cb=row(rnn_conv_b),
        wax=jnp.concatenate([lru_wa[l], lru_wx[l]], axis=2).astype(MXU_DTYPE),
        ba=row(lru_ba), bx=row(lru_bx), lam=row(lru_lambda),
        wa=w_branch_a[l].astype(MXU_DTYPE), wb=w_branch_b[l].astype(MXU_DTYPE),
        wo=w_out[l].astype(MXU_DTYPE), lg1=row(ln1_g), lb1=row(ln1_b),
        wup=ffn_w_up[l].astype(MXU_DTYPE), fcw=ffn_conv_w[l], fcb=row(ffn_conv_b),
        wdn=ffn_w_down[l].astype(MXU_DTYPE), lg2=row(ln2_g), lb2=row(ln2_b))


def _prompt_layer(x, w, alpha):
    bsz, t_len, d = x.shape
    dh, di, cols = w["dh"], w["di"], w["cols"]
    kc = ATTN_KC
    t_pad = -(-t_len // kc) * kc
    ptile = kc * _largest_tile(t_pad // kc, 3, 1)
    rtile = _largest_tile(t_len, 1024, 16)
    topk = min(TOPK_MAX, t_len // 4)
    one = lambda dtype: [(dtype, False, 1)]
    std = [(cols["xr"], 1.0, one(f32)), (cols["gr"], 1.0, one(ACT_DTYPE)),
           (cols["k"], 1.0, [(f32, False, N_KV_HEADS), (MXU_DTYPE, True, 1)]),
           (cols["v"], 1.0, [(f32, False, N_KV_HEADS)]),
           (cols["ki"], 1.0, [(f32, False, 1), (MXU_DTYPE, True, 1)]),
           (cols["ga"], 1.0, one(ACT_DTYPE)), (cols["gb"], 1.0, one(ACT_DTYPE))]
    tr = [(cols["q"].T, dh ** -0.5 * LOG2E, MXU_DTYPE, False), (cols["qi"].T, di ** -0.5, MXU_DTYPE, False),
          (cols["wi"].T, N_IDX_HEADS ** -0.5, f32, False), (cols["v"].T, 1.0, MXU_DTYPE, True)]
    xr, gr, k, kb, v, ki, kib, ga, gb, qT, qiT, wiT, vT4 = _project(
        x, std, tr, tile=ptile, t_pad=t_pad, kc=kc)

    c_rnn, w_rnn = xr.shape[2], w["cw"].shape[0]
    g, h_last, nbuf = _rglru(
        xr, gr, jnp.zeros((bsz, 1, c_rnn), f32), jnp.zeros((bsz, SUBLANES, c_rnn), f32),
        w["cw"], w["cb"], w["wax"], w["ba"], w["bx"], w["lam"], tile=rtile, tstride=1)
    o = _attn_prompt(qT, qiT, wiT, kb, vT4, kib, t_len=t_len, topk=topk)
    x1 = _merge(g, o, ga, gb, x, w["wa"], w["wb"], w["wo"], w["lg1"], w["lb1"], tile=rtile, alpha=alpha)
    f2, w_ffn = w["fcw"].shape[1], w["fcw"].shape[0]
    x2, fbuf = _ffn(x1, jnp.zeros((bsz, SUBLANES, f2), f32), w["wup"], w["fcw"], w["fcb"], w["wdn"],
                    w["lg2"], w["lb2"], tile=_largest_tile(t_len, 512, 16), alpha=alpha, tstride=1)
    state = (k.reshape(bsz, t_len, N_KV_HEADS, dh), v.reshape(bsz, t_len, N_KV_HEADS, dh), ki,
             h_last[:, 0, :], nbuf[:, SUBLANES - (w_rnn - 1):, :], fbuf[:, SUBLANES - (w_ffn - 1):, :])
    return x2, state


def _to_time_major(a):
    a = jnp.swapaxes(a, 0, 1)
    return a.reshape((1, a.shape[0] * a.shape[1]) + a.shape[2:])


def _from_time_major(a, bsz):
    a = a.reshape((a.shape[1] // bsz, bsz) + a.shape[2:])
    return jnp.swapaxes(a, 0, 1)


def _sample_layer(x, w, alpha, layer, page_table, cache_ikt, cache_k3, cache_v3, h0, lru_buf, ffn_buf, bsz):
    rows, d = x.shape[1], x.shape[2]
    n_new = rows // bsz
    dh, di, cols = w["dh"], w["di"], w["cols"]
    past = page_table.shape[1] * PAGE_SIZE
    topk = min(TOPK_MAX, (past + n_new) // 4)
    one = lambda dtype: [(dtype, False, 1)]
    std = [(cols["xr"], 1.0, one(f32)), (cols["gr"], 1.0, one(ACT_DTYPE)),
           (cols["q"], dh ** -0.5 * LOG2E, one(MXU_DTYPE)), (cols["k"], 1.0, one(f32)),
           (cols["v"], 1.0, one(f32)), (cols["qi"], di ** -0.5, one(MXU_DTYPE)),
           (cols["ki"], 1.0, one(f32)), (cols["wi"], N_IDX_HEADS ** -0.5, one(f32)),
           (cols["ga"], 1.0, one(ACT_DTYPE)), (cols["gb"], 1.0, one(ACT_DTYPE))]
    xr, gr, q, k, v, qi, ki, wi, ga, gb = _project(x, std, [], tile=rows, t_pad=rows, kc=ATTN_KC)

    g, h_last, nbuf = _rglru(xr, gr, h0[None], _to_time_major(lru_buf), w["cw"], w["cb"], w["wax"],
                             w["ba"], w["bx"], w["lam"], tile=rows, tstride=bsz)

    def heads_major(a, n_h):
        a = _from_time_major(a, bsz).reshape(bsz, n_new, n_h, -1)
        return jnp.swapaxes(a, 1, 2).reshape(bsz, n_h * n_new, -1)

    def pad_new(a):
        a = _from_time_major(a, bsz).astype(MXU_DTYPE)
        return jnp.pad(a, ((0, 0), (0, LANES - n_new), (0, 0)))

    kin_t = jnp.swapaxes(pad_new(ki), 1, 2)
    o = _attn_sample(layer, page_table, cache_ikt, cache_k3, cache_v3,
                     heads_major(qi, N_IDX_HEADS), heads_major(wi, N_IDX_HEADS), kin_t,
                     heads_major(q, N_HEADS), pad_new(k), pad_new(v), topk=topk)
    o = jnp.swapaxes(o.reshape(bsz, N_HEADS, n_new, dh), 1, 2).reshape(bsz, n_new, N_HEADS * dh)
    o = _to_time_major(o)

    x1 = _merge(g, o, ga, gb, x, w["wa"], w["wb"], w["wo"], w["lg1"], w["lb1"], tile=rows, alpha=alpha)
    x2, fbuf = _ffn(x1, _to_time_major(ffn_buf), w["wup"], w["fcw"], w["fcb"], w["wdn"], w["lg2"],
                    w["lb2"], tile=rows, alpha=alpha, tstride=bsz)
    state = (_from_time_major(k, bsz).reshape(bsz, n_new, N_KV_HEADS, dh),
             _from_time_major(v, bsz).reshape(bsz, n_new, N_KV_HEADS, dh),
             _from_time_major(ki, bsz), h_last[0], _from_time_major(nbuf, bsz), _from_time_major(fbuf, bsz))
    return x2, state


def kernel(x_prompt, x_sample, cache_k, cache_v, cache_idx_k, state_lru_h, state_lru_conv, state_ffn_conv,
           page_table, meta_tokens, w_in, rnn_conv_w, rnn_conv_b, lru_wa, lru_ba, lru_wx, lru_bx, lru_lambda,
           w_branch_a, w_branch_b, w_out, ln1_g, ln1_b, ffn_w_up, ffn_conv_w, ffn_conv_b, ffn_w_down,
           ln2_g, ln2_b):
    depth = w_in.shape[0]
    alpha = (2.0 * depth) ** 0.25
    bsz, _, d = x_prompt.shape
    dbsz = x_sample.shape[0]
    meta = jnp.broadcast_to(meta_tokens.astype(x_prompt.dtype), (bsz, N_META, d))
    xp = jnp.concatenate([meta, x_prompt], axis=1)
    xs = _to_time_major(x_sample)
    cache_k3 = cache_k.reshape(cache_k.shape[:2] + (-1, cache_k.shape[4]))
    cache_v3 = cache_v.reshape(cache_v.shape[:2] + (-1, cache_v.shape[4]))
    cache_ikt = jnp.swapaxes(cache_idx_k, 2, 3)
    p_states, s_states = [], []
    for l in range(depth):
        w = _layer_weights(l, w_in, rnn_conv_w, rnn_conv_b, lru_wa, lru_ba, lru_wx, lru_bx, lru_lambda,
                           w_branch_a, w_branch_b, w_out, ln1_g, ln1_b, ffn_w_up, ffn_conv_w, ffn_conv_b,
                           ffn_w_down, ln2_g, ln2_b)
        xp, st = _prompt_layer(xp, w, alpha)
        p_states.append(st)
        xs, st = _sample_layer(xs, w, alpha, l, page_table, cache_ikt, cache_k3, cache_v3,
                               state_lru_h[l], state_lru_conv[l], state_ffn_conv[l], dbsz)
        s_states.append(st)
    stack = lambda states, n: jnp.stack([st[n] for st in states])
    return ((xp[:, N_META:], _from_time_major(xs, dbsz))
            + tuple(stack(p_states, n) for n in range(6))
            + tuple(stack(s_states, n) for n in range(6)))
```

```python
import functools

import jax
import jax.numpy as jnp
from jax import lax
from jax.experimental import pallas as pl
from jax.experimental.pallas import tpu as pltpu

f32 = jnp.float32

N_META = 16
N_RNN_BLOCKS = 8
LRU_C = 8.0
N_HEADS = 8
N_KV_HEADS = 4
KV_GROUP = N_HEADS // N_KV_HEADS
N_IDX_HEADS = 8
TOPK_MAX = 256
PAGE_SIZE = 128
LN_EPS = 1e-5

NEG = -1e30
BIG = 1e30
NO_TIE_LIMIT = 1e9
LOG2E = 1.4426950408889634
F32_TINY = float(jnp.finfo(jnp.float32).tiny)

MXU_DTYPE = jnp.bfloat16
ACT_DTYPE = jnp.bfloat16

V7X_VMEM_BYTES = 64 * 1024 * 1024
VMEM_LIMIT = V7X_VMEM_BYTES * 7 // 8
LANES = 128
SUBLANES = 8

ATTN_QB = 256
ATTN_KC = 256
ATTN_KC2 = 1024
COUNT_CHAINS = 8
COUNT_ROWS = COUNT_CHAINS * SUBLANES
PIPE_SLOTS = 2
DENOM_ROWS = 16
DENOM_FLOOR = 2.0 ** -90
MAX_BISECT = 4096
HI_MARGIN = 2.0 ** -20
FIRST_PROBE_FRACTION = 0.75
MAX_TIE_BISECT = 64
SAMPLE_PAGES_PER_STEP = 16


def _cparams(n_grid):
    return pltpu.CompilerParams(
        dimension_semantics=("arbitrary",) * n_grid, vmem_limit_bytes=VMEM_LIMIT)


def _resident(shape):
    nd = len(shape)
    return pl.BlockSpec(tuple(shape), lambda *_: (0,) * nd, pipeline_mode=pl.Buffered(1))


def _largest_tile(n, limit, multiple):
    best = None
    for d in range(multiple, min(n, limit) + 1, multiple):
        if n % d == 0:
            best = d
    assert best is not None, (n, limit, multiple)
    return best


def _gelu(x):
    return jax.nn.gelu(x, approximate=True)


def _sigmoid(x):
    return 0.5 * jnp.tanh(0.5 * x) + 0.5


def _softplus(x):
    return jnp.maximum(x, 0.0) + jnp.log1p(jnp.exp(-jnp.abs(x)))


def _layer_norm(y, g, b):
    mu = jnp.mean(y, axis=-1, keepdims=True)
    d = y - mu
    var = jnp.mean(d * d, axis=-1, keepdims=True)
    return d * lax.rsqrt(var + LN_EPS) * g + b


def _causal_conv(u, carry, w, b, tstride):
    width = w.shape[0]
    rows = u.shape[0]
    y = b + w[width - 1:width, :] * u
    if tstride % SUBLANES == 0:
        ext = jnp.concatenate([carry, u], axis=0)
        for j in range(width - 1):
            y = y + w[j:j + 1, :] * ext[j * tstride:j * tstride + rows, :]
        return y, ext[rows:, :]
    assert tstride == 1 and carry.shape[0] == SUBLANES
    row = lax.broadcasted_iota(jnp.int32, (SUBLANES, u.shape[1]), 0)
    for s in range(1, width):
        rolled = pltpu.roll(u, s, axis=0)
        head = jnp.where(row < s, pltpu.roll(carry, s, axis=0), rolled[0:SUBLANES, :])
        shifted = jnp.concatenate([head, rolled[SUBLANES:, :]], axis=0)
        y = y + w[width - 1 - s:width - s, :] * shifted
    return y, u[rows - SUBLANES:, :]


def _proj_kernel(x_ref, *refs, n_outs, n_tr, scales, t_valid, tile, kc):
    n_w = len(n_outs) + n_tr
    w_refs = refs[:n_w]
    o_refs = list(refs[n_w:])
    t0 = pl.program_id(1) * tile
    row = lax.broadcasted_iota(jnp.int32, (tile, 1), 0) + t0
    xb = jnp.where(row < t_valid, x_ref[0], 0.0).astype(MXU_DTYPE)
    for n, n_out in enumerate(n_outs):
        y = jnp.dot(xb, w_refs[n][...], preferred_element_type=f32)
        if scales[n] != 1.0:
            y = y * scales[n]
        for _ in range(n_out):
            o = o_refs.pop(0)
            heads = y.shape[1] // o.shape[2]
            if heads == 1:
                o[0] = y.astype(o.dtype)
            else:
                for g in range(heads):
                    o[0, pl.ds(g, tile, stride=heads), :] = (
                        y[:, g * o.shape[2]:(g + 1) * o.shape[2]].astype(o.dtype))
    for n in range(len(n_outs), n_w):
        y = lax.dot_general(w_refs[n][...], xb, (((1,), (1,)), ((), ())),
                            preferred_element_type=f32)
        if scales[n] != 1.0:
            y = y * scales[n]
        o = o_refs.pop(0)
        if len(o.shape) == 4:
            for cc in range(tile // kc):
                o[0, cc] = y[:, cc * kc:(cc + 1) * kc].astype(o.dtype)
        else:
            o[0] = y.astype(o.dtype)


def _project(x, std, tr, *, tile, t_pad, kc):
    bsz, t_len, d = x.shape
    n_t = t_pad // tile
    in_specs = [pl.BlockSpec((1, tile, d), lambda b, t: (b, t, 0))]
    out_shapes, out_specs, scales = [], [], []
    for w, scale, outs in std:
        in_specs.append(_resident(w.shape))
        n = w.shape[1]
        for dtype, padded, heads in outs:
            rows = t_pad if padded else t_len
            out_shapes.append(jax.ShapeDtypeStruct((bsz, rows * heads, n // heads), dtype))
            out_specs.append(pl.BlockSpec((1, tile * heads, n // heads), lambda b, t: (b, t, 0)))
        scales.append(scale)
    for w, scale, dtype, chunked in tr:
        in_specs.append(_resident(w.shape))
        n = w.shape[0]
        if chunked:
            out_shapes.append(jax.ShapeDtypeStruct((bsz, t_pad // kc, n, kc), dtype))
            out_specs.append(pl.BlockSpec((1, tile // kc, n, kc), lambda b, t: (b, t, 0, 0)))
        else:
            out_shapes.append(jax.ShapeDtypeStruct((bsz, n, t_pad), dtype))
            out_specs.append(pl.BlockSpec((1, n, tile), lambda b, t: (b, 0, t)))
        scales.append(scale)
    kern = functools.partial(_proj_kernel, n_outs=tuple(len(outs) for _, _, outs in std), n_tr=len(tr),
                             scales=tuple(scales), t_valid=t_len, tile=tile, kc=kc)
    return pl.pallas_call(
        kern, grid=(bsz, n_t), in_specs=in_specs, out_specs=out_specs, out_shape=out_shapes,
        compiler_params=_cparams(2), name="proj",
    )(x, *[w for w, *_ in std], *[w for w, *_ in tr])


def _lru_gates(xc, wax_ref, ba, bx, lam):
    blk = wax_ref.shape[1]
    xcb = xc.astype(MXU_DTYPE)
    rs, gs = [], []
    for n in range(wax_ref.shape[0]):
        y = jnp.dot(xcb[:, n * blk:(n + 1) * blk], wax_ref[n], preferred_element_type=f32)
        rs.append(y[:, :blk])
        gs.append(y[:, blk:])
    r = _sigmoid(jnp.concatenate(rs, axis=1) + ba)
    i = _sigmoid(jnp.concatenate(gs, axis=1) + bx)
    log_a = (-LRU_C) * r * _softplus(-lam)
    a = jnp.exp(log_a)
    th = jnp.tanh(log_a)
    one_minus_a2 = -2.0 * th / (1.0 - th)
    root = one_minus_a2 * lax.rsqrt(jnp.maximum(one_minus_a2, F32_TINY))
    return a, root * (i * xc)


def _scan8(a8, b8, h_prev):
    row = lax.broadcasted_iota(jnp.int32, a8.shape, 0)
    a, b = a8, b8
    for s in (1, 2, 4):
        ar = pltpu.roll(a, s, axis=0)
        br = pltpu.roll(b, s, axis=0)
        m = row >= s
        b = jnp.where(m, a * br + b, b)
        a = jnp.where(m, a * ar, a)
    return a * h_prev + b


def _rglru_kernel(xr_ref, gr_ref, h0_ref, buf_ref, cw_ref, cb_ref, wax_ref, ba_ref, bx_ref, lam_ref,
                  g_ref, hl_ref, nbuf_ref, a_s, b_s, hc_s, cc_s, *, tstride):
    @pl.when(pl.program_id(1) == 0)
    def _():
        cc_s[...] = buf_ref[0]
        hc_s[...] = jnp.broadcast_to(h0_ref[0], hc_s.shape)

    xr = xr_ref[0]
    rows = xr.shape[0]
    xc, ncarry = _causal_conv(xr, cc_s[...], cw_ref[...], cb_ref[...], tstride)
    cc_s[...] = ncarry
    nbuf_ref[0] = ncarry
    a, inp = _lru_gates(xc, wax_ref, ba_ref[...], bx_ref[...], lam_ref[...])
    if tstride == 1:
        a_s[...] = a
        b_s[...] = inp

        def body(k, h_prev):
            r0 = pl.multiple_of(k * SUBLANES, SUBLANES)
            h = _scan8(a_s[pl.ds(r0, SUBLANES), :], b_s[pl.ds(r0, SUBLANES), :], h_prev)
            b_s[pl.ds(r0, SUBLANES), :] = h
            return jnp.broadcast_to(h[SUBLANES - 1:SUBLANES, :], h.shape)

        h_last = lax.fori_loop(0, rows // SUBLANES, body, hc_s[...])
        hc_s[...] = h_last
        hl_ref[0] = h_last[0:1, :]
        hs = b_s[...]
    else:
        h = hc_s[...]
        pieces = []
        for s in range(rows // tstride):
            h = a[s * tstride:(s + 1) * tstride, :] * h + inp[s * tstride:(s + 1) * tstride, :]
            pieces.append(h)
        hc_s[...] = h
        hl_ref[0] = h
        hs = jnp.concatenate(pieces, axis=0)
    g_ref[0] = (hs * _gelu(gr_ref[0].astype(f32))).astype(g_ref.dtype)


def _rglru(xr, gr, h0, buf, cw, cb, wax, ba, bx, lam, *, tile, tstride):
    bsz, t_len, c = xr.shape
    hr, cr = h0.shape[1], buf.shape[1]
    hrows = SUBLANES if tstride == 1 else hr
    row_spec = pl.BlockSpec((1, tile, c), lambda b, t: (b, t, 0))
    scratch = [pltpu.VMEM((tile, c), f32), pltpu.VMEM((tile, c), f32),
               pltpu.VMEM((hrows, c), f32), pltpu.VMEM((cr, c), f32)]
    return pl.pallas_call(
        functools.partial(_rglru_kernel, tstride=tstride),
        grid=(bsz, t_len // tile),
        in_specs=[row_spec, row_spec,
                  pl.BlockSpec((1, hr, c), lambda b, t: (b, 0, 0)),
                  pl.BlockSpec((1, cr, c), lambda b, t: (b, 0, 0)),
                  _resident(cw.shape), _resident(cb.shape), _resident(wax.shape),
                  _resident(ba.shape), _resident(bx.shape), _resident(lam.shape)],
        out_specs=[row_spec,
                   pl.BlockSpec((1, hr, c), lambda b, t: (b, 0, 0)),
                   pl.BlockSpec((1, cr, c), lambda b, t: (b, 0, 0))],
        out_shape=[jax.ShapeDtypeStruct((bsz, t_len, c), ACT_DTYPE),
                   jax.ShapeDtypeStruct((bsz, hr, c), f32),
                   jax.ShapeDtypeStruct((bsz, cr, c), f32)],
        scratch_shapes=scratch, compiler_params=_cparams(2), name="rglru",
    )(xr, gr, h0, buf, cw, cb, wax, ba, bx, lam)


def _select_threshold(count_ge, count_tie, lo0, hi0, n_allowed, n_keys, topk):
    kf = float(topk)

    def n_active(done):
        return jnp.sum(jnp.where(done, 0.0, 1.0))

    lo = lo0
    hi_start = hi0 + (jnp.abs(hi0) * HI_MARGIN + F32_TINY)
    cnt_lo = n_allowed
    cnt_hi = jnp.zeros_like(lo0)
    stalled = jnp.zeros_like(lo0)
    frac0 = jnp.where(n_allowed > 4.0 * kf, FIRST_PROBE_FRACTION, 0.5)

    def bis_cond(st):
        return jnp.logical_and(st[-1] > 0.0, st[-2] < MAX_BISECT)

    def bis_body(st):
        lo, hi, cnt_lo, cnt_hi, stalled, frac, it, _ = st
        live = jnp.logical_and(cnt_lo > kf, stalled <= 0.0)
        n_live = jnp.sum(jnp.where(live, 1.0, 0.0))
        mid = lo + (hi - lo) * frac
        mid = jnp.where(jnp.logical_and(lo < 0.0, hi > 0.0), 0.0, mid)
        mid = jnp.where(jnp.logical_and(lo == 0.0, hi > F32_TINY), F32_TINY, mid)
        mid = jnp.where(jnp.logical_and(hi == 0.0, lo < -F32_TINY), -F32_TINY, mid)
        c = count_ge(mid)
        up = jnp.logical_and(live, c >= kf)
        dn = jnp.logical_and(live, c < kf)
        stall_now = jnp.logical_and(live, jnp.logical_or(mid <= lo, mid >= hi))
        lo = jnp.where(up, mid, lo)
        cnt_lo = jnp.where(up, c, cnt_lo)
        hi = jnp.where(dn, mid, hi)
        cnt_hi = jnp.where(dn, c, cnt_hi)
        stalled = jnp.where(stall_now, 1.0, stalled)
        return lo, hi, cnt_lo, cnt_hi, stalled, jnp.full_like(frac, 0.5), it + 1, n_live

    st = (lo, hi_start, cnt_lo, cnt_hi, stalled, frac0, jnp.int32(0), n_active(cnt_lo <= kf))
    lo, _, cnt_lo, cnt_hi, _, _, _, _ = lax.while_loop(bis_cond, bis_body, st)
    t = lo

    need = cnt_lo > kf
    want = kf - cnt_hi

    def tie_cond(st):
        return jnp.logical_and(st[-1] > 0.0, st[-2] < MAX_TIE_BISECT)

    def tie_body(st):
        jl, jh, jf, found, it, _ = st
        live = found <= 0.0
        jm = jnp.floor((jl + jh) * 0.5)
        c = count_tie(t, jm)
        hit = jnp.logical_and(live, c == want)
        jf = jnp.where(hit, jm, jf)
        found = jnp.where(hit, 1.0, found)
        jl = jnp.where(jnp.logical_and(live, c < want), jm, jl)
        jh = jnp.where(jnp.logical_and(live, c > want), jm, jh)
        return jl, jh, jf, found, it + 1, n_active(found > 0.0)

    found0 = jnp.where(need, 0.0, 1.0)
    st = (jnp.zeros_like(t), jnp.zeros_like(t) + n_keys, jnp.full_like(t, NO_TIE_LIMIT), found0,
          jnp.int32(0), n_active(found0 > 0.0))
    _, _, jf, _, _, _ = lax.while_loop(tie_cond, tie_body, st)
    return t, jnp.where(need, jf, NO_TIE_LIMIT)


def _attn_prompt_kernel(qT_ref, qiT_ref, wiT_ref, kb_ref, vT_ref, kib_ref, o_ref,
                        s_ref, q2_ref, acc_ref, m_ref, sa_ref, pb_ref, al_ref,
                        *, qb, kc, kc2, topk):
    i = pl.program_id(1)
    n_ih = wiT_ref.shape[1]
    di = qiT_ref.shape[1] // n_ih
    n_grp, dh = q2_ref.shape[0], q2_ref.shape[1]
    grp = q2_ref.shape[2] // qb
    c_last = (i * qb) // kc
    n_ch = c_last + 1
    n_ch2 = (n_ch * kc + kc2 - 1) // kc2

    qi_all = jnp.concatenate([qiT_ref[0, h * di:(h + 1) * di, :] for h in range(n_ih)], axis=1)
    wi = wiT_ref[0]
    kio = lax.broadcasted_iota(jnp.int32, (kc, qb), 0)
    qio = lax.broadcasted_iota(jnp.int32, (kc, qb), 1)

    def chunk_scores(c):
        k0 = pl.multiple_of(c * kc, kc)
        rel = jnp.dot(kib_ref[0, pl.ds(k0, kc), :], qi_all, preferred_element_type=f32)
        sc = wi[0:1, :] * jnp.maximum(rel[:, 0:qb], 0.0)
        for h in range(1, n_ih):
            sc = sc + wi[h:h + 1, :] * jnp.maximum(rel[:, h * qb:(h + 1) * qb], 0.0)
        return k0, sc

    def fold(x, op):
        return op(x.reshape(kc // SUBLANES, SUBLANES, qb), axis=0)

    def p1_body(k, carry):
        vmax, vmin = carry
        for c in (2 * k, jnp.minimum(2 * k + 1, c_last - 1)):
            k0, sc = chunk_scores(c)
            s_ref[pl.ds(k0, kc), :] = sc
            vmax = jnp.maximum(vmax, fold(sc, jnp.max))
            vmin = jnp.minimum(vmin, fold(sc, jnp.min))
        return vmax, vmin

    vmax, vmin = lax.fori_loop(
        0, (c_last + 1) // 2, p1_body,
        (jnp.full((SUBLANES, qb), NEG, f32), jnp.full((SUBLANES, qb), BIG, f32)))
    k0, sc = chunk_scores(c_last)
    allowed = (kio + k0) <= (qio + i * qb)
    s_ref[pl.ds(k0, kc), :] = jnp.where(allowed, sc, NEG)
    vmax = jnp.maximum(vmax, fold(jnp.where(allowed, sc, NEG), jnp.max))
    vmin = jnp.minimum(vmin, fold(jnp.where(allowed, sc, BIG), jnp.min))

    def fill_body(c, carry):
        s_ref[pl.ds(pl.multiple_of(c * kc, kc), kc), :] = jnp.full((kc, qb), NEG, f32)
        return carry

    lax.fori_loop(n_ch, n_ch2 * (kc2 // kc), fill_body, 0)

    def count_slabs(hit_fn):
        def body(c, acc):
            r0 = pl.multiple_of(c * kc2, kc2)
            for s in range(kc2 // COUNT_ROWS):
                r = pl.multiple_of(r0 + s * COUNT_ROWS, COUNT_ROWS)
                acc = jnp.where(hit_fn(s_ref[pl.ds(r, COUNT_ROWS), :], r), acc + 1.0, acc)
            return acc
        acc = lax.fori_loop(0, n_ch2, body, jnp.zeros((COUNT_ROWS, qb), f32))
        return jnp.sum(acc, axis=0, keepdims=True)

    def count_ge(x):
        xb = jnp.broadcast_to(x, (COUNT_ROWS, qb))
        return count_slabs(lambda blk, r: blk >= xb)

    def count_tie(t, j):
        tb = jnp.broadcast_to(t, (COUNT_ROWS, qb))
        row = lax.broadcasted_iota(jnp.int32, (COUNT_ROWS, qb), 0).astype(f32)

        def hit(blk, r):
            return jnp.logical_and(blk == tb, row < j - r.astype(f32))
        return count_slabs(hit)

    n_allowed = (lax.broadcasted_iota(jnp.int32, (1, qb), 1) + (i * qb + 1)).astype(f32)
    t, jsel = _select_threshold(
        count_ge, count_tie,
        jnp.min(vmin, axis=0, keepdims=True), jnp.max(vmax, axis=0, keepdims=True),
        n_allowed, (n_ch * kc).astype(f32), topk)

    for g in range(n_grp):
        q2_ref[g] = jnp.concatenate(
            [qT_ref[0, (g * grp + j) * dh:(g * grp + j + 1) * dh, :] for j in range(grp)], axis=1)
    ones_rows = jnp.ones((DENOM_ROWS, kc), pb_ref.dtype)

    def attend(exact):
        m_ref[...] = jnp.full(m_ref.shape, NEG, f32)
        acc_ref[...] = jnp.zeros(acc_ref.shape, f32)

        def qk_stage(c, slot, g):
            k0 = pl.multiple_of(jnp.minimum(c, n_ch - 1) * kc, kc)
            sa_ref[slot, g] = jnp.dot(kb_ref[0, pl.ds(k0, kc), g * dh:(g + 1) * dh], q2_ref[g],
                                      preferred_element_type=f32)

        def selection_mask(c):
            valid = c < n_ch
            k0 = pl.multiple_of(jnp.minimum(c, n_ch - 1) * kc, kc)
            blk = s_ref[pl.ds(k0, kc), :]
            kidx = (kio + k0).astype(f32)
            t_c = jnp.where(valid, t, BIG)
            j_c = jnp.where(valid, jsel, -1.0)
            sel = jnp.logical_or(blk > t_c, jnp.logical_and(blk == t_c, kidx < j_c))
            if exact:
                return jnp.concatenate([jnp.where(sel, 0.0, NEG)] * grp, axis=1)
            return jnp.concatenate([jnp.where(sel, 1.0, 0.0).astype(pb_ref.dtype)] * grp, axis=1)

        def softmax_stage(mask, slot, g):
            st = sa_ref[slot, g]
            if exact:
                st = st + mask
            m_old = m_ref[g]
            m_new = jnp.maximum(m_old, jnp.max(st, axis=0, keepdims=True))
            p = jnp.exp2(st - m_new).astype(pb_ref.dtype)
            pb_ref[slot, g] = p if exact else p * mask
            al_ref[slot, g] = jnp.exp2(m_old - m_new)
            m_ref[g] = m_new

        def pv_stage(c, slot, g):
            cc = jnp.clip(c, 0, n_ch - 1)
            v_aug = jnp.concatenate([vT_ref[0, cc, g * dh:(g + 1) * dh, :], ones_rows], axis=0)
            pv = jnp.dot(v_aug, pb_ref[slot, g], preferred_element_type=f32)
            acc_ref[g] = al_ref[slot, g] * acc_ref[g] + pv

        lag = PIPE_SLOTS // 2
        for c in range(lag):
            for g in range(n_grp):
                qk_stage(c, c, g)
        for slot in range(lag, PIPE_SLOTS):
            pb_ref[slot] = jnp.zeros(pb_ref.shape[1:], pb_ref.dtype)
            al_ref[slot] = jnp.ones(al_ref.shape[1:], f32)

        def p3_body(k, carry):
            for slot in range(PIPE_SLOTS):
                c = PIPE_SLOTS * k + slot
                mask = selection_mask(c)
                for g in range(n_grp):
                    pv_stage(c - lag, (slot + lag) % PIPE_SLOTS, g)
                    qk_stage(c + lag, (slot + lag) % PIPE_SLOTS, g)
                    softmax_stage(mask, slot, g)
            return carry

        lax.fori_loop(0, (n_ch + lag + PIPE_SLOTS - 1) // PIPE_SLOTS, p3_body, 0)

    attend(exact=False)
    denom_min = jnp.min(acc_ref[:, dh:dh + 1, :])

    @pl.when(jnp.logical_not(denom_min >= DENOM_FLOOR))
    def _():
        attend(exact=True)

    for g in range(n_grp):
        o = acc_ref[g, 0:dh, :] / acc_ref[g, dh:dh + 1, :]
        for j in range(grp):
            h = g * grp + j
            o_ref[0, :, h * dh:(h + 1) * dh] = o[:, j * qb:(j + 1) * qb].T.astype(o_ref.dtype)


def _attn_prompt(qT, qiT, wiT, kb, vT4, kib, *, t_len, topk):
    bsz, dq, t_pad = qT.shape
    qb, kc, kc2 = ATTN_QB, ATTN_KC, ATTN_KC2
    dkv = kb.shape[2]
    dh = dq // N_HEADS
    s_rows = -(-t_pad // kc2) * kc2
    kern = functools.partial(_attn_prompt_kernel, qb=qb, kc=kc, kc2=kc2, topk=topk)
    return pl.pallas_call(
        kern, grid=(bsz, -(-t_len // qb)),
        in_specs=[pl.BlockSpec((1, dq, qb), lambda b, i: (b, 0, i)),
                  pl.BlockSpec((1, qiT.shape[1], qb), lambda b, i: (b, 0, i)),
                  pl.BlockSpec((1, wiT.shape[1], qb), lambda b, i: (b, 0, i)),
                  pl.BlockSpec((1, t_pad, dkv), lambda b, i: (b, 0, 0), pipeline_mode=pl.Buffered(1)),
                  pl.BlockSpec((1,) + vT4.shape[1:], lambda b, i: (b, 0, 0, 0),
                               pipeline_mode=pl.Buffered(1)),
                  pl.BlockSpec((1, t_pad, kib.shape[2]), lambda b, i: (b, 0, 0),
                               pipeline_mode=pl.Buffered(1))],
        out_specs=pl.BlockSpec((1, qb, dq), lambda b, i: (b, i, 0)),
        out_shape=jax.ShapeDtypeStruct((bsz, t_pad, dq), ACT_DTYPE),
        scratch_shapes=[pltpu.VMEM((s_rows, qb), f32),
                        pltpu.VMEM((N_KV_HEADS, dh, KV_GROUP * qb), MXU_DTYPE),
                        pltpu.VMEM((N_KV_HEADS, dh + DENOM_ROWS, KV_GROUP * qb), f32),
                        pltpu.VMEM((N_KV_HEADS, 1, KV_GROUP * qb), f32),
                        pltpu.VMEM((PIPE_SLOTS, N_KV_HEADS, kc, KV_GROUP * qb), f32),
                        pltpu.VMEM((PIPE_SLOTS, N_KV_HEADS, kc, KV_GROUP * qb), MXU_DTYPE),
                        pltpu.VMEM((PIPE_SLOTS, N_KV_HEADS, 1, KV_GROUP * qb), f32)],
        compiler_params=_cparams(2), name="attn_prompt",
    )(qT, qiT, wiT, kb, vT4, kib)


def _sattn_select_kernel(pt_ref, qi_ref, wi_ref, kin_ref, *rest, pg, ns, topk, n_new):
    del pt_ref
    pages = rest[:pg]
    s_ref, t_ref, j_ref = rest[pg:]
    st = pl.program_id(1)
    pgk = pg * PAGE_SIZE
    qi = qi_ref[0]
    wi = wi_ref[0]
    n_ih = qi.shape[0] // n_new

    def scores(keys_t):
        rel = jnp.dot(qi, keys_t, preferred_element_type=f32)
        rel = jnp.maximum(rel, 0.0) * wi
        return jnp.sum(rel.reshape(n_ih, n_new, keys_t.shape[1]), axis=0)

    kp = jnp.concatenate([p[...] for p in pages], axis=1).astype(MXU_DTYPE)
    s_ref[0, st] = scores(kp)

    @pl.when(st == ns - 1)
    def _():
        n_pad = kin_ref.shape[2]
        sn = scores(kin_ref[0])
        lane = lax.broadcasted_iota(jnp.int32, (n_new, n_pad), 1)
        qrow = lax.broadcasted_iota(jnp.int32, (n_new, n_pad), 0)
        ok_new = lane <= qrow
        s_ref[0, ns] = jnp.concatenate(
            [jnp.where(ok_new, sn, NEG), jnp.full((n_new, pgk - n_pad), NEG, f32)], axis=1)

        s_all = s_ref[0]
        past = s_all[:-1]
        hi0 = jnp.maximum(jnp.max(jnp.max(past, axis=0), axis=1, keepdims=True),
                          jnp.max(jnp.where(ok_new, sn, NEG), axis=1, keepdims=True))
        lo0 = jnp.minimum(jnp.min(jnp.min(past, axis=0), axis=1, keepdims=True),
                          jnp.min(jnp.where(ok_new, sn, BIG), axis=1, keepdims=True))
        kidx = (lax.broadcasted_iota(jnp.int32, s_all.shape, 0) * pgk
                + lax.broadcasted_iota(jnp.int32, s_all.shape, 2)).astype(f32)

        def total(x):
            return jnp.sum(jnp.sum(x, axis=0), axis=1, keepdims=True)

        def count_ge(x):
            return total(jnp.where(s_all >= x[None], 1.0, 0.0))

        def count_tie(t, j):
            hit = jnp.logical_and(s_all == t[None], kidx < j[None])
            return total(jnp.where(hit, 1.0, 0.0))

        q1 = lax.broadcasted_iota(jnp.int32, (n_new, 1), 0)
        n_allowed = (q1 + (ns * pgk + 1)).astype(f32)
        t, jsel = _select_threshold(count_ge, count_tie, lo0, hi0, n_allowed, float(ns * pgk + n_pad),
                                    topk)
        t_ref[0] = jnp.broadcast_to(t, t_ref.shape[1:])
        j_ref[0] = jnp.broadcast_to(jsel, j_ref.shape[1:])


def _sattn_attend_kernel(pt_ref, q_ref, s_ref, snew_ref, t_ref, j_ref, knew_ref, vnew_ref, *rest,
                         pg, ns, n_new):
    del pt_ref
    kpages, vpages = rest[:pg], rest[pg:2 * pg]
    o_ref, m_s, l_s, acc_s = rest[2 * pg:]
    st = pl.program_id(1)
    pgk = pg * PAGE_SIZE
    q = q_ref[0]
    dh = q.shape[1]
    rows_g = KV_GROUP * n_new
    t = t_ref[0][:, 0:1]
    jsel = j_ref[0][:, 0:1]

    @pl.when(st == 0)
    def _():
        m_s[...] = jnp.full(m_s.shape, NEG, f32)
        l_s[...] = jnp.zeros(l_s.shape, f32)
        acc_s[...] = jnp.zeros(acc_s.shape, f32)

    def update(sc, base, keys_of, vals_of):
        n = sc.shape[1]
        kidx = (lax.broadcasted_iota(jnp.int32, (n_new, n), 1) + base).astype(f32)
        sel = jnp.logical_or(sc > t, jnp.logical_and(sc == t, kidx < jsel))
        bias = jnp.where(sel, 0.0, NEG)
        bias = jnp.concatenate([bias] * KV_GROUP, axis=0)
        s = jnp.concatenate(
            [lax.dot_general(q[g * rows_g:(g + 1) * rows_g, :], keys_of(g), (((1,), (1,)), ((), ())),
                             preferred_element_type=f32) + bias for g in range(N_KV_HEADS)], axis=0)
        m_old = m_s[...]
        m_new = jnp.maximum(m_old, jnp.max(s, axis=1, keepdims=True))
        alpha = jnp.exp2(m_old - m_new)
        p = jnp.exp2(s - m_new)
        l_s[...] = alpha * l_s[...] + jnp.sum(p, axis=1, keepdims=True)
        m_s[...] = m_new
        pb = p.astype(MXU_DTYPE)
        pv = jnp.concatenate(
            [jnp.dot(pb[g * rows_g:(g + 1) * rows_g, :], vals_of(g), preferred_element_type=f32)
             for g in range(N_KV_HEADS)], axis=0)
        acc_s[...] = alpha * acc_s[...] + pv

    def paged(pages):
        def head_rows(g):
            return jnp.concatenate([p[pl.ds(g, PAGE_SIZE, stride=N_KV_HEADS), :] for p in pages],
                                   axis=0).astype(MXU_DTYPE)
        return head_rows

    def fresh(ref):
        return lambda g: ref[0][:, g * dh:(g + 1) * dh]

    update(s_ref[0, 0], st * pgk, paged(kpages), paged(vpages))

    @pl.when(st == ns - 1)
    def _():
        n_pad = knew_ref.shape[1]
        update(snew_ref[0, 0][:, :n_pad], ns * pgk, fresh(knew_ref), fresh(vnew_ref))
        o_ref[0] = (acc_s[...] / l_s[...]).astype(o_ref.dtype)


def _attn_sample(layer, page_table, cache_ikt, cache_k3, cache_v3, qi_hq, wi_hq, kin_t, q_hq, knew, vnew,
                 *, topk):
    bsz, n_pages = page_table.shape
    n_new = qi_hq.shape[1] // N_IDX_HEADS
    pg = _largest_tile(n_pages, SAMPLE_PAGES_PER_STEP, 1)
    ns = n_pages // pg
    pgk = pg * PAGE_SIZE
    di = cache_ikt.shape[2]
    dh = cache_k3.shape[3]
    dkv = knew.shape[2]
    n_pad = knew.shape[1]
    rows = q_hq.shape[1]

    def page_spec(shape, j):
        return pl.BlockSpec((None, None) + shape, lambda b, s, pt: (layer, pt[b, s * pg + j], 0, 0))

    s_all, thr, jsel = pl.pallas_call(
        functools.partial(_sattn_select_kernel, pg=pg, ns=ns, topk=topk, n_new=n_new),
        grid_spec=pltpu.PrefetchScalarGridSpec(
            num_scalar_prefetch=1, grid=(bsz, ns),
            in_specs=[pl.BlockSpec((1,) + qi_hq.shape[1:], lambda b, s, pt: (b, 0, 0)),
                      pl.BlockSpec((1,) + wi_hq.shape[1:], lambda b, s, pt: (b, 0, 0)),
                      pl.BlockSpec((1, di, n_pad), lambda b, s, pt: (b, 0, 0))]
            + [page_spec((di, PAGE_SIZE), j) for j in range(pg)],
            out_specs=[pl.BlockSpec((1, ns + 1, n_new, pgk), lambda b, s, pt: (b, 0, 0, 0)),
                       pl.BlockSpec((1, n_new, LANES), lambda b, s, pt: (b, 0, 0)),
                       pl.BlockSpec((1, n_new, LANES), lambda b, s, pt: (b, 0, 0))]),
        out_shape=[jax.ShapeDtypeStruct((bsz, ns + 1, n_new, pgk), f32),
                   jax.ShapeDtypeStruct((bsz, n_new, LANES), f32),
                   jax.ShapeDtypeStruct((bsz, n_new, LANES), f32)],
        compiler_params=_cparams(2), name="sattn_select",
    )(page_table, qi_hq, wi_hq, kin_t, *([cache_ikt] * pg))

    kv_page = (PAGE_SIZE * N_KV_HEADS, dh)
    return pl.pallas_call(
        functools.partial(_sattn_attend_kernel, pg=pg, ns=ns, n_new=n_new),
        grid_spec=pltpu.PrefetchScalarGridSpec(
            num_scalar_prefetch=1, grid=(bsz, ns),
            in_specs=[pl.BlockSpec((1, rows, dh), lambda b, s, pt: (b, 0, 0)),
                      pl.BlockSpec((1, 1, n_new, pgk), lambda b, s, pt: (b, s, 0, 0)),
                      pl.BlockSpec((1, 1, n_new, pgk), lambda b, s, pt: (b, ns, 0, 0)),
                      pl.BlockSpec((1, n_new, LANES), lambda b, s, pt: (b, 0, 0)),
                      pl.BlockSpec((1, n_new, LANES), lambda b, s, pt: (b, 0, 0)),
                      pl.BlockSpec((1, n_pad, dkv), lambda b, s, pt: (b, 0, 0)),
                      pl.BlockSpec((1, n_pad, dkv), lambda b, s, pt: (b, 0, 0))]
            + [page_spec(kv_page, j) for j in range(pg)] * 2,
            out_specs=pl.BlockSpec((1, rows, dh), lambda b, s, pt: (b, 0, 0)),
            scratch_shapes=[pltpu.VMEM((rows, 1), f32), pltpu.VMEM((rows, 1), f32),
                            pltpu.VMEM((rows, dh), f32)]),
        out_shape=jax.ShapeDtypeStruct((bsz, rows, dh), ACT_DTYPE),
        compiler_params=_cparams(2), name="sattn_attend",
    )(page_table, q_hq, s_all, s_all, thr, jsel, knew, vnew, *([cache_k3] * pg), *([cache_v3] * pg))


def _merge_kernel(g_ref, o_ref, ga_ref, gb_ref, x_ref, wa_ref, wb_ref, wo_ref, lg_ref, lb_ref, out_ref,
                  *, alpha):
    ya = jnp.dot(g_ref[0].astype(MXU_DTYPE), wa_ref[...], preferred_element_type=f32)
    yb = jnp.dot(o_ref[0].astype(MXU_DTYPE), wb_ref[...], preferred_element_type=f32)
    mixed = _sigmoid(ga_ref[0].astype(f32)) * ya + _sigmoid(gb_ref[0].astype(f32)) * yb
    mix = jnp.dot(mixed.astype(MXU_DTYPE), wo_ref[...], preferred_element_type=f32)
    out_ref[0] = _layer_norm(alpha * x_ref[0] + mix, lg_ref[...], lb_ref[...])


def _merge(g, o, ga, gb, x, wa, wb, wo, lg, lb, *, tile, alpha):
    bsz, t_len, d = x.shape
    row_spec = pl.BlockSpec((1, tile, d), lambda b, t: (b, t, 0))
    return pl.pallas_call(
        functools.partial(_merge_kernel, alpha=alpha), grid=(bsz, t_len // tile),
        in_specs=[row_spec] * 5 + [_resident(a.shape) for a in (wa, wb, wo, lg, lb)],
        out_specs=row_spec, out_shape=jax.ShapeDtypeStruct((bsz, t_len, d), f32),
        compiler_params=_cparams(2), name="merge",
    )(g, o, ga, gb, x, wa, wb, wo, lg, lb)


def _ffn_kernel(x_ref, buf_ref, wup_ref, cw_ref, cb_ref, wdn_ref, lg_ref, lb_ref, out_ref, nbuf_ref,
                carry_ref, act_ref, *, alpha, tstride, ck):
    @pl.when(pl.program_id(1) == 0)
    def _():
        carry_ref[...] = buf_ref[0]

    x = x_ref[0]
    xb = x.astype(MXU_DTYPE)
    d_ff = wdn_ref.shape[0]
    for c in range(d_ff // ck):
        halves = []
        for off in (c * ck, d_ff + c * ck):
            u = jnp.dot(xb, wup_ref[:, off:off + ck], preferred_element_type=f32)
            uc, ncarry = _causal_conv(u, carry_ref[:, off:off + ck], cw_ref[:, off:off + ck],
                                      cb_ref[:, off:off + ck], tstride)
            carry_ref[:, off:off + ck] = ncarry
            halves.append(uc)
        act_ref[:, c * ck:(c + 1) * ck] = (_gelu(halves[0]) * halves[1]).astype(act_ref.dtype)
    down = jnp.dot(act_ref[...], wdn_ref[...], preferred_element_type=f32)
    out_ref[0] = _layer_norm(alpha * x + down, lg_ref[...], lb_ref[...])
    nbuf_ref[0] = carry_ref[...]


def _ffn(x, buf, wup, cw, cb, wdn, lg, lb, *, tile, alpha, tstride):
    bsz, t_len, d = x.shape
    cr, f2 = buf.shape[1], buf.shape[2]
    row_spec = pl.BlockSpec((1, tile, d), lambda b, t: (b, t, 0))
    buf_spec = pl.BlockSpec((1, cr, f2), lambda b, t: (b, 0, 0))
    return pl.pallas_call(
        functools.partial(_ffn_kernel, alpha=alpha, tstride=tstride, ck=512),
        grid=(bsz, t_len // tile),
        in_specs=[row_spec, buf_spec] + [_resident(a.shape) for a in (wup, cw, cb, wdn, lg, lb)],
        out_specs=[row_spec, buf_spec],
        out_shape=[jax.ShapeDtypeStruct((bsz, t_len, d), f32),
                   jax.ShapeDtypeStruct((bsz, cr, f2), f32)],
        scratch_shapes=[pltpu.VMEM((cr, f2), f32), pltpu.VMEM((tile, wdn.shape[0]), MXU_DTYPE)],
        compiler_params=_cparams(2), name="ffn",
    )(x, buf, wup, cw, cb, wdn, lg, lb)


def _layer_weights(l, w_in, rnn_conv_w, rnn_conv_b, lru_wa, lru_ba, lru_wx, lru_bx, lru_lambda,
                   w_branch_a, w_branch_b, w_out, ln1_g, ln1_b, ffn_w_up, ffn_conv_w, ffn_conv_b,
                   ffn_w_down, ln2_g, ln2_b):
    d = w_in.shape[1]
    dh = d // N_HEADS
    di = (w_in.shape[2] - 4 * d - (N_HEADS + 2 * N_KV_HEADS) * dh - N_IDX_HEADS) // (N_IDX_HEADS + 1)
    sizes = (d, d, N_HEADS * dh, N_KV_HEADS * dh, N_KV_HEADS * dh, N_IDX_HEADS * di, di, N_IDX_HEADS, d, d)
    names = ("xr", "gr", "q", "k", "v", "qi", "ki", "wi", "ga", "gb")
    cols, off = {}, 0
    for name, size in zip(names, sizes):
        cols[name] = w_in[l][:, off:off + size].astype(MXU_DTYPE)
        off += size
    row = lambda v: v[l][None, :]
    return dict(
        cols=cols, dh=dh, di=di,
        cw=rnn_conv_w[l], cb=row(rnn_conv_b),
        wax=jnp.concatenate([lru_wa[l], lru_wx[l]], axis=2).astype(MXU_DTYPE),
        ba=row(lru_ba), bx=row(lru_bx), lam=row(lru_lambda),
        wa=w_branch_a[l].astype(MXU_DTYPE), wb=w_branch_b[l].astype(MXU_DTYPE),
        wo=w_out[l].astype(MXU_DTYPE), lg1=row(ln1_g), lb1=row(ln1_b),
        wup=ffn_w_up[l].astype(MXU_DTYPE), fcw=ffn_conv_w[l], fcb=row(ffn_conv_b),
        wdn=ffn_w_down[l].astype(MXU_DTYPE), lg2=row(ln2_g), lb2=row(ln2_b))


def _prompt_layer(x, w, alpha):
    bsz, t_len, d = x.shape
    dh, di, cols = w["dh"], w["di"], w["cols"]
    kc = ATTN_KC
    t_pad = -(-t_len // kc) * kc
    ptile = kc * _largest_tile(t_pad // kc, 3, 1)
    rtile = _largest_tile(t_len, 1024, 16)
    topk = min(TOPK_MAX, t_len // 4)
    one = lambda dtype: [(dtype, False, 1)]
    std = [(cols["xr"], 1.0, one(f32)), (cols["gr"], 1.0, one(ACT_DTYPE)),
           (cols["k"], 1.0, [(f32, False, N_KV_HEADS), (MXU_DTYPE, True, 1)]),
           (cols["v"], 1.0, [(f32, False, N_KV_HEADS)]),
           (cols["ki"], 1.0, [(f32, False, 1), (MXU_DTYPE, True, 1)]),
           (cols["ga"], 1.0, one(ACT_DTYPE)), (cols["gb"], 1.0, one(ACT_DTYPE))]
    tr = [(cols["q"].T, dh ** -0.5 * LOG2E, MXU_DTYPE, False), (cols["qi"].T, di ** -0.5, MXU_DTYPE, False),
          (cols["wi"].T, N_IDX_HEADS ** -0.5, f32, False), (cols["v"].T, 1.0, MXU_DTYPE, True)]
    xr, gr, k, kb, v, ki, kib, ga, gb, qT, qiT, wiT, vT4 = _project(
        x, std, tr, tile=ptile, t_pad=t_pad, kc=kc)

    c_rnn, w_rnn = xr.shape[2], w["cw"].shape[0]
    g, h_last, nbuf = _rglru(
        xr, gr, jnp.zeros((bsz, 1, c_rnn), f32), jnp.zeros((bsz, SUBLANES, c_rnn), f32),
        w["cw"], w["cb"], w["wax"], w["ba"], w["bx"], w["lam"], tile=rtile, tstride=1)
    o = _attn_prompt(qT, qiT, wiT, kb, vT4, kib, t_len=t_len, topk=topk)
    x1 = _merge(g, o, ga, gb, x, w["wa"], w["wb"], w["wo"], w["lg1"], w["lb1"], tile=rtile, alpha=alpha)
    f2, w_ffn = w["fcw"].shape[1], w["fcw"].shape[0]
    x2, fbuf = _ffn(x1, jnp.zeros((bsz, SUBLANES, f2), f32), w["wup"], w["fcw"], w["fcb"], w["wdn"],
                    w["lg2"], w["lb2"], tile=_largest_tile(t_len, 1024, 16), alpha=alpha, tstride=1)
    state = (k.reshape(bsz, t_len, N_KV_HEADS, dh), v.reshape(bsz, t_len, N_KV_HEADS, dh), ki,
             h_last[:, 0, :], nbuf[:, SUBLANES - (w_rnn - 1):, :], fbuf[:, SUBLANES - (w_ffn - 1):, :])
    return x2, state


def _to_time_major(a):
    a = jnp.swapaxes(a, 0, 1)
    return a.reshape((1, a.shape[0] * a.shape[1]) + a.shape[2:])


def _from_time_major(a, bsz):
    a = a.reshape((a.shape[1] // bsz, bsz) + a.shape[2:])
    return jnp.swapaxes(a, 0, 1)


def _sample_layer(x, w, alpha, layer, page_table, cache_ikt, cache_k3, cache_v3, h0, lru_buf, ffn_buf, bsz):
    rows, d = x.shape[1], x.shape[2]
    n_new = rows // bsz
    dh, di, cols = w["dh"], w["di"], w["cols"]
    past = page_table.shape[1] * PAGE_SIZE
    topk = min(TOPK_MAX, (past + n_new) // 4)
    one = lambda dtype: [(dtype, False, 1)]
    std = [(cols["xr"], 1.0, one(f32)), (cols["gr"], 1.0, one(ACT_DTYPE)),
           (cols["q"], dh ** -0.5 * LOG2E, one(MXU_DTYPE)), (cols["k"], 1.0, one(f32)),
           (cols["v"], 1.0, one(f32)), (cols["qi"], di ** -0.5, one(MXU_DTYPE)),
           (cols["ki"], 1.0, one(f32)), (cols["wi"], N_IDX_HEADS ** -0.5, one(f32)),
           (cols["ga"], 1.0, one(ACT_DTYPE)), (cols["gb"], 1.0, one(ACT_DTYPE))]
    xr, gr, q, k, v, qi, ki, wi, ga, gb = _project(x, std, [], tile=rows, t_pad=rows, kc=ATTN_KC)

    g, h_last, nbuf = _rglru(xr, gr, h0[None], _to_time_major(lru_buf), w["cw"], w["cb"], w["wax"],
                             w["ba"], w["bx"], w["lam"], tile=rows, tstride=bsz)

    def heads_major(a, n_h):
        a = _from_time_major(a, bsz).reshape(bsz, n_new, n_h, -1)
        return jnp.swapaxes(a, 1, 2).reshape(bsz, n_h * n_new, -1)

    def pad_new(a):
        a = _from_time_major(a, bsz).astype(MXU_DTYPE)
        return jnp.pad(a, ((0, 0), (0, LANES - n_new), (0, 0)))

    kin_t = jnp.swapaxes(pad_new(ki), 1, 2)
    o = _attn_sample(layer, page_table, cache_ikt, cache_k3, cache_v3,
                     heads_major(qi, N_IDX_HEADS), heads_major(wi, N_IDX_HEADS), kin_t,
                     heads_major(q, N_HEADS), pad_new(k), pad_new(v), topk=topk)
    o = jnp.swapaxes(o.reshape(bsz, N_HEADS, n_new, dh), 1, 2).reshape(bsz, n_new, N_HEADS * dh)
    o = _to_time_major(o)

    x1 = _merge(g, o, ga, gb, x, w["wa"], w["wb"], w["wo"], w["lg1"], w["lb1"], tile=rows, alpha=alpha)
    x2, fbuf = _ffn(x1, _to_time_major(ffn_buf), w["wup"], w["fcw"], w["fcb"], w["wdn"], w["lg2"],
                    w["lb2"], tile=rows, alpha=alpha, tstride=bsz)
    state = (_from_time_major(k, bsz).reshape(bsz, n_new, N_KV_HEADS, dh),
             _from_time_major(v, bsz).reshape(bsz, n_new, N_KV_HEADS, dh),
             _from_time_major(ki, bsz), h_last[0], _from_time_major(nbuf, bsz), _from_time_major(fbuf, bsz))
    return x2, state


def kernel(x_prompt, x_sample, cache_k, cache_v, cache_idx_k, state_lru_h, state_lru_conv, state_ffn_conv,
           page_table, meta_tokens, w_in, rnn_conv_w, rnn_conv_b, lru_wa, lru_ba, lru_wx, lru_bx, lru_lambda,
           w_branch_a, w_branch_b, w_out, ln1_g, ln1_b, ffn_w_up, ffn_conv_w, ffn_conv_b, ffn_w_down,
           ln2_g, ln2_b):
    depth = w_in.shape[0]
    alpha = (2.0 * depth) ** 0.25
    bsz, _, d = x_prompt.shape
    dbsz = x_sample.shape[0]
    meta = jnp.broadcast_to(meta_tokens.astype(x_prompt.dtype), (bsz, N_META, d))
    xp = jnp.concatenate([meta, x_prompt], axis=1)
    xs = _to_time_major(x_sample)
    cache_k3 = cache_k.reshape(cache_k.shape[:2] + (-1, cache_k.shape[4]))
    cache_v3 = cache_v.reshape(cache_v.shape[:2] + (-1, cache_v.shape[4]))
    cache_ikt = jnp.swapaxes(cache_idx_k, 2, 3)
    p_states, s_states = [], []
    for l in range(depth):
        w = _layer_weights(l, w_in, rnn_conv_w, rnn_conv_b, lru_wa, lru_ba, lru_wx, lru_bx, lru_lambda,
                           w_branch_a, w_branch_b, w_out, ln1_g, ln1_b, ffn_w_up, ffn_conv_w, ffn_conv_b,
                           ffn_w_down, ln2_g, ln2_b)
        xp, st = _prompt_layer(xp, w, alpha)
        p_states.append(st)
        xs, st = _sample_layer(xs, w, alpha, l, page_table, cache_ikt, cache_k3, cache_v3,
                               state_lru_h[l], state_lru_conv[l], state_ffn_conv[l], dbsz)
        s_states.append(st)
    stack = lambda states, n: jnp.stack([st[n] for st in states])
    return ((xp[:, N_META:], _from_time_major(xs, dbsz))
            + tuple(stack(p_states, n) for n in range(6))
            + tuple(stack(s_states, n) for n in range(6)))
```

```python
import functools

import jax
import jax.numpy as jnp
from jax import lax
from jax.experimental import pallas as pl
from jax.experimental.pallas import tpu as pltpu

f32 = jnp.float32

N_META = 16
N_RNN_BLOCKS = 8
LRU_C = 8.0
N_HEADS = 8
N_KV_HEADS = 4
KV_GROUP = N_HEADS // N_KV_HEADS
N_IDX_HEADS = 8
TOPK_MAX = 256
PAGE_SIZE = 128
LN_EPS = 1e-5

NEG = -1e30
BIG = 1e30
NO_TIE_LIMIT = 1e9
LOG2E = 1.4426950408889634
F32_TINY = float(jnp.finfo(jnp.float32).tiny)

MXU_DTYPE = jnp.bfloat16
ACT_DTYPE = jnp.bfloat16

V7X_VMEM_BYTES = 64 * 1024 * 1024
VMEM_LIMIT = V7X_VMEM_BYTES * 7 // 8
LANES = 128
SUBLANES = 8

ATTN_QB = 256
ATTN_KC = 256
ATTN_KC2 = 1024
COUNT_CHAINS = 8
COUNT_ROWS = COUNT_CHAINS * SUBLANES
PIPE_SLOTS = 2
DENOM_ROWS = 16
DENOM_FLOOR = 2.0 ** -90
MAX_BISECT = 4096
HI_MARGIN = 2.0 ** -20
FIRST_PROBE_FRACTION = 0.75
MAX_TIE_BISECT = 64
SAMPLE_PAGES_PER_STEP = 16


def _cparams(n_grid):
    return pltpu.CompilerParams(
        dimension_semantics=("arbitrary",) * n_grid, vmem_limit_bytes=VMEM_LIMIT)


def _resident(shape):
    nd = len(shape)
    return pl.BlockSpec(tuple(shape), lambda *_: (0,) * nd, pipeline_mode=pl.Buffered(1))


def _largest_tile(n, limit, multiple):
    best = None
    for d in range(multiple, min(n, limit) + 1, multiple):
        if n % d == 0:
            best = d
    assert best is not None, (n, limit, multiple)
    return best


def _gelu(x):
    return jax.nn.gelu(x, approximate=True)


def _sigmoid(x):
    return 0.5 * jnp.tanh(0.5 * x) + 0.5


def _softplus(x):
    return jnp.maximum(x, 0.0) + jnp.log1p(jnp.exp(-jnp.abs(x)))


def _layer_norm(y, g, b):
    mu = jnp.mean(y, axis=-1, keepdims=True)
    d = y - mu
    var = jnp.mean(d * d, axis=-1, keepdims=True)
    return d * lax.rsqrt(var + LN_EPS) * g + b


def _causal_conv(u, carry, w, b, tstride):
    width = w.shape[0]
    rows = u.shape[0]
    y = b + w[width - 1:width, :] * u
    if tstride % SUBLANES == 0:
        ext = jnp.concatenate([carry, u], axis=0)
        for j in range(width - 1):
            y = y + w[j:j + 1, :] * ext[j * tstride:j * tstride + rows, :]
        return y, ext[rows:, :]
    assert tstride == 1 and carry.shape[0] == SUBLANES
    row = lax.broadcasted_iota(jnp.int32, (SUBLANES, u.shape[1]), 0)
    for s in range(1, width):
        rolled = pltpu.roll(u, s, axis=0)
        head = jnp.where(row < s, pltpu.roll(carry, s, axis=0), rolled[0:SUBLANES, :])
        shifted = jnp.concatenate([head, rolled[SUBLANES:, :]], axis=0)
        y = y + w[width - 1 - s:width - s, :] * shifted
    return y, u[rows - SUBLANES:, :]


def _proj_kernel(x_ref, *refs, n_outs, n_tr, scales, t_valid, tile, kc):
    n_w = len(n_outs) + n_tr
    w_refs = refs[:n_w]
    o_refs = list(refs[n_w:])
    t0 = pl.program_id(1) * tile
    row = lax.broadcasted_iota(jnp.int32, (tile, 1), 0) + t0
    xb = jnp.where(row < t_valid, x_ref[0], 0.0).astype(MXU_DTYPE)
    for n, n_out in enumerate(n_outs):
        y = jnp.dot(xb, w_refs[n][...], preferred_element_type=f32)
        if scales[n] != 1.0:
            y = y * scales[n]
        for _ in range(n_out):
            o = o_refs.pop(0)
            heads = y.shape[1] // o.shape[2]
            if heads == 1:
                o[0] = y.astype(o.dtype)
            else:
                for g in range(heads):
                    o[0, pl.ds(g, tile, stride=heads), :] = (
                        y[:, g * o.shape[2]:(g + 1) * o.shape[2]].astype(o.dtype))
    for n in range(len(n_outs), n_w):
        y = lax.dot_general(w_refs[n][...], xb, (((1,), (1,)), ((), ())),
                            preferred_element_type=f32)
        if scales[n] != 1.0:
            y = y * scales[n]
        o = o_refs.pop(0)
        if len(o.shape) == 4:
            for cc in range(tile // kc):
                o[0, cc] = y[:, cc * kc:(cc + 1) * kc].astype(o.dtype)
        else:
            o[0] = y.astype(o.dtype)


def _project(x, std, tr, *, tile, t_pad, kc):
    bsz, t_len, d = x.shape
    n_t = t_pad // tile
    in_specs = [pl.BlockSpec((1, tile, d), lambda b, t: (b, t, 0))]
    out_shapes, out_specs, scales = [], [], []
    for w, scale, outs in std:
        in_specs.append(_resident(w.shape))
        n = w.shape[1]
        for dtype, padded, heads in outs:
            rows = t_pad if padded else t_len
            out_shapes.append(jax.ShapeDtypeStruct((bsz, rows * heads, n // heads), dtype))
            out_specs.append(pl.BlockSpec((1, tile * heads, n // heads), lambda b, t: (b, t, 0)))
        scales.append(scale)
    for w, scale, dtype, chunked in tr:
        in_specs.append(_resident(w.shape))
        n = w.shape[0]
        if chunked:
            out_shapes.append(jax.ShapeDtypeStruct((bsz, t_pad // kc, n, kc), dtype))
            out_specs.append(pl.BlockSpec((1, tile // kc, n, kc), lambda b, t: (b, t, 0, 0)))
        else:
            out_shapes.append(jax.ShapeDtypeStruct((bsz, n, t_pad), dtype))
            out_specs.append(pl.BlockSpec((1, n, tile), lambda b, t: (b, 0, t)))
        scales.append(scale)
    kern = functools.partial(_proj_kernel, n_outs=tuple(len(outs) for _, _, outs in std), n_tr=len(tr),
                             scales=tuple(scales), t_valid=t_len, tile=tile, kc=kc)
    return pl.pallas_call(
        kern, grid=(bsz, n_t), in_specs=in_specs, out_specs=out_specs, out_shape=out_shapes,
        compiler_params=_cparams(2), name="proj",
    )(x, *[w for w, *_ in std], *[w for w, *_ in tr])


def _lru_gates(xc, wax_ref, ba, bx, lam):
    blk = wax_ref.shape[1]
    xcb = xc.astype(MXU_DTYPE)
    rs, gs = [], []
    for n in range(wax_ref.shape[0]):
        y = jnp.dot(xcb[:, n * blk:(n + 1) * blk], wax_ref[n], preferred_element_type=f32)
        rs.append(y[:, :blk])
        gs.append(y[:, blk:])
    r = _sigmoid(jnp.concatenate(rs, axis=1) + ba)
    i = _sigmoid(jnp.concatenate(gs, axis=1) + bx)
    log_a = (-LRU_C) * r * _softplus(-lam)
    a = jnp.exp(log_a)
    th = jnp.tanh(log_a)
    one_minus_a2 = -2.0 * th / (1.0 - th)
    root = one_minus_a2 * lax.rsqrt(jnp.maximum(one_minus_a2, F32_TINY))
    return a, root * (i * xc)


def _scan8(a8, b8, h_prev):
    row = lax.broadcasted_iota(jnp.int32, a8.shape, 0)
    a, b = a8, b8
    for s in (1, 2, 4):
        ar = pltpu.roll(a, s, axis=0)
        br = pltpu.roll(b, s, axis=0)
        m = row >= s
        b = jnp.where(m, a * br + b, b)
        a = jnp.where(m, a * ar, a)
    return a * h_prev + b


def _rglru_kernel(xr_ref, gr_ref, h0_ref, buf_ref, cw_ref, cb_ref, wax_ref, ba_ref, bx_ref, lam_ref,
                  g_ref, hl_ref, nbuf_ref, a_s, b_s, hc_s, cc_s, *, tstride):
    @pl.when(pl.program_id(1) == 0)
    def _():
        cc_s[...] = buf_ref[0]
        hc_s[...] = jnp.broadcast_to(h0_ref[0], hc_s.shape)

    xr = xr_ref[0]
    rows = xr.shape[0]
    xc, ncarry = _causal_conv(xr, cc_s[...], cw_ref[...], cb_ref[...], tstride)
    cc_s[...] = ncarry
    nbuf_ref[0] = ncarry
    a, inp = _lru_gates(xc, wax_ref, ba_ref[...], bx_ref[...], lam_ref[...])
    if tstride == 1:
        a_s[...] = a
        b_s[...] = inp

        def body(k, h_prev):
            r0 = pl.multiple_of(k * SUBLANES, SUBLANES)
            h = _scan8(a_s[pl.ds(r0, SUBLANES), :], b_s[pl.ds(r0, SUBLANES), :], h_prev)
            b_s[pl.ds(r0, SUBLANES), :] = h
            return jnp.broadcast_to(h[SUBLANES - 1:SUBLANES, :], h.shape)

        h_last = lax.fori_loop(0, rows // SUBLANES, body, hc_s[...])
        hc_s[...] = h_last
        hl_ref[0] = h_last[0:1, :]
        hs = b_s[...]
    else:
        h = hc_s[...]
        pieces = []
        for s in range(rows // tstride):
            h = a[s * tstride:(s + 1) * tstride, :] * h + inp[s * tstride:(s + 1) * tstride, :]
            pieces.append(h)
        hc_s[...] = h
        hl_ref[0] = h
        hs = jnp.concatenate(pieces, axis=0)
    g_ref[0] = (hs * _gelu(gr_ref[0].astype(f32))).astype(g_ref.dtype)


def _rglru(xr, gr, h0, buf, cw, cb, wax, ba, bx, lam, *, tile, tstride):
    bsz, t_len, c = xr.shape
    hr, cr = h0.shape[1], buf.shape[1]
    hrows = SUBLANES if tstride == 1 else hr
    row_spec = pl.BlockSpec((1, tile, c), lambda b, t: (b, t, 0))
    scratch = [pltpu.VMEM((tile, c), f32), pltpu.VMEM((tile, c), f32),
               pltpu.VMEM((hrows, c), f32), pltpu.VMEM((cr, c), f32)]
    return pl.pallas_call(
        functools.partial(_rglru_kernel, tstride=tstride),
        grid=(bsz, t_len // tile),
        in_specs=[row_spec, row_spec,
                  pl.BlockSpec((1, hr, c), lambda b, t: (b, 0, 0)),
                  pl.BlockSpec((1, cr, c), lambda b, t: (b, 0, 0)),
                  _resident(cw.shape), _resident(cb.shape), _resident(wax.shape),
                  _resident(ba.shape), _resident(bx.shape), _resident(lam.shape)],
        out_specs=[row_spec,
                   pl.BlockSpec((1, hr, c), lambda b, t: (b, 0, 0)),
                   pl.BlockSpec((1, cr, c), lambda b, t: (b, 0, 0))],
        out_shape=[jax.ShapeDtypeStruct((bsz, t_len, c), ACT_DTYPE),
                   jax.ShapeDtypeStruct((bsz, hr, c), f32),
                   jax.ShapeDtypeStruct((bsz, cr, c), f32)],
        scratch_shapes=scratch, compiler_params=_cparams(2), name="rglru",
    )(xr, gr, h0, buf, cw, cb, wax, ba, bx, lam)


def _select_threshold(count_ge, count_tie, lo0, hi0, n_allowed, n_keys, topk):
    kf = float(topk)

    def n_active(done):
        return jnp.sum(jnp.where(done, 0.0, 1.0))

    lo = lo0
    hi_start = hi0 + (jnp.abs(hi0) * HI_MARGIN + F32_TINY)
    cnt_lo = n_allowed
    cnt_hi = jnp.zeros_like(lo0)
    stalled = jnp.zeros_like(lo0)
    frac0 = jnp.where(n_allowed > 4.0 * kf, FIRST_PROBE_FRACTION, 0.5)

    def bis_cond(st):
        return jnp.logical_and(st[-1] > 0.0, st[-2] < MAX_BISECT)

    def bis_body(st):
        lo, hi, cnt_lo, cnt_hi, stalled, frac, it, _ = st
        live = jnp.logical_and(cnt_lo > kf, stalled <= 0.0)
        n_live = jnp.sum(jnp.where(live, 1.0, 0.0))
        mid = lo + (hi - lo) * frac
        mid = jnp.where(jnp.logical_and(lo < 0.0, hi > 0.0), 0.0, mid)
        mid = jnp.where(jnp.logical_and(lo == 0.0, hi > F32_TINY), F32_TINY, mid)
        mid = jnp.where(jnp.logical_and(hi == 0.0, lo < -F32_TINY), -F32_TINY, mid)
        c = count_ge(mid)
        up = jnp.logical_and(live, c >= kf)
        dn = jnp.logical_and(live, c < kf)
        stall_now = jnp.logical_and(live, jnp.logical_or(mid <= lo, mid >= hi))
        lo = jnp.where(up, mid, lo)
        cnt_lo = jnp.where(up, c, cnt_lo)
        hi = jnp.where(dn, mid, hi)
        cnt_hi = jnp.where(dn, c, cnt_hi)
        stalled = jnp.where(stall_now, 1.0, stalled)
        return lo, hi, cnt_lo, cnt_hi, stalled, jnp.full_like(frac, 0.5), it + 1, n_live

    st = (lo, hi_start, cnt_lo, cnt_hi, stalled, frac0, jnp.int32(0), n_active(cnt_lo <= kf))
    lo, _, cnt_lo, cnt_hi, _, _, _, _ = lax.while_loop(bis_cond, bis_body, st)
    t = lo

    need = cnt_lo > kf
    want = kf - cnt_hi

    def tie_cond(st):
        return jnp.logical_and(st[-1] > 0.0, st[-2] < MAX_TIE_BISECT)

    def tie_body(st):
        jl, jh, jf, found, it, _ = st
        live = found <= 0.0
        jm = jnp.floor((jl + jh) * 0.5)
        c = count_tie(t, jm)
        hit = jnp.logical_and(live, c == want)
        jf = jnp.where(hit, jm, jf)
        found = jnp.where(hit, 1.0, found)
        jl = jnp.where(jnp.logical_and(live, c < want), jm, jl)
        jh = jnp.where(jnp.logical_and(live, c > want), jm, jh)
        return jl, jh, jf, found, it + 1, n_active(found > 0.0)

    found0 = jnp.where(need, 0.0, 1.0)
    st = (jnp.zeros_like(t), jnp.zeros_like(t) + n_keys, jnp.full_like(t, NO_TIE_LIMIT), found0,
          jnp.int32(0), n_active(found0 > 0.0))
    _, _, jf, _, _, _ = lax.while_loop(tie_cond, tie_body, st)
    return t, jnp.where(need, jf, NO_TIE_LIMIT)


def _attn_prompt_kernel(qT_ref, qiT_ref, wiT_ref, kb_ref, vT_ref, kib_ref, o_ref,
                        s_ref, q2_ref, acc_ref, m_ref, sa_ref, pb_ref, al_ref,
                        *, qb, kc, kc2, topk):
    i = pl.program_id(1)
    n_ih = wiT_ref.shape[1]
    di = qiT_ref.shape[1] // n_ih
    n_grp, dh = q2_ref.shape[0], q2_ref.shape[1]
    grp = q2_ref.shape[2] // qb
    c_last = (i * qb) // kc
    n_ch = c_last + 1
    n_ch2 = (n_ch * kc + kc2 - 1) // kc2

    qi_all = jnp.concatenate([qiT_ref[0, h * di:(h + 1) * di, :] for h in range(n_ih)], axis=1)
    wi = wiT_ref[0]
    kio = lax.broadcasted_iota(jnp.int32, (kc, qb), 0)
    qio = lax.broadcasted_iota(jnp.int32, (kc, qb), 1)

    def chunk_scores(c):
        k0 = pl.multiple_of(c * kc, kc)
        rel = jnp.dot(kib_ref[0, pl.ds(k0, kc), :], qi_all, preferred_element_type=f32)
        sc = wi[0:1, :] * jnp.maximum(rel[:, 0:qb], 0.0)
        for h in range(1, n_ih):
            sc = sc + wi[h:h + 1, :] * jnp.maximum(rel[:, h * qb:(h + 1) * qb], 0.0)
        return k0, sc

    def fold(x, op):
        return op(x.reshape(kc // SUBLANES, SUBLANES, qb), axis=0)

    def p1_body(k, carry):
        vmax, vmin = carry
        for c in (2 * k, jnp.minimum(2 * k + 1, c_last - 1)):
            k0, sc = chunk_scores(c)
            s_ref[pl.ds(k0, kc), :] = sc
            vmax = jnp.maximum(vmax, fold(sc, jnp.max))
            vmin = jnp.minimum(vmin, fold(sc, jnp.min))
        return vmax, vmin

    vmax, vmin = lax.fori_loop(
        0, (c_last + 1) // 2, p1_body,
        (jnp.full((SUBLANES, qb), NEG, f32), jnp.full((SUBLANES, qb), BIG, f32)))
    k0, sc = chunk_scores(c_last)
    allowed = (kio + k0) <= (qio + i * qb)
    s_ref[pl.ds(k0, kc), :] = jnp.where(allowed, sc, NEG)
    vmax = jnp.maximum(vmax, fold(jnp.where(allowed, sc, NEG), jnp.max))
    vmin = jnp.minimum(vmin, fold(jnp.where(allowed, sc, BIG), jnp.min))

    def fill_body(c, carry):
        s_ref[pl.ds(pl.multiple_of(c * kc, kc), kc), :] = jnp.full((kc, qb), NEG, f32)
        return carry

    lax.fori_loop(n_ch, n_ch2 * (kc2 // kc), fill_body, 0)

    def count_slabs(hit_fn):
        def body(c, acc):
            r0 = pl.multiple_of(c * kc2, kc2)
            for s in range(kc2 // COUNT_ROWS):
                r = pl.multiple_of(r0 + s * COUNT_ROWS, COUNT_ROWS)
                acc = jnp.where(hit_fn(s_ref[pl.ds(r, COUNT_ROWS), :], r), acc + 1.0, acc)
            return acc
        acc = lax.fori_loop(0, n_ch2, body, jnp.zeros((COUNT_ROWS, qb), f32))
        return jnp.sum(acc, axis=0, keepdims=True)

    def count_ge(x):
        xb = jnp.broadcast_to(x, (COUNT_ROWS, qb))
        return count_slabs(lambda blk, r: blk >= xb)

    def count_tie(t, j):
        tb = jnp.broadcast_to(t, (COUNT_ROWS, qb))
        row = lax.broadcasted_iota(jnp.int32, (COUNT_ROWS, qb), 0).astype(f32)

        def hit(blk, r):
            return jnp.logical_and(blk == tb, row < j - r.astype(f32))
        return count_slabs(hit)

    n_allowed = (lax.broadcasted_iota(jnp.int32, (1, qb), 1) + (i * qb + 1)).astype(f32)
    t, jsel = _select_threshold(
        count_ge, count_tie,
        jnp.min(vmin, axis=0, keepdims=True), jnp.max(vmax, axis=0, keepdims=True),
        n_allowed, (n_ch * kc).astype(f32), topk)

    for g in range(n_grp):
        q2_ref[g] = jnp.concatenate(
            [qT_ref[0, (g * grp + j) * dh:(g * grp + j + 1) * dh, :] for j in range(grp)], axis=1)
    ones_rows = jnp.ones((DENOM_ROWS, kc), pb_ref.dtype)

    def attend(exact):
        m_ref[...] = jnp.full(m_ref.shape, NEG, f32)
        acc_ref[...] = jnp.zeros(acc_ref.shape, f32)

        def qk_stage(c, slot, g):
            k0 = pl.multiple_of(jnp.minimum(c, n_ch - 1) * kc, kc)
            sa_ref[slot, g] = jnp.dot(kb_ref[0, pl.ds(k0, kc), g * dh:(g + 1) * dh], q2_ref[g],
                                      preferred_element_type=f32)

        def selection_mask(c):
            valid = c < n_ch
            k0 = pl.multiple_of(jnp.minimum(c, n_ch - 1) * kc, kc)
            blk = s_ref[pl.ds(k0, kc), :]
            kidx = (kio + k0).astype(f32)
            t_c = jnp.where(valid, t, BIG)
            j_c = jnp.where(valid, jsel, -1.0)
            sel = jnp.logical_or(blk > t_c, jnp.logical_and(blk == t_c, kidx < j_c))
            if exact:
                return jnp.concatenate([jnp.where(sel, 0.0, NEG)] * grp, axis=1)
            return jnp.concatenate([jnp.where(sel, 1.0, 0.0).astype(pb_ref.dtype)] * grp, axis=1)

        def softmax_stage(mask, slot, g):
            st = sa_ref[slot, g]
            if exact:
                st = st + mask
            m_old = m_ref[g]
            m_new = jnp.maximum(m_old, jnp.max(st, axis=0, keepdims=True))
            p = jnp.exp2(st - m_new).astype(pb_ref.dtype)
            pb_ref[slot, g] = p if exact else p * mask
            al_ref[slot, g] = jnp.exp2(m_old - m_new)
            m_ref[g] = m_new

        def pv_stage(c, slot, g):
            cc = jnp.clip(c, 0, n_ch - 1)
            v_aug = jnp.concatenate([vT_ref[0, cc, g * dh:(g + 1) * dh, :], ones_rows], axis=0)
            pv = jnp.dot(v_aug, pb_ref[slot, g], preferred_element_type=f32)
            acc_ref[g] = al_ref[slot, g] * acc_ref[g] + pv

        lag = PIPE_SLOTS // 2
        for c in range(lag):
            for g in range(n_grp):
                qk_stage(c, c, g)
        for slot in range(lag, PIPE_SLOTS):
            pb_ref[slot] = jnp.zeros(pb_ref.shape[1:], pb_ref.dtype)
            al_ref[slot] = jnp.ones(al_ref.shape[1:], f32)

        def p3_body(k, carry):
            for slot in range(PIPE_SLOTS):
                c = PIPE_SLOTS * k + slot
                mask = selection_mask(c)
                for g in range(n_grp):
                    pv_stage(c - lag, (slot + lag) % PIPE_SLOTS, g)
                    qk_stage(c + lag, (slot + lag) % PIPE_SLOTS, g)
                    softmax_stage(mask, slot, g)
            return carry

        lax.fori_loop(0, (n_ch + lag + PIPE_SLOTS - 1) // PIPE_SLOTS, p3_body, 0)

    attend(exact=False)
    denom_min = jnp.min(acc_ref[:, dh:dh + 1, :])

    @pl.when(jnp.logical_not(denom_min >= DENOM_FLOOR))
    def _():
        attend(exact=True)

    for g in range(n_grp):
        o = acc_ref[g, 0:dh, :] / acc_ref[g, dh:dh + 1, :]
        for j in range(grp):
            h = g * grp + j
            o_ref[0, :, h * dh:(h + 1) * dh] = o[:, j * qb:(j + 1) * qb].T.astype(o_ref.dtype)


def _attn_prompt(qT, qiT, wiT, kb, vT4, kib, *, t_len, topk):
    bsz, dq, t_pad = qT.shape
    qb, kc, kc2 = ATTN_QB, ATTN_KC, ATTN_KC2
    dkv = kb.shape[2]
    dh = dq // N_HEADS
    s_rows = -(-t_pad // kc2) * kc2
    kern = functools.partial(_attn_prompt_kernel, qb=qb, kc=kc, kc2=kc2, topk=topk)
    return pl.pallas_call(
        kern, grid=(bsz, -(-t_len // qb)),
        in_specs=[pl.BlockSpec((1, dq, qb), lambda b, i: (b, 0, i)),
                  pl.BlockSpec((1, qiT.shape[1], qb), lambda b, i: (b, 0, i)),
                  pl.BlockSpec((1, wiT.shape[1], qb), lambda b, i: (b, 0, i)),
                  pl.BlockSpec((1, t_pad, dkv), lambda b, i: (b, 0, 0), pipeline_mode=pl.Buffered(1)),
                  pl.BlockSpec((1,) + vT4.shape[1:], lambda b, i: (b, 0, 0, 0),
                               pipeline_mode=pl.Buffered(1)),
                  pl.BlockSpec((1, t_pad, kib.shape[2]), lambda b, i: (b, 0, 0),
                               pipeline_mode=pl.Buffered(1))],
        out_specs=pl.BlockSpec((1, qb, dq), lambda b, i: (b, i, 0)),
        out_shape=jax.ShapeDtypeStruct((bsz, t_pad, dq), ACT_DTYPE),
        scratch_shapes=[pltpu.VMEM((s_rows, qb), f32),
                        pltpu.VMEM((N_KV_HEADS, dh, KV_GROUP * qb), MXU_DTYPE),
                        pltpu.VMEM((N_KV_HEADS, dh + DENOM_ROWS, KV_GROUP * qb), f32),
                        pltpu.VMEM((N_KV_HEADS, 1, KV_GROUP * qb), f32),
                        pltpu.VMEM((PIPE_SLOTS, N_KV_HEADS, kc, KV_GROUP * qb), f32),
                        pltpu.VMEM((PIPE_SLOTS, N_KV_HEADS, kc, KV_GROUP * qb), MXU_DTYPE),
                        pltpu.VMEM((PIPE_SLOTS, N_KV_HEADS, 1, KV_GROUP * qb), f32)],
        compiler_params=_cparams(2), name="attn_prompt",
    )(qT, qiT, wiT, kb, vT4, kib)


def _sattn_select_kernel(pt_ref, qi_ref, wi_ref, kin_ref, *rest, pg, ns, topk, n_new):
    del pt_ref
    pages = rest[:pg]
    s_ref, t_ref, j_ref = rest[pg:]
    st = pl.program_id(1)
    pgk = pg * PAGE_SIZE
    qi = qi_ref[0]
    wi = wi_ref[0]
    n_ih = qi.shape[0] // n_new

    def scores(keys_t):
        rel = jnp.dot(qi, keys_t, preferred_element_type=f32)
        rel = jnp.maximum(rel, 0.0) * wi
        return jnp.sum(rel.reshape(n_ih, n_new, keys_t.shape[1]), axis=0)

    kp = jnp.concatenate([p[...] for p in pages], axis=1).astype(MXU_DTYPE)
    s_ref[0, st] = scores(kp)

    @pl.when(st == ns - 1)
    def _():
        n_pad = kin_ref.shape[2]
        sn = scores(kin_ref[0])
        lane = lax.broadcasted_iota(jnp.int32, (n_new, n_pad), 1)
        qrow = lax.broadcasted_iota(jnp.int32, (n_new, n_pad), 0)
        ok_new = lane <= qrow
        s_ref[0, ns] = jnp.concatenate(
            [jnp.where(ok_new, sn, NEG), jnp.full((n_new, pgk - n_pad), NEG, f32)], axis=1)

        s_all = s_ref[0]
        past = s_all[:-1]
        hi0 = jnp.maximum(jnp.max(jnp.max(past, axis=0), axis=1, keepdims=True),
                          jnp.max(jnp.where(ok_new, sn, NEG), axis=1, keepdims=True))
        lo0 = jnp.minimum(jnp.min(jnp.min(past, axis=0), axis=1, keepdims=True),
                          jnp.min(jnp.where(ok_new, sn, BIG), axis=1, keepdims=True))
        lane_f = lax.broadcasted_iota(jnp.int32, (n_new, pgk), 1).astype(f32)

        def count_chunks(hit_fn):
            acc = jnp.zeros((n_new, pgk), f32)
            for c in range(ns + 1):
                acc = jnp.where(hit_fn(s_ref[0, c], c), acc + 1.0, acc)
            return jnp.sum(acc, axis=1, keepdims=True)

        def count_ge(x):
            return count_chunks(lambda blk, c: blk >= x)

        def count_tie(t, j):
            return count_chunks(lambda blk, c: jnp.logical_and(blk == t, lane_f < j - float(c * pgk)))

        q1 = lax.broadcasted_iota(jnp.int32, (n_new, 1), 0)
        n_allowed = (q1 + (ns * pgk + 1)).astype(f32)
        t, jsel = _select_threshold(count_ge, count_tie, lo0, hi0, n_allowed, float(ns * pgk + n_pad),
                                    topk)
        t_ref[0] = jnp.broadcast_to(t, t_ref.shape[1:])
        j_ref[0] = jnp.broadcast_to(jsel, j_ref.shape[1:])


def _sattn_attend_kernel(pt_ref, q_ref, s_ref, snew_ref, t_ref, j_ref, knew_ref, vnew_ref, *rest,
                         pg, ns, n_new):
    del pt_ref
    kpages, vpages = rest[:pg], rest[pg:2 * pg]
    o_ref, m_s, l_s, acc_s = rest[2 * pg:]
    st = pl.program_id(1)
    pgk = pg * PAGE_SIZE
    q = q_ref[0]
    dh = q.shape[1]
    rows_g = KV_GROUP * n_new
    t = t_ref[0][:, 0:1]
    jsel = j_ref[0][:, 0:1]

    @pl.when(st == 0)
    def _():
        m_s[...] = jnp.full(m_s.shape, NEG, f32)
        l_s[...] = jnp.zeros(l_s.shape, f32)
        acc_s[...] = jnp.zeros(acc_s.shape, f32)

    def update(sc, base, keys_of, vals_of):
        n = sc.shape[1]
        kidx = (lax.broadcasted_iota(jnp.int32, (n_new, n), 1) + base).astype(f32)
        sel = jnp.logical_or(sc > t, jnp.logical_and(sc == t, kidx < jsel))
        bias = jnp.where(sel, 0.0, NEG)
        bias = jnp.concatenate([bias] * KV_GROUP, axis=0)
        s = jnp.concatenate(
            [lax.dot_general(q[g * rows_g:(g + 1) * rows_g, :], keys_of(g), (((1,), (1,)), ((), ())),
                             preferred_element_type=f32) + bias for g in range(N_KV_HEADS)], axis=0)
        m_old = m_s[...]
        m_new = jnp.maximum(m_old, jnp.max(s, axis=1, keepdims=True))
        alpha = jnp.exp2(m_old - m_new)
        p = jnp.exp2(s - m_new)
        l_s[...] = alpha * l_s[...] + jnp.sum(p, axis=1, keepdims=True)
        m_s[...] = m_new
        pb = p.astype(MXU_DTYPE)
        pv = jnp.concatenate(
            [jnp.dot(pb[g * rows_g:(g + 1) * rows_g, :], vals_of(g), preferred_element_type=f32)
             for g in range(N_KV_HEADS)], axis=0)
        acc_s[...] = alpha * acc_s[...] + pv

    def paged(pages):
        def head_rows(g):
            return jnp.concatenate([p[pl.ds(g, PAGE_SIZE, stride=N_KV_HEADS), :] for p in pages],
                                   axis=0).astype(MXU_DTYPE)
        return head_rows

    def fresh(ref):
        return lambda g: ref[0][:, g * dh:(g + 1) * dh]

    update(s_ref[0, 0], st * pgk, paged(kpages), paged(vpages))

    @pl.when(st == ns - 1)
    def _():
        n_pad = knew_ref.shape[1]
        update(snew_ref[0, 0][:, :n_pad], ns * pgk, fresh(knew_ref), fresh(vnew_ref))
        o_ref[0] = (acc_s[...] / l_s[...]).astype(o_ref.dtype)


def _attn_sample(layer, page_table, cache_ikt, cache_k3, cache_v3, qi_hq, wi_hq, kin_t, q_hq, knew, vnew,
                 *, topk):
    bsz, n_pages = page_table.shape
    n_new = qi_hq.shape[1] // N_IDX_HEADS
    pg = _largest_tile(n_pages, SAMPLE_PAGES_PER_STEP, 1)
    ns = n_pages // pg
    pgk = pg * PAGE_SIZE
    di = cache_ikt.shape[2]
    dh = cache_k3.shape[3]
    dkv = knew.shape[2]
    n_pad = knew.shape[1]
    rows = q_hq.shape[1]

    def page_spec(shape, j):
        return pl.BlockSpec((None, None) + shape, lambda b, s, pt: (layer, pt[b, s * pg + j], 0, 0))

    s_all, thr, jsel = pl.pallas_call(
        functools.partial(_sattn_select_kernel, pg=pg, ns=ns, topk=topk, n_new=n_new),
        grid_spec=pltpu.PrefetchScalarGridSpec(
            num_scalar_prefetch=1, grid=(bsz, ns),
            in_specs=[pl.BlockSpec((1,) + qi_hq.shape[1:], lambda b, s, pt: (b, 0, 0)),
                      pl.BlockSpec((1,) + wi_hq.shape[1:], lambda b, s, pt: (b, 0, 0)),
                      pl.BlockSpec((1, di, n_pad), lambda b, s, pt: (b, 0, 0))]
            + [page_spec((di, PAGE_SIZE), j) for j in range(pg)],
            out_specs=[pl.BlockSpec((1, ns + 1, n_new, pgk), lambda b, s, pt: (b, 0, 0, 0)),
                       pl.BlockSpec((1, n_new, LANES), lambda b, s, pt: (b, 0, 0)),
                       pl.BlockSpec((1, n_new, LANES), lambda b, s, pt: (b, 0, 0))]),
        out_shape=[jax.ShapeDtypeStruct((bsz, ns + 1, n_new, pgk), f32),
                   jax.ShapeDtypeStruct((bsz, n_new, LANES), f32),
                   jax.ShapeDtypeStruct((bsz, n_new, LANES), f32)],
        compiler_params=_cparams(2), name="sattn_select",
    )(page_table, qi_hq, wi_hq, kin_t, *([cache_ikt] * pg))

    kv_page = (PAGE_SIZE * N_KV_HEADS, dh)
    return pl.pallas_call(
        functools.partial(_sattn_attend_kernel, pg=pg, ns=ns, n_new=n_new),
        grid_spec=pltpu.PrefetchScalarGridSpec(
            num_scalar_prefetch=1, grid=(bsz, ns),
            in_specs=[pl.BlockSpec((1, rows, dh), lambda b, s, pt: (b, 0, 0)),
                      pl.BlockSpec((1, 1, n_new, pgk), lambda b, s, pt: (b, s, 0, 0)),
                      pl.BlockSpec((1, 1, n_new, pgk), lambda b, s, pt: (b, ns, 0, 0)),
                      pl.BlockSpec((1, n_new, LANES), lambda b, s, pt: (b, 0, 0)),
                      pl.BlockSpec((1, n_new, LANES), lambda b, s, pt: (b, 0, 0)),
                      pl.BlockSpec((1, n_pad, dkv), lambda b, s, pt: (b, 0, 0)),
                      pl.BlockSpec((1, n_pad, dkv), lambda b, s, pt: (b, 0, 0))]
            + [page_spec(kv_page, j) for j in range(pg)] * 2,
            out_specs=pl.BlockSpec((1, rows, dh), lambda b, s, pt: (b, 0, 0)),
            scratch_shapes=[pltpu.VMEM((rows, 1), f32), pltpu.VMEM((rows, 1), f32),
                            pltpu.VMEM((rows, dh), f32)]),
        out_shape=jax.ShapeDtypeStruct((bsz, rows, dh), ACT_DTYPE),
        compiler_params=_cparams(2), name="sattn_attend",
    )(page_table, q_hq, s_all, s_all, thr, jsel, knew, vnew, *([cache_k3] * pg), *([cache_v3] * pg))


def _merge_kernel(g_ref, o_ref, ga_ref, gb_ref, x_ref, wa_ref, wb_ref, wo_ref, lg_ref, lb_ref, out_ref,
                  *, alpha):
    ya = jnp.dot(g_ref[0].astype(MXU_DTYPE), wa_ref[...], preferred_element_type=f32)
    yb = jnp.dot(o_ref[0].astype(MXU_DTYPE), wb_ref[...], preferred_element_type=f32)
    mixed = _sigmoid(ga_ref[0].astype(f32)) * ya + _sigmoid(gb_ref[0].astype(f32)) * yb
    mix = jnp.dot(mixed.astype(MXU_DTYPE), wo_ref[...], preferred_element_type=f32)
    out_ref[0] = _layer_norm(alpha * x_ref[0] + mix, lg_ref[...], lb_ref[...])


def _merge(g, o, ga, gb, x, wa, wb, wo, lg, lb, *, tile, alpha):
    bsz, t_len, d = x.shape
    row_spec = pl.BlockSpec((1, tile, d), lambda b, t: (b, t, 0))
    return pl.pallas_call(
        functools.partial(_merge_kernel, alpha=alpha), grid=(bsz, t_len // tile),
        in_specs=[row_spec] * 5 + [_resident(a.shape) for a in (wa, wb, wo, lg, lb)],
        out_specs=row_spec, out_shape=jax.ShapeDtypeStruct((bsz, t_len, d), f32),
        compiler_params=_cparams(2), name="merge",
    )(g, o, ga, gb, x, wa, wb, wo, lg, lb)


def _ffn_kernel(x_ref, buf_ref, wup_ref, cw_ref, cb_ref, wdn_ref, lg_ref, lb_ref, out_ref, nbuf_ref,
                carry_ref, act_ref, *, alpha, tstride, ck):
    @pl.when(pl.program_id(1) == 0)
    def _():
        carry_ref[...] = buf_ref[0]

    x = x_ref[0]
    xb = x.astype(MXU_DTYPE)
    d_ff = wdn_ref.shape[0]
    for c in range(d_ff // ck):
        halves = []
        for off in (c * ck, d_ff + c * ck):
            u = jnp.dot(xb, wup_ref[:, off:off + ck], preferred_element_type=f32)
            uc, ncarry = _causal_conv(u, carry_ref[:, off:off + ck], cw_ref[:, off:off + ck],
                                      cb_ref[:, off:off + ck], tstride)
            carry_ref[:, off:off + ck] = ncarry
            halves.append(uc)
        act_ref[:, c * ck:(c + 1) * ck] = (_gelu(halves[0]) * halves[1]).astype(act_ref.dtype)
    down = jnp.dot(act_ref[...], wdn_ref[...], preferred_element_type=f32)
    out_ref[0] = _layer_norm(alpha * x + down, lg_ref[...], lb_ref[...])
    nbuf_ref[0] = carry_ref[...]


def _ffn(x, buf, wup, cw, cb, wdn, lg, lb, *, tile, alpha, tstride):
    bsz, t_len, d = x.shape
    cr, f2 = buf.shape[1], buf.shape[2]
    row_spec = pl.BlockSpec((1, tile, d), lambda b, t: (b, t, 0))
    buf_spec = pl.BlockSpec((1, cr, f2), lambda b, t: (b, 0, 0))
    return pl.pallas_call(
        functools.partial(_ffn_kernel, alpha=alpha, tstride=tstride, ck=512),
        grid=(bsz, t_len // tile),
        in_specs=[row_spec, buf_spec] + [_resident(a.shape) for a in (wup, cw, cb, wdn, lg, lb)],
        out_specs=[row_spec, buf_spec],
        out_shape=[jax.ShapeDtypeStruct((bsz, t_len, d), f32),
                   jax.ShapeDtypeStruct((bsz, cr, f2), f32)],
        scratch_shapes=[pltpu.VMEM((cr, f2), f32), pltpu.VMEM((tile, wdn.shape[0]), MXU_DTYPE)],
        compiler_params=_cparams(2), name="ffn",
    )(x, buf, wup, cw, cb, wdn, lg, lb)


def _layer_weights(l, w_in, rnn_conv_w, rnn_conv_b, lru_wa, lru_ba, lru_wx, lru_bx, lru_lambda,
                   w_branch_a, w_branch_b, w_out, ln1_g, ln1_b, ffn_w_up, ffn_conv_w, ffn_conv_b,
                   ffn_w_down, ln2_g, ln2_b):
    d = w_in.shape[1]
    dh = d // N_HEADS
    di = (w_in.shape[2] - 4 * d - (N_HEADS + 2 * N_KV_HEADS) * dh - N_IDX_HEADS) // (N_IDX_HEADS + 1)
    sizes = (d, d, N_HEADS * dh, N_KV_HEADS * dh, N_KV_HEADS * dh, N_IDX_HEADS * di, di, N_IDX_HEADS, d, d)
    names = ("xr", "gr", "q", "k", "v", "qi", "ki", "wi", "ga", "gb")
    cols, off = {}, 0
    for name, size in zip(names, sizes):
        cols[name] = w_in[l][:, off:off + size].astype(MXU_DTYPE)
        off += size
    row = lambda v: v[l][None, :]
    return dict(
        cols=cols, dh=dh, di=di,
        cw=rnn_conv_w[l], cb=row(rnn_conv_b),
        wax=jnp.concatenate([lru_wa[l], lru_wx[l]], axis=2).astype(MXU_DTYPE),
        ba=row(lru_ba), bx=row(lru_bx), lam=row(lru_lambda),
        wa=w_branch_a[l].astype(MXU_DTYPE), wb=w_branch_b[l].astype(MXU_DTYPE),
        wo=w_out[l].astype(MXU_DTYPE), lg1=row(ln1_g), lb1=row(ln1_b),
        wup=ffn_w_up[l].astype(MXU_DTYPE), fcw=ffn_conv_w[l], fcb=row(ffn_conv_b),
        wdn=ffn_w_down[l].astype(MXU_DTYPE), lg2=row(ln2_g), lb2=row(ln2_b))


def _prompt_layer(x, w, alpha):
    bsz, t_len, d = x.shape
    dh, di, cols = w["dh"], w["di"], w["cols"]
    kc = ATTN_KC
    t_pad = -(-t_len // kc) * kc
    ptile = kc * _largest_tile(t_pad // kc, 3, 1)
    rtile = _largest_tile(t_len, 1024, 16)
    topk = min(TOPK_MAX, t_len // 4)
    one = lambda dtype: [(dtype, False, 1)]
    std = [(cols["xr"], 1.0, one(f32)), (cols["gr"], 1.0, one(ACT_DTYPE)),
           (cols["k"], 1.0, [(f32, False, N_KV_HEADS), (MXU_DTYPE, True, 1)]),
           (cols["v"], 1.0, [(f32, False, N_KV_HEADS)]),
           (cols["ki"], 1.0, [(f32, False, 1), (MXU_DTYPE, True, 1)]),
           (cols["ga"], 1.0, one(ACT_DTYPE)), (cols["gb"], 1.0, one(ACT_DTYPE))]
    tr = [(cols["q"].T, dh ** -0.5 * LOG2E, MXU_DTYPE, False), (cols["qi"].T, di ** -0.5, MXU_DTYPE, False),
          (cols["wi"].T, N_IDX_HEADS ** -0.5, f32, False), (cols["v"].T, 1.0, MXU_DTYPE, True)]
    xr, gr, k, kb, v, ki, kib, ga, gb, qT, qiT, wiT, vT4 = _project(
        x, std, tr, tile=ptile, t_pad=t_pad, kc=kc)

    c_rnn, w_rnn = xr.shape[2], w["cw"].shape[0]
    g, h_last, nbuf = _rglru(
        xr, gr, jnp.zeros((bsz, 1, c_rnn), f32), jnp.zeros((bsz, SUBLANES, c_rnn), f32),
        w["cw"], w["cb"], w["wax"], w["ba"], w["bx"], w["lam"], tile=rtile, tstride=1)
    o = _attn_prompt(qT, qiT, wiT, kb, vT4, kib, t_len=t_len, topk=topk)
    x1 = _merge(g, o, ga, gb, x, w["wa"], w["wb"], w["wo"], w["lg1"], w["lb1"], tile=rtile, alpha=alpha)
    f2, w_ffn = w["fcw"].shape[1], w["fcw"].shape[0]
    x2, fbuf = _ffn(x1, jnp.zeros((bsz, SUBLANES, f2), f32), w["wup"], w["fcw"], w["fcb"], w["wdn"],
                    w["lg2"], w["lb2"], tile=_largest_tile(t_len, 1024, 16), alpha=alpha, tstride=1)
    state = (k.reshape(bsz, t_len, N_KV_HEADS, dh), v.reshape(bsz, t_len, N_KV_HEADS, dh), ki,
             h_last[:, 0, :], nbuf[:, SUBLANES - (w_rnn - 1):, :], fbuf[:, SUBLANES - (w_ffn - 1):, :])
    return x2, state


def _to_time_major(a):
    a = jnp.swapaxes(a, 0, 1)
    return a.reshape((1, a.shape[0] * a.shape[1]) + a.shape[2:])


def _from_time_major(a, bsz):
    a = a.reshape((a.shape[1] // bsz, bsz) + a.shape[2:])
    return jnp.swapaxes(a, 0, 1)


def _sample_layer(x, w, alpha, layer, page_table, cache_ikt, cache_k3, cache_v3, h0, lru_buf, ffn_buf, bsz):
    rows, d = x.shape[1], x.shape[2]
    n_new = rows // bsz
    dh, di, cols = w["dh"], w["di"], w["cols"]
    past = page_table.shape[1] * PAGE_SIZE
    topk = min(TOPK_MAX, (past + n_new) // 4)
    one = lambda dtype: [(dtype, False, 1)]
    std = [(cols["xr"], 1.0, one(f32)), (cols["gr"], 1.0, one(ACT_DTYPE)),
           (cols["q"], dh ** -0.5 * LOG2E, one(MXU_DTYPE)), (cols["k"], 1.0, one(f32)),
           (cols["v"], 1.0, one(f32)), (cols["qi"], di ** -0.5, one(MXU_DTYPE)),
           (cols["ki"], 1.0, one(f32)), (cols["wi"], N_IDX_HEADS ** -0.5, one(f32)),
           (cols["ga"], 1.0, one(ACT_DTYPE)), (cols["gb"], 1.0, one(ACT_DTYPE))]
    xr, gr, q, k, v, qi, ki, wi, ga, gb = _project(x, std, [], tile=rows, t_pad=rows, kc=ATTN_KC)

    g, h_last, nbuf = _rglru(xr, gr, h0[None], _to_time_major(lru_buf), w["cw"], w["cb"], w["wax"],
                             w["ba"], w["bx"], w["lam"], tile=rows, tstride=bsz)

    def heads_major(a, n_h):
        a = _from_time_major(a, bsz).reshape(bsz, n_new, n_h, -1)
        return jnp.swapaxes(a, 1, 2).reshape(bsz, n_h * n_new, -1)

    def pad_new(a):
        a = _from_time_major(a, bsz).astype(MXU_DTYPE)
        return jnp.pad(a, ((0, 0), (0, LANES - n_new), (0, 0)))

    kin_t = jnp.swapaxes(pad_new(ki), 1, 2)
    o = _attn_sample(layer, page_table, cache_ikt, cache_k3, cache_v3,
                     heads_major(qi, N_IDX_HEADS), heads_major(wi, N_IDX_HEADS), kin_t,
                     heads_major(q, N_HEADS), pad_new(k), pad_new(v), topk=topk)
    o = jnp.swapaxes(o.reshape(bsz, N_HEADS, n_new, dh), 1, 2).reshape(bsz, n_new, N_HEADS * dh)
    o = _to_time_major(o)

    x1 = _merge(g, o, ga, gb, x, w["wa"], w["wb"], w["wo"], w["lg1"], w["lb1"], tile=rows, alpha=alpha)
    x2, fbuf = _ffn(x1, _to_time_major(ffn_buf), w["wup"], w["fcw"], w["fcb"], w["wdn"], w["lg2"],
                    w["lb2"], tile=rows, alpha=alpha, tstride=bsz)
    state = (_from_time_major(k, bsz).reshape(bsz, n_new, N_KV_HEADS, dh),
             _from_time_major(v, bsz).reshape(bsz, n_new, N_KV_HEADS, dh),
             _from_time_major(ki, bsz), h_last[0], _from_time_major(nbuf, bsz), _from_time_major(fbuf, bsz))
    return x2, state


def kernel(x_prompt, x_sample, cache_k, cache_v, cache_idx_k, state_lru_h, state_lru_conv, state_ffn_conv,
           page_table, meta_tokens, w_in, rnn_conv_w, rnn_conv_b, lru_wa, lru_ba, lru_wx, lru_bx, lru_lambda,
           w_branch_a, w_branch_b, w_out, ln1_g, ln1_b, ffn_w_up, ffn_conv_w, ffn_conv_b, ffn_w_down,
           ln2_g, ln2_b):
    depth = w_in.shape[0]
    alpha = (2.0 * depth) ** 0.25
    bsz, _, d = x_prompt.shape
    dbsz = x_sample.shape[0]
    meta = jnp.broadcast_to(meta_tokens.astype(x_prompt.dtype), (bsz, N_META, d))
    xp = jnp.concatenate([meta, x_prompt], axis=1)
    xs = _to_time_major(x_sample)
    cache_k3 = cache_k.reshape(cache_k.shape[:2] + (-1, cache_k.shape[4]))
    cache_v3 = cache_v.reshape(cache_v.shape[:2] + (-1, cache_v.shape[4]))
    cache_ikt = jnp.swapaxes(cache_idx_k, 2, 3)
    p_states, s_states = [], []
    for l in range(depth):
        w = _layer_weights(l, w_in, rnn_conv_w, rnn_conv_b, lru_wa, lru_ba, lru_wx, lru_bx, lru_lambda,
                           w_branch_a, w_branch_b, w_out, ln1_g, ln1_b, ffn_w_up, ffn_conv_w, ffn_conv_b,
                           ffn_w_down, ln2_g, ln2_b)
        xp, st = _prompt_layer(xp, w, alpha)
        p_states.append(st)
        xs, st = _sample_layer(xs, w, alpha, l, page_table, cache_ikt, cache_k3, cache_v3,
                               state_lru_h[l], state_lru_conv[l], state_ffn_conv[l], dbsz)
        s_states.append(st)
    stack = lambda states, n: jnp.stack([st[n] for st in states])
    return ((xp[:, N_META:], _from_time_major(xs, dbsz))
            + tuple(stack(p_states, n) for n in range(6))
            + tuple(stack(s_states, n) for n in range(6)))
```

```python
import functools

import jax
import jax.numpy as jnp
from jax import lax
from jax.experimental import pallas as pl
from jax.experimental.pallas import tpu as pltpu

f32 = jnp.float32

N_META = 16
N_RNN_BLOCKS = 8
LRU_C = 8.0
N_HEADS = 8
N_KV_HEADS = 4
KV_GROUP = N_HEADS // N_KV_HEADS
N_IDX_HEADS = 8
TOPK_MAX = 256
PAGE_SIZE = 128
LN_EPS = 1e-5

NEG = -1e30
BIG = 1e30
NO_TIE_LIMIT = 1e9
LOG2E = 1.4426950408889634
F32_TINY = float(jnp.finfo(jnp.float32).tiny)

MXU_DTYPE = jnp.bfloat16
ACT_DTYPE = jnp.bfloat16

V7X_VMEM_BYTES = 64 * 1024 * 1024
VMEM_LIMIT = V7X_VMEM_BYTES * 7 // 8
LANES = 128
SUBLANES = 8

ATTN_QB = 256
ATTN_KC = 256
ATTN_KC2 = 1024
COUNT_CHAINS = 8
COUNT_ROWS = COUNT_CHAINS * SUBLANES
PIPE_SLOTS = 2
DENOM_ROWS = 16
DENOM_FLOOR = 2.0 ** -90
MAX_BISECT = 4096
HI_MARGIN = 2.0 ** -20
FIRST_PROBE_FRACTION = 0.75
MAX_TIE_BISECT = 64
SAMPLE_PAGES_PER_STEP = 32


def _cparams(n_grid):
    return pltpu.CompilerParams(
        dimension_semantics=("arbitrary",) * n_grid, vmem_limit_bytes=VMEM_LIMIT)


def _resident(shape):
    nd = len(shape)
    return pl.BlockSpec(tuple(shape), lambda *_: (0,) * nd, pipeline_mode=pl.Buffered(1))


def _largest_tile(n, limit, multiple):
    best = None
    for d in range(multiple, min(n, limit) + 1, multiple):
        if n % d == 0:
            best = d
    assert best is not None, (n, limit, multiple)
    return best


def _gelu(x):
    return jax.nn.gelu(x, approximate=True)


def _sigmoid(x):
    return 0.5 * jnp.tanh(0.5 * x) + 0.5


def _softplus(x):
    return jnp.maximum(x, 0.0) + jnp.log1p(jnp.exp(-jnp.abs(x)))


def _layer_norm(y, g, b):
    mu = jnp.mean(y, axis=-1, keepdims=True)
    d = y - mu
    var = jnp.mean(d * d, axis=-1, keepdims=True)
    return d * lax.rsqrt(var + LN_EPS) * g + b


def _causal_conv(u, carry, w, b, tstride):
    width = w.shape[0]
    rows = u.shape[0]
    y = b + w[width - 1:width, :] * u
    if tstride % SUBLANES == 0:
        ext = jnp.concatenate([carry, u], axis=0)
        for j in range(width - 1):
            y = y + w[j:j + 1, :] * ext[j * tstride:j * tstride + rows, :]
        return y, ext[rows:, :]
    assert tstride == 1 and carry.shape[0] == SUBLANES
    row = lax.broadcasted_iota(jnp.int32, (SUBLANES, u.shape[1]), 0)
    for s in range(1, width):
        rolled = pltpu.roll(u, s, axis=0)
        head = jnp.where(row < s, pltpu.roll(carry, s, axis=0), rolled[0:SUBLANES, :])
        shifted = jnp.concatenate([head, rolled[SUBLANES:, :]], axis=0)
        y = y + w[width - 1 - s:width - s, :] * shifted
    return y, u[rows - SUBLANES:, :]


def _proj_kernel(x_ref, *refs, n_outs, n_tr, scales, t_valid, tile, kc):
    n_w = len(n_outs) + n_tr
    w_refs = refs[:n_w]
    o_refs = list(refs[n_w:])
    t0 = pl.program_id(1) * tile
    row = lax.broadcasted_iota(jnp.int32, (tile, 1), 0) + t0
    xb = jnp.where(row < t_valid, x_ref[0], 0.0).astype(MXU_DTYPE)
    for n, n_out in enumerate(n_outs):
        y = jnp.dot(xb, w_refs[n][...], preferred_element_type=f32)
        if scales[n] != 1.0:
            y = y * scales[n]
        for _ in range(n_out):
            o = o_refs.pop(0)
            heads = y.shape[1] // o.shape[2]
            if heads == 1:
                o[0] = y.astype(o.dtype)
            else:
                for g in range(heads):
                    o[0, pl.ds(g, tile, stride=heads), :] = (
                        y[:, g * o.shape[2]:(g + 1) * o.shape[2]].astype(o.dtype))
    for n in range(len(n_outs), n_w):
        y = lax.dot_general(w_refs[n][...], xb, (((1,), (1,)), ((), ())),
                            preferred_element_type=f32)
        if scales[n] != 1.0:
            y = y * scales[n]
        o = o_refs.pop(0)
        if len(o.shape) == 4:
            for cc in range(tile // kc):
                o[0, cc] = y[:, cc * kc:(cc + 1) * kc].astype(o.dtype)
        else:
            o[0] = y.astype(o.dtype)


def _project(x, std, tr, *, tile, t_pad, kc):
    bsz, t_len, d = x.shape
    n_t = t_pad // tile
    in_specs = [pl.BlockSpec((1, tile, d), lambda b, t: (b, t, 0))]
    out_shapes, out_specs, scales = [], [], []
    for w, scale, outs in std:
        in_specs.append(_resident(w.shape))
        n = w.shape[1]
        for dtype, padded, heads in outs:
            rows = t_pad if padded else t_len
            out_shapes.append(jax.ShapeDtypeStruct((bsz, rows * heads, n // heads), dtype))
            out_specs.append(pl.BlockSpec((1, tile * heads, n // heads), lambda b, t: (b, t, 0)))
        scales.append(scale)
    for w, scale, dtype, chunked in tr:
        in_specs.append(_resident(w.shape))
        n = w.shape[0]
        if chunked:
            out_shapes.append(jax.ShapeDtypeStruct((bsz, t_pad // kc, n, kc), dtype))
            out_specs.append(pl.BlockSpec((1, tile // kc, n, kc), lambda b, t: (b, t, 0, 0)))
        else:
            out_shapes.append(jax.ShapeDtypeStruct((bsz, n, t_pad), dtype))
            out_specs.append(pl.BlockSpec((1, n, tile), lambda b, t: (b, 0, t)))
        scales.append(scale)
    kern = functools.partial(_proj_kernel, n_outs=tuple(len(outs) for _, _, outs in std), n_tr=len(tr),
                             scales=tuple(scales), t_valid=t_len, tile=tile, kc=kc)
    return pl.pallas_call(
        kern, grid=(bsz, n_t), in_specs=in_specs, out_specs=out_specs, out_shape=out_shapes,
        compiler_params=_cparams(2), name="proj",
    )(x, *[w for w, *_ in std], *[w for w, *_ in tr])


def _lru_gates(xc, wax_ref, ba, bx, lam):
    blk = wax_ref.shape[1]
    xcb = xc.astype(MXU_DTYPE)
    rs, gs = [], []
    for n in range(wax_ref.shape[0]):
        y = jnp.dot(xcb[:, n * blk:(n + 1) * blk], wax_ref[n], preferred_element_type=f32)
        rs.append(y[:, :blk])
        gs.append(y[:, blk:])
    r = _sigmoid(jnp.concatenate(rs, axis=1) + ba)
    i = _sigmoid(jnp.concatenate(gs, axis=1) + bx)
    log_a = (-LRU_C) * r * _softplus(-lam)
    a = jnp.exp(log_a)
    th = jnp.tanh(log_a)
    one_minus_a2 = -2.0 * th / (1.0 - th)
    root = one_minus_a2 * lax.rsqrt(jnp.maximum(one_minus_a2, F32_TINY))
    return a, root * (i * xc)


def _scan8(a8, b8, h_prev):
    row = lax.broadcasted_iota(jnp.int32, a8.shape, 0)
    a, b = a8, b8
    for s in (1, 2, 4):
        ar = pltpu.roll(a, s, axis=0)
        br = pltpu.roll(b, s, axis=0)
        m = row >= s
        b = jnp.where(m, a * br + b, b)
        a = jnp.where(m, a * ar, a)
    return a * h_prev + b


def _rglru_kernel(xr_ref, gr_ref, h0_ref, buf_ref, cw_ref, cb_ref, wax_ref, ba_ref, bx_ref, lam_ref,
                  g_ref, hl_ref, nbuf_ref, a_s, b_s, hc_s, cc_s, *, tstride):
    @pl.when(pl.program_id(1) == 0)
    def _():
        cc_s[...] = buf_ref[0]
        hc_s[...] = jnp.broadcast_to(h0_ref[0], hc_s.shape)

    xr = xr_ref[0]
    rows = xr.shape[0]
    xc, ncarry = _causal_conv(xr, cc_s[...], cw_ref[...], cb_ref[...], tstride)
    cc_s[...] = ncarry
    nbuf_ref[0] = ncarry
    a, inp = _lru_gates(xc, wax_ref, ba_ref[...], bx_ref[...], lam_ref[...])
    if tstride == 1:
        a_s[...] = a
        b_s[...] = inp

        def body(k, h_prev):
            r0 = pl.multiple_of(k * SUBLANES, SUBLANES)
            h = _scan8(a_s[pl.ds(r0, SUBLANES), :], b_s[pl.ds(r0, SUBLANES), :], h_prev)
            b_s[pl.ds(r0, SUBLANES), :] = h
            return jnp.broadcast_to(h[SUBLANES - 1:SUBLANES, :], h.shape)

        h_last = lax.fori_loop(0, rows // SUBLANES, body, hc_s[...])
        hc_s[...] = h_last
        hl_ref[0] = h_last[0:1, :]
        hs = b_s[...]
    else:
        h = hc_s[...]
        pieces = []
        for s in range(rows // tstride):
            h = a[s * tstride:(s + 1) * tstride, :] * h + inp[s * tstride:(s + 1) * tstride, :]
            pieces.append(h)
        hc_s[...] = h
        hl_ref[0] = h
        hs = jnp.concatenate(pieces, axis=0)
    g_ref[0] = (hs * _gelu(gr_ref[0].astype(f32))).astype(g_ref.dtype)


def _rglru(xr, gr, h0, buf, cw, cb, wax, ba, bx, lam, *, tile, tstride):
    bsz, t_len, c = xr.shape
    hr, cr = h0.shape[1], buf.shape[1]
    hrows = SUBLANES if tstride == 1 else hr
    row_spec = pl.BlockSpec((1, tile, c), lambda b, t: (b, t, 0))
    scratch = [pltpu.VMEM((tile, c), f32), pltpu.VMEM((tile, c), f32),
               pltpu.VMEM((hrows, c), f32), pltpu.VMEM((cr, c), f32)]
    return pl.pallas_call(
        functools.partial(_rglru_kernel, tstride=tstride),
        grid=(bsz, t_len // tile),
        in_specs=[row_spec, row_spec,
                  pl.BlockSpec((1, hr, c), lambda b, t: (b, 0, 0)),
                  pl.BlockSpec((1, cr, c), lambda b, t: (b, 0, 0)),
                  _resident(cw.shape), _resident(cb.shape), _resident(wax.shape),
                  _resident(ba.shape), _resident(bx.shape), _resident(lam.shape)],
        out_specs=[row_spec,
                   pl.BlockSpec((1, hr, c), lambda b, t: (b, 0, 0)),
                   pl.BlockSpec((1, cr, c), lambda b, t: (b, 0, 0))],
        out_shape=[jax.ShapeDtypeStruct((bsz, t_len, c), ACT_DTYPE),
                   jax.ShapeDtypeStruct((bsz, hr, c), f32),
                   jax.ShapeDtypeStruct((bsz, cr, c), f32)],
        scratch_shapes=scratch, compiler_params=_cparams(2), name="rglru",
    )(xr, gr, h0, buf, cw, cb, wax, ba, bx, lam)


def _select_threshold(count_ge, count_tie, lo0, hi0, n_allowed, n_keys, topk):
    kf = float(topk)

    def n_active(done):
        return jnp.sum(jnp.where(done, 0.0, 1.0))

    lo = lo0
    hi_start = hi0 + (jnp.abs(hi0) * HI_MARGIN + F32_TINY)
    cnt_lo = n_allowed
    cnt_hi = jnp.zeros_like(lo0)
    stalled = jnp.zeros_like(lo0)
    frac0 = jnp.where(n_allowed > 4.0 * kf, FIRST_PROBE_FRACTION, 0.5)

    def bis_cond(st):
        return jnp.logical_and(st[-1] > 0.0, st[-2] < MAX_BISECT)

    def bis_body(st):
        lo, hi, cnt_lo, cnt_hi, stalled, frac, it, _ = st
        live = jnp.logical_and(cnt_lo > kf, stalled <= 0.0)
        n_live = jnp.sum(jnp.where(live, 1.0, 0.0))
        mid = lo + (hi - lo) * frac
        mid = jnp.where(jnp.logical_and(lo < 0.0, hi > 0.0), 0.0, mid)
        mid = jnp.where(jnp.logical_and(lo == 0.0, hi > F32_TINY), F32_TINY, mid)
        mid = jnp.where(jnp.logical_and(hi == 0.0, lo < -F32_TINY), -F32_TINY, mid)
        c = count_ge(mid)
        up = jnp.logical_and(live, c >= kf)
        dn = jnp.logical_and(live, c < kf)
        stall_now = jnp.logical_and(live, jnp.logical_or(mid <= lo, mid >= hi))
        lo = jnp.where(up, mid, lo)
        cnt_lo = jnp.where(up, c, cnt_lo)
        hi = jnp.where(dn, mid, hi)
        cnt_hi = jnp.where(dn, c, cnt_hi)
        stalled = jnp.where(stall_now, 1.0, stalled)
        return lo, hi, cnt_lo, cnt_hi, stalled, jnp.full_like(frac, 0.5), it + 1, n_live

    st = (lo, hi_start, cnt_lo, cnt_hi, stalled, frac0, jnp.int32(0), n_active(cnt_lo <= kf))
    lo, _, cnt_lo, cnt_hi, _, _, _, _ = lax.while_loop(bis_cond, bis_body, st)
    t = lo

    need = cnt_lo > kf
    want = kf - cnt_hi

    def tie_cond(st):
        return jnp.logical_and(st[-1] > 0.0, st[-2] < MAX_TIE_BISECT)

    def tie_body(st):
        jl, jh, jf, found, it, _ = st
        live = found <= 0.0
        jm = jnp.floor((jl + jh) * 0.5)
        c = count_tie(t, jm)
        hit = jnp.logical_and(live, c == want)
        jf = jnp.where(hit, jm, jf)
        found = jnp.where(hit, 1.0, found)
        jl = jnp.where(jnp.logical_and(live, c < want), jm, jl)
        jh = jnp.where(jnp.logical_and(live, c > want), jm, jh)
        return jl, jh, jf, found, it + 1, n_active(found > 0.0)

    found0 = jnp.where(need, 0.0, 1.0)
    st = (jnp.zeros_like(t), jnp.zeros_like(t) + n_keys, jnp.full_like(t, NO_TIE_LIMIT), found0,
          jnp.int32(0), n_active(found0 > 0.0))
    _, _, jf, _, _, _ = lax.while_loop(tie_cond, tie_body, st)
    return t, jnp.where(need, jf, NO_TIE_LIMIT)


def _attn_prompt_kernel(qT_ref, qiT_ref, wiT_ref, kb_ref, vT_ref, kib_ref, o_ref,
                        s_ref, q2_ref, acc_ref, m_ref, sa_ref, pb_ref, al_ref,
                        *, qb, kc, kc2, topk):
    i = pl.program_id(1)
    n_ih = wiT_ref.shape[1]
    di = qiT_ref.shape[1] // n_ih
    n_grp, dh = q2_ref.shape[0], q2_ref.shape[1]
    grp = q2_ref.shape[2] // qb
    c_last = (i * qb) // kc
    n_ch = c_last + 1
    n_ch2 = (n_ch * kc + kc2 - 1) // kc2

    qi_all = jnp.concatenate([qiT_ref[0, h * di:(h + 1) * di, :] for h in range(n_ih)], axis=1)
    wi = wiT_ref[0]
    kio = lax.broadcasted_iota(jnp.int32, (kc, qb), 0)
    qio = lax.broadcasted_iota(jnp.int32, (kc, qb), 1)

    def chunk_scores(c):
        k0 = pl.multiple_of(c * kc, kc)
        rel = jnp.dot(kib_ref[0, pl.ds(k0, kc), :], qi_all, preferred_element_type=f32)
        sc = wi[0:1, :] * jnp.maximum(rel[:, 0:qb], 0.0)
        for h in range(1, n_ih):
            sc = sc + wi[h:h + 1, :] * jnp.maximum(rel[:, h * qb:(h + 1) * qb], 0.0)
        return k0, sc

    def fold(x, op):
        return op(x.reshape(kc // SUBLANES, SUBLANES, qb), axis=0)

    def p1_body(k, carry):
        vmax, vmin = carry
        for c in (2 * k, jnp.minimum(2 * k + 1, c_last - 1)):
            k0, sc = chunk_scores(c)
            s_ref[pl.ds(k0, kc), :] = sc
            vmax = jnp.maximum(vmax, fold(sc, jnp.max))
            vmin = jnp.minimum(vmin, fold(sc, jnp.min))
        return vmax, vmin

    vmax, vmin = lax.fori_loop(
        0, (c_last + 1) // 2, p1_body,
        (jnp.full((SUBLANES, qb), NEG, f32), jnp.full((SUBLANES, qb), BIG, f32)))
    k0, sc = chunk_scores(c_last)
    allowed = (kio + k0) <= (qio + i * qb)
    s_ref[pl.ds(k0, kc), :] = jnp.where(allowed, sc, NEG)
    vmax = jnp.maximum(vmax, fold(jnp.where(allowed, sc, NEG), jnp.max))
    vmin = jnp.minimum(vmin, fold(jnp.where(allowed, sc, BIG), jnp.min))

    def fill_body(c, carry):
        s_ref[pl.ds(pl.multiple_of(c * kc, kc), kc), :] = jnp.full((kc, qb), NEG, f32)
        return carry

    lax.fori_loop(n_ch, n_ch2 * (kc2 // kc), fill_body, 0)

    def count_slabs(hit_fn):
        def body(c, acc):
            r0 = pl.multiple_of(c * kc2, kc2)
            for s in range(kc2 // COUNT_ROWS):
                r = pl.multiple_of(r0 + s * COUNT_ROWS, COUNT_ROWS)
                acc = jnp.where(hit_fn(s_ref[pl.ds(r, COUNT_ROWS), :], r), acc + 1.0, acc)
            return acc
        acc = lax.fori_loop(0, n_ch2, body, jnp.zeros((COUNT_ROWS, qb), f32))
        return jnp.sum(acc, axis=0, keepdims=True)

    def count_ge(x):
        xb = jnp.broadcast_to(x, (COUNT_ROWS, qb))
        return count_slabs(lambda blk, r: blk >= xb)

    def count_tie(t, j):
        tb = jnp.broadcast_to(t, (COUNT_ROWS, qb))
        row = lax.broadcasted_iota(jnp.int32, (COUNT_ROWS, qb), 0).astype(f32)

        def hit(blk, r):
            return jnp.logical_and(blk == tb, row < j - r.astype(f32))
        return count_slabs(hit)

    n_allowed = (lax.broadcasted_iota(jnp.int32, (1, qb), 1) + (i * qb + 1)).astype(f32)
    t, jsel = _select_threshold(
        count_ge, count_tie,
        jnp.min(vmin, axis=0, keepdims=True), jnp.max(vmax, axis=0, keepdims=True),
        n_allowed, (n_ch * kc).astype(f32), topk)

    for g in range(n_grp):
        q2_ref[g] = jnp.concatenate(
            [qT_ref[0, (g * grp + j) * dh:(g * grp + j + 1) * dh, :] for j in range(grp)], axis=1)
    ones_rows = jnp.ones((DENOM_ROWS, kc), pb_ref.dtype)

    def attend(exact):
        m_ref[...] = jnp.full(m_ref.shape, NEG, f32)
        acc_ref[...] = jnp.zeros(acc_ref.shape, f32)

        def qk_stage(c, slot, g):
            k0 = pl.multiple_of(jnp.minimum(c, n_ch - 1) * kc, kc)
            sa_ref[slot, g] = jnp.dot(kb_ref[0, pl.ds(k0, kc), g * dh:(g + 1) * dh], q2_ref[g],
                                      preferred_element_type=f32)

        def selection_mask(c):
            valid = c < n_ch
            k0 = pl.multiple_of(jnp.minimum(c, n_ch - 1) * kc, kc)
            blk = s_ref[pl.ds(k0, kc), :]
            kidx = (kio + k0).astype(f32)
            t_c = jnp.where(valid, t, BIG)
            j_c = jnp.where(valid, jsel, -1.0)
            sel = jnp.logical_or(blk > t_c, jnp.logical_and(blk == t_c, kidx < j_c))
            if exact:
                return jnp.concatenate([jnp.where(sel, 0.0, NEG)] * grp, axis=1)
            return jnp.concatenate([jnp.where(sel, 1.0, 0.0).astype(pb_ref.dtype)] * grp, axis=1)

        def softmax_stage(mask, slot, g):
            st = sa_ref[slot, g]
            if exact:
                st = st + mask
            m_old = m_ref[g]
            m_new = jnp.maximum(m_old, jnp.max(st, axis=0, keepdims=True))
            p = jnp.exp2(st - m_new).astype(pb_ref.dtype)
            pb_ref[slot, g] = p if exact else p * mask
            al_ref[slot, g] = jnp.exp2(m_old - m_new)
            m_ref[g] = m_new

        def pv_stage(c, slot, g):
            cc = jnp.clip(c, 0, n_ch - 1)
            v_aug = jnp.concatenate([vT_ref[0, cc, g * dh:(g + 1) * dh, :], ones_rows], axis=0)
            pv = jnp.dot(v_aug, pb_ref[slot, g], preferred_element_type=f32)
            acc_ref[g] = al_ref[slot, g] * acc_ref[g] + pv

        lag = PIPE_SLOTS // 2
        for c in range(lag):
            for g in range(n_grp):
                qk_stage(c, c, g)
        for slot in range(lag, PIPE_SLOTS):
            pb_ref[slot] = jnp.zeros(pb_ref.shape[1:], pb_ref.dtype)
            al_ref[slot] = jnp.ones(al_ref.shape[1:], f32)

        def p3_body(k, carry):
            for slot in range(PIPE_SLOTS):
                c = PIPE_SLOTS * k + slot
                mask = selection_mask(c)
                for g in range(n_grp):
                    pv_stage(c - lag, (slot + lag) % PIPE_SLOTS, g)
                    qk_stage(c + lag, (slot + lag) % PIPE_SLOTS, g)
                    softmax_stage(mask, slot, g)
            return carry

        lax.fori_loop(0, (n_ch + lag + PIPE_SLOTS - 1) // PIPE_SLOTS, p3_body, 0)

    attend(exact=False)
    denom_min = jnp.min(acc_ref[:, dh:dh + 1, :])

    @pl.when(jnp.logical_not(denom_min >= DENOM_FLOOR))
    def _():
        attend(exact=True)

    for g in range(n_grp):
        o = acc_ref[g, 0:dh, :] / acc_ref[g, dh:dh + 1, :]
        for j in range(grp):
            h = g * grp + j
            o_ref[0, :, h * dh:(h + 1) * dh] = o[:, j * qb:(j + 1) * qb].T.astype(o_ref.dtype)


def _attn_prompt(qT, qiT, wiT, kb, vT4, kib, *, t_len, topk):
    bsz, dq, t_pad = qT.shape
    qb, kc, kc2 = ATTN_QB, ATTN_KC, ATTN_KC2
    dkv = kb.shape[2]
    dh = dq // N_HEADS
    s_rows = -(-t_pad // kc2) * kc2
    kern = functools.partial(_attn_prompt_kernel, qb=qb, kc=kc, kc2=kc2, topk=topk)
    return pl.pallas_call(
        kern, grid=(bsz, -(-t_len // qb)),
        in_specs=[pl.BlockSpec((1, dq, qb), lambda b, i: (b, 0, i)),
                  pl.BlockSpec((1, qiT.shape[1], qb), lambda b, i: (b, 0, i)),
                  pl.BlockSpec((1, wiT.shape[1], qb), lambda b, i: (b, 0, i)),
                  pl.BlockSpec((1, t_pad, dkv), lambda b, i: (b, 0, 0), pipeline_mode=pl.Buffered(1)),
                  pl.BlockSpec((1,) + vT4.shape[1:], lambda b, i: (b, 0, 0, 0),
                               pipeline_mode=pl.Buffered(1)),
                  pl.BlockSpec((1, t_pad, kib.shape[2]), lambda b, i: (b, 0, 0),
                               pipeline_mode=pl.Buffered(1))],
        out_specs=pl.BlockSpec((1, qb, dq), lambda b, i: (b, i, 0)),
        out_shape=jax.ShapeDtypeStruct((bsz, t_pad, dq), ACT_DTYPE),
        scratch_shapes=[pltpu.VMEM((s_rows, qb), f32),
                        pltpu.VMEM((N_KV_HEADS, dh, KV_GROUP * qb), MXU_DTYPE),
                        pltpu.VMEM((N_KV_HEADS, dh + DENOM_ROWS, KV_GROUP * qb), f32),
                        pltpu.VMEM((N_KV_HEADS, 1, KV_GROUP * qb), f32),
                        pltpu.VMEM((PIPE_SLOTS, N_KV_HEADS, kc, KV_GROUP * qb), f32),
                        pltpu.VMEM((PIPE_SLOTS, N_KV_HEADS, kc, KV_GROUP * qb), MXU_DTYPE),
                        pltpu.VMEM((PIPE_SLOTS, N_KV_HEADS, 1, KV_GROUP * qb), f32)],
        compiler_params=_cparams(2), name="attn_prompt",
    )(qT, qiT, wiT, kb, vT4, kib)


def _sattn_select_kernel(pt_ref, qi_ref, wi_ref, kin_ref, *rest, pg, ns, topk, n_new):
    del pt_ref
    pages = rest[:pg]
    s_ref, t_ref, j_ref = rest[pg:]
    st = pl.program_id(1)
    pgk = pg * PAGE_SIZE
    qi = qi_ref[0]
    wi = wi_ref[0]
    n_ih = qi.shape[0] // n_new

    def scores(keys_t):
        rel = jnp.dot(qi, keys_t, preferred_element_type=f32)
        rel = jnp.maximum(rel, 0.0) * wi
        return jnp.sum(rel.reshape(n_ih, n_new, keys_t.shape[1]), axis=0)

    kp = jnp.concatenate([p[...] for p in pages], axis=1).astype(MXU_DTYPE)
    s_ref[0, st] = scores(kp)

    @pl.when(st == ns - 1)
    def _():
        n_pad = kin_ref.shape[2]
        sn = scores(kin_ref[0])
        lane = lax.broadcasted_iota(jnp.int32, (n_new, n_pad), 1)
        qrow = lax.broadcasted_iota(jnp.int32, (n_new, n_pad), 0)
        ok_new = lane <= qrow
        s_ref[0, ns] = jnp.concatenate(
            [jnp.where(ok_new, sn, NEG), jnp.full((n_new, pgk - n_pad), NEG, f32)], axis=1)

        s_all = s_ref[0]
        past = s_all[:-1]
        hi0 = jnp.maximum(jnp.max(jnp.max(past, axis=0), axis=1, keepdims=True),
                          jnp.max(jnp.where(ok_new, sn, NEG), axis=1, keepdims=True))
        lo0 = jnp.minimum(jnp.min(jnp.min(past, axis=0), axis=1, keepdims=True),
                          jnp.min(jnp.where(ok_new, sn, BIG), axis=1, keepdims=True))
        kidx = (lax.broadcasted_iota(jnp.int32, s_all.shape, 0) * pgk
                + lax.broadcasted_iota(jnp.int32, s_all.shape, 2)).astype(f32)

        def total(x):
            return jnp.sum(jnp.sum(x, axis=0), axis=1, keepdims=True)

        def count_ge(x):
            return total(jnp.where(s_all >= x[None], 1.0, 0.0))

        def count_tie(t, j):
            hit = jnp.logical_and(s_all == t[None], kidx < j[None])
            return total(jnp.where(hit, 1.0, 0.0))

        q1 = lax.broadcasted_iota(jnp.int32, (n_new, 1), 0)
        n_allowed = (q1 + (ns * pgk + 1)).astype(f32)
        t, jsel = _select_threshold(count_ge, count_tie, lo0, hi0, n_allowed, float(ns * pgk + n_pad),
                                    topk)
        t_ref[0] = jnp.broadcast_to(t, t_ref.shape[1:])
        j_ref[0] = jnp.broadcast_to(jsel, j_ref.shape[1:])


def _sattn_attend_kernel(pt_ref, q_ref, s_ref, snew_ref, t_ref, j_ref, knew_ref, vnew_ref, *rest,
                         pg, ns, n_new):
    del pt_ref
    kpages, vpages = rest[:pg], rest[pg:2 * pg]
    o_ref, m_s, l_s, acc_s = rest[2 * pg:]
    st = pl.program_id(1)
    pgk = pg * PAGE_SIZE
    q = q_ref[0]
    dh = q.shape[1]
    rows_g = KV_GROUP * n_new
    t = t_ref[0][:, 0:1]
    jsel = j_ref[0][:, 0:1]

    @pl.when(st == 0)
    def _():
        m_s[...] = jnp.full(m_s.shape, NEG, f32)
        l_s[...] = jnp.zeros(l_s.shape, f32)
        acc_s[...] = jnp.zeros(acc_s.shape, f32)

    def update(sc, base, keys_of, vals_of):
        n = sc.shape[1]
        kidx = (lax.broadcasted_iota(jnp.int32, (n_new, n), 1) + base).astype(f32)
        sel = jnp.logical_or(sc > t, jnp.logical_and(sc == t, kidx < jsel))
        bias = jnp.where(sel, 0.0, NEG)
        bias = jnp.concatenate([bias] * KV_GROUP, axis=0)
        s = jnp.concatenate(
            [lax.dot_general(q[g * rows_g:(g + 1) * rows_g, :], keys_of(g), (((1,), (1,)), ((), ())),
                             preferred_element_type=f32) + bias for g in range(N_KV_HEADS)], axis=0)
        m_old = m_s[...]
        m_new = jnp.maximum(m_old, jnp.max(s, axis=1, keepdims=True))
        alpha = jnp.exp2(m_old - m_new)
        p = jnp.exp2(s - m_new)
        l_s[...] = alpha * l_s[...] + jnp.sum(p, axis=1, keepdims=True)
        m_s[...] = m_new
        pb = p.astype(MXU_DTYPE)
        pv = jnp.concatenate(
            [jnp.dot(pb[g * rows_g:(g + 1) * rows_g, :], vals_of(g), preferred_element_type=f32)
             for g in range(N_KV_HEADS)], axis=0)
        acc_s[...] = alpha * acc_s[...] + pv

    def paged(pages):
        def head_rows(g):
            return jnp.concatenate([p[pl.ds(g, PAGE_SIZE, stride=N_KV_HEADS), :] for p in pages],
                                   axis=0).astype(MXU_DTYPE)
        return head_rows

    def fresh(ref):
        return lambda g: ref[0][:, g * dh:(g + 1) * dh]

    update(s_ref[0, 0], st * pgk, paged(kpages), paged(vpages))

    @pl.when(st == ns - 1)
    def _():
        n_pad = knew_ref.shape[1]
        update(snew_ref[0, 0][:, :n_pad], ns * pgk, fresh(knew_ref), fresh(vnew_ref))
        o_ref[0] = (acc_s[...] / l_s[...]).astype(o_ref.dtype)


def _attn_sample(layer, page_table, cache_ikt, cache_k3, cache_v3, qi_hq, wi_hq, kin_t, q_hq, knew, vnew,
                 *, topk):
    bsz, n_pages = page_table.shape
    n_new = qi_hq.shape[1] // N_IDX_HEADS
    pg = _largest_tile(n_pages, SAMPLE_PAGES_PER_STEP, 1)
    ns = n_pages // pg
    pgk = pg * PAGE_SIZE
    di = cache_ikt.shape[2]
    dh = cache_k3.shape[3]
    dkv = knew.shape[2]
    n_pad = knew.shape[1]
    rows = q_hq.shape[1]

    def page_spec(shape, j):
        return pl.BlockSpec((None, None) + shape, lambda b, s, pt: (layer, pt[b, s * pg + j], 0, 0))

    s_all, thr, jsel = pl.pallas_call(
        functools.partial(_sattn_select_kernel, pg=pg, ns=ns, topk=topk, n_new=n_new),
        grid_spec=pltpu.PrefetchScalarGridSpec(
            num_scalar_prefetch=1, grid=(bsz, ns),
            in_specs=[pl.BlockSpec((1,) + qi_hq.shape[1:], lambda b, s, pt: (b, 0, 0)),
                      pl.BlockSpec((1,) + wi_hq.shape[1:], lambda b, s, pt: (b, 0, 0)),
                      pl.BlockSpec((1, di, n_pad), lambda b, s, pt: (b, 0, 0))]
            + [page_spec((di, PAGE_SIZE), j) for j in range(pg)],
            out_specs=[pl.BlockSpec((1, ns + 1, n_new, pgk), lambda b, s, pt: (b, 0, 0, 0)),
                       pl.BlockSpec((1, n_new, LANES), lambda b, s, pt: (b, 0, 0)),
                       pl.BlockSpec((1, n_new, LANES), lambda b, s, pt: (b, 0, 0))]),
        out_shape=[jax.ShapeDtypeStruct((bsz, ns + 1, n_new, pgk), f32),
                   jax.ShapeDtypeStruct((bsz, n_new, LANES), f32),
                   jax.ShapeDtypeStruct((bsz, n_new, LANES), f32)],
        compiler_params=_cparams(2), name="sattn_select",
    )(page_table, qi_hq, wi_hq, kin_t, *([cache_ikt] * pg))

    kv_page = (PAGE_SIZE * N_KV_HEADS, dh)
    return pl.pallas_call(
        functools.partial(_sattn_attend_kernel, pg=pg, ns=ns, n_new=n_new),
        grid_spec=pltpu.PrefetchScalarGridSpec(
            num_scalar_prefetch=1, grid=(bsz, ns),
            in_specs=[pl.BlockSpec((1, rows, dh), lambda b, s, pt: (b, 0, 0)),
                      pl.BlockSpec((1, 1, n_new, pgk), lambda b, s, pt: (b, s, 0, 0)),
                      pl.BlockSpec((1, 1, n_new, pgk), lambda b, s, pt: (b, ns, 0, 0)),
                      pl.BlockSpec((1, n_new, LANES), lambda b, s, pt: (b, 0, 0)),
                      pl.BlockSpec((1, n_new, LANES), lambda b, s, pt: (b, 0, 0)),
                      pl.BlockSpec((1, n_pad, dkv), lambda b, s, pt: (b, 0, 0)),
                      pl.BlockSpec((1, n_pad, dkv), lambda b, s, pt: (b, 0, 0))]
            + [page_spec(kv_page, j) for j in range(pg)] * 2,
            out_specs=pl.BlockSpec((1, rows, dh), lambda b, s, pt: (b, 0, 0)),
            scratch_shapes=[pltpu.VMEM((rows, 1), f32), pltpu.VMEM((rows, 1), f32),
                            pltpu.VMEM((rows, dh), f32)]),
        out_shape=jax.ShapeDtypeStruct((bsz, rows, dh), ACT_DTYPE),
        compiler_params=_cparams(2), name="sattn_attend",
    )(page_table, q_hq, s_all, s_all, thr, jsel, knew, vnew, *([cache_k3] * pg), *([cache_v3] * pg))


def _merge_kernel(g_ref, o_ref, ga_ref, gb_ref, x_ref, wa_ref, wb_ref, wo_ref, lg_ref, lb_ref, out_ref,
                  *, alpha):
    ya = jnp.dot(g_ref[0].astype(MXU_DTYPE), wa_ref[...], preferred_element_type=f32)
    yb = jnp.dot(o_ref[0].astype(MXU_DTYPE), wb_ref[...], preferred_element_type=f32)
    mixed = _sigmoid(ga_ref[0].astype(f32)) * ya + _sigmoid(gb_ref[0].astype(f32)) * yb
    mix = jnp.dot(mixed.astype(MXU_DTYPE), wo_ref[...], preferred_element_type=f32)
    out_ref[0] = _layer_norm(alpha * x_ref[0] + mix, lg_ref[...], lb_ref[...])


def _merge(g, o, ga, gb, x, wa, wb, wo, lg, lb, *, tile, alpha):
    bsz, t_len, d = x.shape
    row_spec = pl.BlockSpec((1, tile, d), lambda b, t: (b, t, 0))
    return pl.pallas_call(
        functools.partial(_merge_kernel, alpha=alpha), grid=(bsz, t_len // tile),
        in_specs=[row_spec] * 5 + [_resident(a.shape) for a in (wa, wb, wo, lg, lb)],
        out_specs=row_spec, out_shape=jax.ShapeDtypeStruct((bsz, t_len, d), f32),
        compiler_params=_cparams(2), name="merge",
    )(g, o, ga, gb, x, wa, wb, wo, lg, lb)


def _ffn_kernel(x_ref, buf_ref, wup_ref, cw_ref, cb_ref, wdn_ref, lg_ref, lb_ref, out_ref, nbuf_ref,
                carry_ref, act_ref, *, alpha, tstride, ck):
    @pl.when(pl.program_id(1) == 0)
    def _():
        carry_ref[...] = buf_ref[0]

    x = x_ref[0]
    xb = x.astype(MXU_DTYPE)
    d_ff = wdn_ref.shape[0]
    for c in range(d_ff // ck):
        halves = []
        for off in (c * ck, d_ff + c * ck):
            u = jnp.dot(xb, wup_ref[:, off:off + ck], preferred_element_type=f32)
            uc, ncarry = _causal_conv(u, carry_ref[:, off:off + ck], cw_ref[:, off:off + ck],
                                      cb_ref[:, off:off + ck], tstride)
            carry_ref[:, off:off + ck] = ncarry
            halves.append(uc)
        act_ref[:, c * ck:(c + 1) * ck] = (_gelu(halves[0]) * halves[1]).astype(act_ref.dtype)
    down = jnp.dot(act_ref[...], wdn_ref[...], preferred_element_type=f32)
    out_ref[0] = _layer_norm(alpha * x + down, lg_ref[...], lb_ref[...])
    nbuf_ref[0] = carry_ref[...]


def _ffn(x, buf, wup, cw, cb, wdn, lg, lb, *, tile, alpha, tstride):
    bsz, t_len, d = x.shape
    cr, f2 = buf.shape[1], buf.shape[2]
    row_spec = pl.BlockSpec((1, tile, d), lambda b, t: (b, t, 0))
    buf_spec = pl.BlockSpec((1, cr, f2), lambda b, t: (b, 0, 0))
    return pl.pallas_call(
        functools.partial(_ffn_kernel, alpha=alpha, tstride=tstride, ck=512),
        grid=(bsz, t_len // tile),
        in_specs=[row_spec, buf_spec] + [_resident(a.shape) for a in (wup, cw, cb, wdn, lg, lb)],
        out_specs=[row_spec, buf_spec],
        out_shape=[jax.ShapeDtypeStruct((bsz, t_len, d), f32),
                   jax.ShapeDtypeStruct((bsz, cr, f2), f32)],
        scratch_shapes=[pltpu.VMEM((cr, f2), f32), pltpu.VMEM((tile, wdn.shape[0]), MXU_DTYPE)],
        compiler_params=_cparams(2), name="ffn",
    )(x, buf, wup, cw, cb, wdn, lg, lb)


def _layer_weights(l, w_in, rnn_conv_w, rnn_conv_b, lru_wa, lru_ba, lru_wx, lru_bx, lru_lambda,
                   w_branch_a, w_branch_b, w_out, ln1_g, ln1_b, ffn_w_up, ffn_conv_w, ffn_conv_b,
                   ffn_w_down, ln2_g, ln2_b):
    d = w_in.shape[1]
    dh = d // N_HEADS
    di = (w_in.shape[2] - 4 * d - (N_HEADS + 2 * N_KV_HEADS) * dh - N_IDX_HEADS) // (N_IDX_HEADS + 1)
    sizes = (d, d, N_HEADS * dh, N_KV_HEADS * dh, N_KV_HEADS * dh, N_IDX_HEADS * di, di, N_IDX_HEADS, d, d)
    names = ("xr", "gr", "q", "k", "v", "qi", "ki", "wi", "ga", "gb")
    cols, off = {}, 0
    for name, size in zip(names, sizes):
        cols[name] = w_in[l][:, off:off + size].astype(MXU_DTYPE)
        off += size
    row = lambda v: v[l][None, :]
    return dict(
        cols=cols, dh=dh, di=di,
        cw=rnn_conv_w[l], cb=row(rnn_conv_b),
        wax=jnp.concatenate([lru_wa[l], lru_wx[l]], axis=2).astype(MXU_DTYPE),
        ba=row(lru_ba), bx=row(lru_bx), lam=row(lru_lambda),
        wa=w_branch_a[l].astype(MXU_DTYPE), wb=w_branch_b[l].astype(MXU_DTYPE),
        wo=w_out[l].astype(MXU_DTYPE), lg1=row(ln1_g), lb1=row(ln1_b),
        wup=ffn_w_up[l].astype(MXU_DTYPE), fcw=ffn_conv_w[l], fcb=row(ffn_conv_b),
        wdn=ffn_w_down[l].astype(MXU_DTYPE), lg2=row(ln2_g), lb2=row(ln2_b))


def _prompt_layer(x, w, alpha):
    bsz, t_len, d = x.shape
    dh, di, cols = w["dh"], w["di"], w["cols"]
    kc = ATTN_KC
    t_pad = -(-t_len // kc) * kc
    ptile = kc * _largest_tile(t_pad // kc, 3, 1)
    rtile = _largest_tile(t_len, 1024, 16)
    topk = min(TOPK_MAX, t_len // 4)
    one = lambda dtype: [(dtype, False, 1)]
    std = [(cols["xr"], 1.0, one(f32)), (cols["gr"], 1.0, one(ACT_DTYPE)),
           (cols["k"], 1.0, [(f32, False, N_KV_HEADS), (MXU_DTYPE, True, 1)]),
           (cols["v"], 1.0, [(f32, False, N_KV_HEADS)]),
           (cols["ki"], 1.0, [(f32, False, 1), (MXU_DTYPE, True, 1)]),
           (cols["ga"], 1.0, one(ACT_DTYPE)), (cols["gb"], 1.0, one(ACT_DTYPE))]
    tr = [(cols["q"].T, dh ** -0.5 * LOG2E, MXU_DTYPE, False), (cols["qi"].T, di ** -0.5, MXU_DTYPE, False),
          (cols["wi"].T, N_IDX_HEADS ** -0.5, f32, False), (cols["v"].T, 1.0, MXU_DTYPE, True)]
    xr, gr, k, kb, v, ki, kib, ga, gb, qT, qiT, wiT, vT4 = _project(
        x, std, tr, tile=ptile, t_pad=t_pad, kc=kc)

    c_rnn, w_rnn = xr.shape[2], w["cw"].shape[0]
    g, h_last, nbuf = _rglru(
        xr, gr, jnp.zeros((bsz, 1, c_rnn), f32), jnp.zeros((bsz, SUBLANES, c_rnn), f32),
        w["cw"], w["cb"], w["wax"], w["ba"], w["bx"], w["lam"], tile=rtile, tstride=1)
    o = _attn_prompt(qT, qiT, wiT, kb, vT4, kib, t_len=t_len, topk=topk)
    x1 = _merge(g, o, ga, gb, x, w["wa"], w["wb"], w["wo"], w["lg1"], w["lb1"], tile=rtile, alpha=alpha)
    f2, w_ffn = w["fcw"].shape[1], w["fcw"].shape[0]
    x2, fbuf = _ffn(x1, jnp.zeros((bsz, SUBLANES, f2), f32), w["wup"], w["fcw"], w["fcb"], w["wdn"],
                    w["lg2"], w["lb2"], tile=_largest_tile(t_len, 1024, 16), alpha=alpha, tstride=1)
    state = (k.reshape(bsz, t_len, N_KV_HEADS, dh), v.reshape(bsz, t_len, N_KV_HEADS, dh), ki,
             h_last[:, 0, :], nbuf[:, SUBLANES - (w_rnn - 1):, :], fbuf[:, SUBLANES - (w_ffn - 1):, :])
    return x2, state


def _to_time_major(a):
    a = jnp.swapaxes(a, 0, 1)
    return a.reshape((1, a.shape[0] * a.shape[1]) + a.shape[2:])


def _from_time_major(a, bsz):
    a = a.reshape((a.shape[1] // bsz, bsz) + a.shape[2:])
    return jnp.swapaxes(a, 0, 1)


def _sample_layer(x, w, alpha, layer, page_table, cache_ikt, cache_k3, cache_v3, h0, lru_buf, ffn_buf, bsz):
    rows, d = x.shape[1], x.shape[2]
    n_new = rows // bsz
    dh, di, cols = w["dh"], w["di"], w["cols"]
    past = page_table.shape[1] * PAGE_SIZE
    topk = min(TOPK_MAX, (past + n_new) // 4)
    one = lambda dtype: [(dtype, False, 1)]
    std = [(cols["xr"], 1.0, one(f32)), (cols["gr"], 1.0, one(ACT_DTYPE)),
           (cols["q"], dh ** -0.5 * LOG2E, one(MXU_DTYPE)), (cols["k"], 1.0, one(f32)),
           (cols["v"], 1.0, one(f32)), (cols["qi"], di ** -0.5, one(MXU_DTYPE)),
           (cols["ki"], 1.0, one(f32)), (cols["wi"], N_IDX_HEADS ** -0.5, one(f32)),
           (cols["ga"], 1.0, one(ACT_DTYPE)), (cols["gb"], 1.0, one(ACT_DTYPE))]
    xr, gr, q, k, v, qi, ki, wi, ga, gb = _project(x, std, [], tile=rows, t_pad=rows, kc=ATTN_KC)

    g, h_last, nbuf = _rglru(xr, gr, h0[None], _to_time_major(lru_buf), w["cw"], w["cb"], w["wax"],
                             w["ba"], w["bx"], w["lam"], tile=rows, tstride=bsz)

    def heads_major(a, n_h):
        a = _from_time_major(a, bsz).reshape(bsz, n_new, n_h, -1)
        return jnp.swapaxes(a, 1, 2).reshape(bsz, n_h * n_new, -1)

    def pad_new(a):
        a = _from_time_major(a, bsz).astype(MXU_DTYPE)
        return jnp.pad(a, ((0, 0), (0, LANES - n_new), (0, 0)))

    kin_t = jnp.swapaxes(pad_new(ki), 1, 2)
    o = _attn_sample(layer, page_table, cache_ikt, cache_k3, cache_v3,
                     heads_major(qi, N_IDX_HEADS), heads_major(wi, N_IDX_HEADS), kin_t,
                     heads_major(q, N_HEADS), pad_new(k), pad_new(v), topk=topk)
    o = jnp.swapaxes(o.reshape(bsz, N_HEADS, n_new, dh), 1, 2).reshape(bsz, n_new, N_HEADS * dh)
    o = _to_time_major(o)

    x1 = _merge(g, o, ga, gb, x, w["wa"], w["wb"], w["wo"], w["lg1"], w["lb1"], tile=rows, alpha=alpha)
    x2, fbuf = _ffn(x1, _to_time_major(ffn_buf), w["wup"], w["fcw"], w["fcb"], w["wdn"], w["lg2"],
                    w["lb2"], tile=rows, alpha=alpha, tstride=bsz)
    state = (_from_time_major(k, bsz).reshape(bsz, n_new, N_KV_HEADS, dh),
             _from_time_major(v, bsz).reshape(bsz, n_new, N_KV_HEADS, dh),
             _from_time_major(ki, bsz), h_last[0], _from_time_major(nbuf, bsz), _from_time_major(fbuf, bsz))
    return x2, state


def kernel(x_prompt, x_sample, cache_k, cache_v, cache_idx_k, state_lru_h, state_lru_conv, state_ffn_conv,
           page_table, meta_tokens, w_in, rnn_conv_w, rnn_conv_b, lru_wa, lru_ba, lru_wx, lru_bx, lru_lambda,
           w_branch_a, w_branch_b, w_out, ln1_g, ln1_b, ffn_w_up, ffn_conv_w, ffn_conv_b, ffn_w_down,
           ln2_g, ln2_b):
    depth = w_in.shape[0]
    alpha = (2.0 * depth) ** 0.25
    bsz, _, d = x_prompt.shape
    dbsz = x_sample.shape[0]
    meta = jnp.broadcast_to(meta_tokens.astype(x_prompt.dtype), (bsz, N_META, d))
    xp = jnp.concatenate([meta, x_prompt], axis=1)
    xs = _to_time_major(x_sample)
    cache_k3 = cache_k.reshape(cache_k.shape[:2] + (-1, cache_k.shape[4]))
    cache_v3 = cache_v.reshape(cache_v.shape[:2] + (-1, cache_v.shape[4]))
    cache_ikt = jnp.swapaxes(cache_idx_k, 2, 3)
    p_states, s_states = [], []
    for l in range(depth):
        w = _layer_weights(l, w_in, rnn_conv_w, rnn_conv_b, lru_wa, lru_ba, lru_wx, lru_bx, lru_lambda,
                           w_branch_a, w_branch_b, w_out, ln1_g, ln1_b, ffn_w_up, ffn_conv_w, ffn_conv_b,
                           ffn_w_down, ln2_g, ln2_b)
        xp, st = _prompt_layer(xp, w, alpha)
        p_states.append(st)
        xs, st = _sample_layer(xs, w, alpha, l, page_table, cache_ikt, cache_k3, cache_v3,
                               state_lru_h[l], state_lru_conv[l], state_ffn_conv[l], dbsz)
        s_states.append(st)
    stack = lambda states, n: jnp.stack([st[n] for st in states])
    return ((xp[:, N_META:], _from_time_major(xs, dbsz))
            + tuple(stack(p_states, n) for n in range(6))
            + tuple(stack(s_states, n) for n in range(6)))
```
